```python
import math
import jax, jax.numpy as jnp
from jax import lax
import numpy as np

D_MODEL = 1024
BATCH = 1
SEQ = 16384
DEPTH = 1

GLA_HEADS = 4
GLA_DK = 128
GLA_DV = 256
GLA_GATE_RANK = 16
GLA_GATE_NORMALIZER = 16.0
GLA_CHUNK = 64
SWA_HEADS = 16
SWA_KV_HEADS = 4
SWA_HEAD_DIM = 64
SWA_WINDOW = 128
SWA_BLOCK = 128
D_FF = 2816
MACARON_WEIGHT = 0.5
N_ADA = 9
LN_EPS = 1e-5
RMS_EPS = 1e-6
NEG_INF = -1e30
DEEPNORM_ALPHA = (2.0 * DEPTH) ** 0.25
DEEPNORM_BETA = (8.0 * DEPTH) ** -0.25
MIX_SIZES = (
    GLA_HEADS * GLA_DK,
    GLA_HEADS * GLA_DK,
    GLA_HEADS * GLA_DV,
    GLA_GATE_RANK,
    GLA_HEADS * GLA_DV,
    SWA_HEADS * SWA_HEAD_DIM,
    SWA_KV_HEADS * SWA_HEAD_DIM,
    SWA_KV_HEADS * SWA_HEAD_DIM,
    D_MODEL,
    D_MODEL,
)
MIX_VALUE_COLS = (2, 7)
MIX_WIDTH = sum(MIX_SIZES)

kernel_name = "hybrid_gla_swa_macaron_deepnorm_adaln"


def _split_offsets(sizes):
    offs, acc = [], 0
    for s in sizes[:-1]:
        acc += s
        offs.append(acc)
    return offs


def alibi_slopes(n):
    return 2.0 ** (-8.0 * jnp.arange(1, n + 1, dtype=jnp.float32) / n)


def layer_norm(x, g, b):
    xf = x.astype(jnp.float32)
    mu = jnp.mean(xf, axis=-1, keepdims=True)
    var = jnp.mean(jnp.square(xf - mu), axis=-1, keepdims=True)
    y = (xf - mu) * lax.rsqrt(var + LN_EPS) * g.astype(jnp.float32) + b.astype(jnp.float32)
    return y.astype(x.dtype)


def modulate(x, shift, scale):
    return x * (1.0 + scale[:, None, :]) + shift[:, None, :]


def swiglu(h, w_gate, w_up, w_down):
    return (jax.nn.silu(h @ w_gate) * (h @ w_up)) @ w_down


def gla_chunked(q, k, v, log_a):
    B, S, H, DK = q.shape
    DV = v.shape[-1]
    C = GLA_CHUNK
    nc = S // C

    def to_chunks(t):
        return jnp.moveaxis(t.reshape(B, nc, C, *t.shape[2:]), 1, 0)

    causal = jnp.tril(jnp.ones((C, C), dtype=bool))[None, :, :, None, None]

    def step(state, inp):
        qc, kc, vc, gc = inp
        b = jnp.cumsum(gc, axis=1)
        o_inter = jnp.einsum('bthk,bhkv->bthv', qc * jnp.exp(b), state)
        diff = b[:, :, None] - b[:, None, :]
        decay = jnp.where(causal, jnp.exp(jnp.where(causal, diff, 0.0)), 0.0)
        scores = jnp.einsum('bthk,bshk,btshk->btsh', qc, kc, decay)
        o_intra = jnp.einsum('btsh,bshv->bthv', scores, vc)
        b_last = b[:, -1]
        k_dec = kc * jnp.exp(b_last[:, None] - b)
        state = jnp.exp(b_last)[..., None] * state + jnp.einsum('bshk,bshv->bhkv', k_dec, vc)
        return state, o_inter + o_intra

    init = jnp.zeros((B, H, DK, DV), jnp.float32)
    _, o = lax.scan(step, init, (to_chunks(q), to_chunks(k), to_chunks(v), to_chunks(log_a)))
    return jnp.moveaxis(o, 0, 1).reshape(B, S, H, DV)


def swa_banded(q, k, v, sinks):
    B, S = q.shape[:2]
    BLK = SWA_BLOCK
    nb = S // BLK
    KV = SWA_KV_HEADS
    G = SWA_HEADS // SWA_KV_HEADS
    HD = SWA_HEAD_DIM
    qb = q.reshape(B, nb, BLK, KV, G, HD)
    pad = ((0, 0), (BLK, 0), (0, 0), (0, 0))
    kp = jnp.pad(k, pad).reshape(B, nb + 1, BLK, KV, HD)
    vp = jnp.pad(v, pad).reshape(B, nb + 1, BLK, KV, HD)
    kb = jnp.concatenate([kp[:, :-1], kp[:, 1:]], axis=2)
    vb = jnp.concatenate([vp[:, :-1], vp[:, 1:]], axis=2)
    scores = jnp.einsum('bnqkgd,bnskd->bnkgqs', qb, kb,
                        preferred_element_type=jnp.float32) * (HD ** -0.5)
    q_rel = jnp.arange(BLK)[:, None] + BLK
    k_rel = jnp.arange(2 * BLK)[None, :]
    dist = q_rel - k_rel
    band = (dist >= 0) & (dist < SWA_WINDOW)
    key_abs = (jnp.arange(nb)[:, None] - 1) * BLK + k_rel
    mask = band[None] & (key_abs >= 0)[:, None, :]
    slopes = alibi_slopes(SWA_HEADS).reshape(KV, G, 1, 1)
    scores = scores - slopes * dist.astype(jnp.float32)
    scores = jnp.where(mask[None, :, None, None], scores, NEG_INF)
    sink = sinks.astype(jnp.float32).reshape(1, 1, KV, G, 1, 1)
    m = jnp.maximum(jnp.max(scores, axis=-1, keepdims=True), sink)
    p = jnp.exp(scores - m)
    p = p / (jnp.sum(p, axis=-1, keepdims=True) + jnp.exp(sink - m))
    out = jnp.einsum('bnkgqs,bnskd->bnqkgd', p.astype(vb.dtype), vb)
    return out.reshape(B, S, SWA_HEADS * HD)


def hybrid_mixer(h, w_in, w_gla_gate_up, b_gla_gate, gla_norm_g, w_branch_gla,
                 swa_sinks, w_branch_swa, w_out):
    B, S, _ = h.shape
    f32 = jnp.float32
    proj = h @ w_in
    (gq, gk, gv, g_lr, g_out, sq, sk, sv, gate_a, gate_b) = jnp.split(
        proj, _split_offsets(MIX_SIZES), axis=-1)
    q = gq.reshape(B, S, GLA_HEADS, GLA_DK).astype(f32) * (GLA_DK ** -0.5)
    k = gk.reshape(B, S, GLA_HEADS, GLA_DK).astype(f32)
    v = gv.reshape(B, S, GLA_HEADS, GLA_DV).astype(f32)
    log_a = jax.nn.log_sigmoid((g_lr @ w_gla_gate_up + b_gla_gate).astype(f32))
    log_a = log_a.reshape(B, S, GLA_HEADS, GLA_DK) / GLA_GATE_NORMALIZER
    o = gla_chunked(q, k, v, log_a)
    o = o * lax.rsqrt(jnp.mean(jnp.square(o), axis=-1, keepdims=True) + RMS_EPS) * gla_norm_g.astype(f32)
    o = o.reshape(B, S, GLA_HEADS * GLA_DV).astype(h.dtype) * jax.nn.silu(g_out)
    y_a = o @ w_branch_gla
    a = swa_banded(sq.reshape(B, S, SWA_HEADS, SWA_HEAD_DIM),
                   sk.reshape(B, S, SWA_KV_HEADS, SWA_HEAD_DIM),
                   sv.reshape(B, S, SWA_KV_HEADS, SWA_HEAD_DIM), swa_sinks)
    y_b = a @ w_branch_swa
    merged = jax.nn.sigmoid(gate_a) * y_a + jax.nn.sigmoid(gate_b) * y_b
    return merged @ w_out


def setup_inputs(seed: int = 0) -> dict:
    key = jax.random.key(seed)
    ks = jax.random.split(key, 32)
    D, L = D_MODEL, DEPTH
    f32 = jnp.float32

    def nrm(k, shape, scale):
        return jax.random.normal(k, shape, f32) * scale

    col_scale = jnp.concatenate([
        jnp.full((s,), DEEPNORM_BETA if i in MIX_VALUE_COLS else 1.0, f32)
        for i, s in enumerate(MIX_SIZES)])
    return {
        "x": nrm(ks[0], (BATCH, SEQ, D), 1.0),
        "c": nrm(ks[1], (BATCH, D), 1.0),
        "w_ada": nrm(ks[2], (L, D, N_ADA * D), D ** -0.5),
        "b_ada": nrm(ks[3], (L, N_ADA * D), 0.02),
        "ffn1_w_gate": nrm(ks[4], (L, D, D_FF), DEEPNORM_BETA * D ** -0.5),
        "ffn1_w_up": nrm(ks[5], (L, D, D_FF), DEEPNORM_BETA * D ** -0.5),
        "ffn1_w_down": nrm(ks[6], (L, D_FF, D), DEEPNORM_BETA * D_FF ** -0.5),
        "ln1_g": 1.0 + nrm(ks[7], (L, D), 0.02),
        "ln1_b": nrm(ks[8], (L, D), 0.02),
        "w_in": nrm(ks[9], (L, D, MIX_WIDTH), D ** -0.5) * col_scale,
        "w_gla_gate_up": nrm(ks[10], (L, GLA_GATE_RANK, GLA_HEADS * GLA_DK), GLA_GATE_RANK ** -0.5),
        "b_gla_gate": nrm(ks[11], (L, GLA_HEADS * GLA_DK), 0.1),
        "gla_norm_g": 1.0 + nrm(ks[12], (L, GLA_DV), 0.02),
        "w_branch_gla": nrm(ks[13], (L, GLA_HEADS * GLA_DV, D), DEEPNORM_BETA * (GLA_HEADS * GLA_DV) ** -0.5),
        "swa_sinks": nrm(ks[14], (L, SWA_HEADS), 0.5),
        "w_branch_swa": nrm(ks[15], (L, SWA_HEADS * SWA_HEAD_DIM, D), DEEPNORM_BETA * (SWA_HEADS * SWA_HEAD_DIM) ** -0.5),
        "w_out": nrm(ks[16], (L, D, D), DEEPNORM_BETA * D ** -0.5),
        "ln2_g": 1.0 + nrm(ks[17], (L, D), 0.02),
        "ln2_b": nrm(ks[18], (L, D), 0.02),
        "ffn2_w_gate": nrm(ks[19], (L, D, D_FF), DEEPNORM_BETA * D ** -0.5),
        "ffn2_w_up": nrm(ks[20], (L, D, D_FF), DEEPNORM_BETA * D ** -0.5),
        "ffn2_w_down": nrm(ks[21], (L, D_FF, D), DEEPNORM_BETA * D_FF ** -0.5),
        "ln3_g": 1.0 + nrm(ks[22], (L, D), 0.02),
        "ln3_b": nrm(ks[23], (L, D), 0.02),
    }


def reference(x, c, w_ada, b_ada, ffn1_w_gate, ffn1_w_up, ffn1_w_down, ln1_g, ln1_b,
              w_in, w_gla_gate_up, b_gla_gate, gla_norm_g, w_branch_gla, swa_sinks,
              w_branch_swa, w_out, ln2_g, ln2_b, ffn2_w_gate, ffn2_w_up, ffn2_w_down,
              ln3_g, ln3_b):
    for l in range(DEPTH):
        ada = jax.nn.silu(c) @ w_ada[l] + b_ada[l]
        sh1, sc1, gt1, sh2, sc2, gt2, sh3, sc3, gt3 = jnp.split(ada, N_ADA, axis=-1)
        y = swiglu(modulate(x, sh1, sc1), ffn1_w_gate[l], ffn1_w_up[l], ffn1_w_down[l])
        x = layer_norm(DEEPNORM_ALPHA * x + MACARON_WEIGHT * gt1[:, None, :] * y, ln1_g[l], ln1_b[l])
        y = hybrid_mixer(modulate(x, sh2, sc2), w_in[l], w_gla_gate_up[l], b_gla_gate[l],
                         gla_norm_g[l], w_branch_gla[l], swa_sinks[l], w_branch_swa[l], w_out[l])
        x = layer_norm(DEEPNORM_ALPHA * x + gt2[:, None, :] * y, ln2_g[l], ln2_b[l])
        y = swiglu(modulate(x, sh3, sc3), ffn2_w_gate[l], ffn2_w_up[l], ffn2_w_down[l])
        x = layer_norm(DEEPNORM_ALPHA * x + MACARON_WEIGHT * gt3[:, None, :] * y, ln3_g[l], ln3_b[l])
    return x
```

```python
import functools

import numpy as np
import jax
import jax.numpy as jnp
from jax import lax
from jax.experimental import pallas as pl
from jax.experimental.pallas import tpu as pltpu

D_MODEL = 1024
SEQ = 16384
D_FF = 2816
N_ADA = 9
LN_EPS = 1e-5
RMS_EPS = 1e-6
NEG_INF = -1e30
DEEPNORM_ALPHA = 2.0 ** 0.25
MACARON_WEIGHT = 0.5

GLA_HEADS = 4
GLA_DK = 128
GLA_DV = 256
GLA_GATE_RANK = 16
GLA_GATE_NORMALIZER = 16.0
GLA_CHUNK = 64
GLA_LEVELS = (32, 16, 8, 4, 2, 1)

SWA_HEADS = 16
SWA_KV_HEADS = 4
SWA_HEAD_DIM = 64
SWA_BLOCK = 128

MIX_SIZES = (512, 512, 1024, 16, 1024, 1024, 256, 256, 1024, 1024)

LANES = 128
V7X_VMEM_LIMIT_BYTES = 56 * 1024 * 1024

BF16 = jnp.bfloat16
F32 = jnp.float32


def _cparams():
    return pltpu.CompilerParams(dimension_semantics=("arbitrary",),
                                vmem_limit_bytes=V7X_VMEM_LIMIT_BYTES)


def _resident(shape):
    return pl.BlockSpec(shape, lambda i: (0,) * len(shape), pipeline_mode=pl.Buffered(1))


def _dot(a, b):
    return jnp.dot(a, b, preferred_element_type=F32)


def _dot_t(a, b):
    return lax.dot_general(a, b, (((1,), (1,)), ((), ())), preferred_element_type=F32)


def _sigmoid(x):
    return 1.0 / (1.0 + jnp.exp(-x))


def _layer_norm(r, g, b):
    mu = jnp.mean(r, axis=-1, keepdims=True)
    d = r - mu
    var = jnp.mean(d * d, axis=-1, keepdims=True)
    return d * lax.rsqrt(var + LN_EPS) * g + b


def _ada_slice(ada_ref, idx):
    return ada_ref[:, idx * D_MODEL:(idx + 1) * D_MODEL]


ADA_TILE_N = 1536


def _ada_kernel(c_ref, w_ref, b_ref, o_ref):
    c = c_ref[...]
    s = (c * _sigmoid(c)).astype(BF16)
    s8 = jnp.broadcast_to(s, (8, D_MODEL))
    y = _dot(s8, w_ref[...].astype(BF16))
    o_ref[...] = y[0:1, :] + b_ref[...]


def _ada(c, w_ada, b_ada):
    n = N_ADA * D_MODEL
    return pl.pallas_call(
        _ada_kernel,
        grid=(n // ADA_TILE_N,),
        in_specs=[pl.BlockSpec((1, D_MODEL), lambda i: (0, 0)),
                  pl.BlockSpec((D_MODEL, ADA_TILE_N), lambda i: (0, i)),
                  pl.BlockSpec((1, ADA_TILE_N), lambda i: (0, i))],
        out_specs=pl.BlockSpec((1, ADA_TILE_N), lambda i: (0, i)),
        out_shape=jax.ShapeDtypeStruct((1, n), F32),
        compiler_params=_cparams(),
        name="ada",
    )(c, w_ada, b_ada.reshape(1, n))


FFN_TILE_M = 512
FFN_SPLIT = 1536


def _ffn_kernel(x_ref, ada_ref, wg_ref, wu_ref, wd_ref, lng_ref, lnb_ref, o_ref, *, ada_base):
    x = x_ref[...]
    sh = _ada_slice(ada_ref, ada_base)
    sc = _ada_slice(ada_ref, ada_base + 1)
    gt = _ada_slice(ada_ref, ada_base + 2)
    h = (x * (1.0 + sc) + sh).astype(BF16)
    y = None
    for lo, hi in ((0, FFN_SPLIT), (FFN_SPLIT, D_FF)):
        g = _dot(h, wg_ref[:, lo:hi])
        u = _dot(h, wu_ref[:, lo:hi])
        a = (g * _sigmoid(g) * u).astype(BF16)
        part = _dot(a, wd_ref[lo:hi, :])
        y = part if y is None else y + part
    r = DEEPNORM_ALPHA * x + (MACARON_WEIGHT * gt) * y
    o_ref[...] = _layer_norm(r, lng_ref[...], lnb_ref[...])


def _ffn(x, ada, wg, wu, wd, ln_g, ln_b, ada_base):
    tm = FFN_TILE_M
    row = pl.BlockSpec((tm, D_MODEL), lambda i: (i, 0))
    return pl.pallas_call(
        functools.partial(_ffn_kernel, ada_base=ada_base),
        grid=(SEQ // tm,),
        in_specs=[row, _resident((1, N_ADA * D_MODEL)),
                  _resident((D_MODEL, D_FF)), _resident((D_MODEL, D_FF)), _resident((D_FF, D_MODEL)),
                  _resident((1, D_MODEL)), _resident((1, D_MODEL))],
        out_specs=row,
        out_shape=jax.ShapeDtypeStruct((SEQ, D_MODEL), F32),
        compiler_params=_cparams(),
        name="ffn",
    )(x, ada, wg, wu, wd, ln_g, ln_b)


PROJ_TILE_M = 512
LR_PAD = LANES


def _log_sigmoid(z):
    return -(jnp.maximum(-z, 0.0) + jnp.log1p(jnp.exp(-jnp.abs(z))))


def _proj_kernel(x_ref, ada_ref, wq_ref, wk_ref, wv_ref, wlr_ref, wup_ref, bup_ref, wgo_ref,
                 wsq_ref, wsk_ref, wsv_ref, wga_ref, wgb_ref,
                 q_ref, k_ref, v_ref, la_ref, go_ref, sq_ref, sk_ref, sv_ref, ga_ref, gb_ref):
    x = x_ref[...]
    h = (x * (1.0 + _ada_slice(ada_ref, 4)) + _ada_slice(ada_ref, 3)).astype(BF16)
    q_ref[...] = _dot(h, wq_ref[...]) * (GLA_DK ** -0.5)
    k_ref[...] = _dot(h, wk_ref[...])
    v_ref[...] = _dot(h, wv_ref[...]).astype(BF16)
    lr = _dot(h, wlr_ref[...]).astype(BF16)
    z = _dot(lr, wup_ref[...]) + bup_ref[...]
    la_ref[...] = _log_sigmoid(z) / GLA_GATE_NORMALIZER
    go = _dot(h, wgo_ref[...])
    go_ref[...] = go * _sigmoid(go)
    sq_ref[...] = _dot(h, wsq_ref[...]).astype(BF16)
    sk_ref[...] = _dot(h, wsk_ref[...]).astype(BF16)
    sv_ref[...] = _dot(h, wsv_ref[...]).astype(BF16)
    ga_ref[...] = _sigmoid(_dot(h, wga_ref[...]))
    gb_ref[...] = _sigmoid(_dot(h, wgb_ref[...]))


def _proj(x, ada, ws, wup, bup):
    tm = PROJ_TILE_M
    wq, wk, wv, wlr, wgo, wsq, wsk, wsv, wga, wgb = ws
    qk = GLA_HEADS * GLA_DK
    kvw = SWA_KV_HEADS * SWA_HEAD_DIM

    def row(n):
        return pl.BlockSpec((tm, n), lambda i: (i, 0))

    def w(n):
        return _resident((D_MODEL, n))

    out_cols = ((qk, F32), (qk, F32), (D_MODEL, BF16), (qk, F32), (D_MODEL, F32),
                (D_MODEL, BF16), (kvw, BF16), (kvw, BF16), (D_MODEL, F32), (D_MODEL, F32))
    return pl.pallas_call(
        _proj_kernel,
        grid=(SEQ // tm,),
        in_specs=[row(D_MODEL), _resident((1, N_ADA * D_MODEL)),
                  w(qk), w(qk), w(D_MODEL), w(LR_PAD), _resident((LR_PAD, qk)), _resident((1, qk)),
                  w(D_MODEL), w(D_MODEL), w(kvw), w(kvw), w(D_MODEL), w(D_MODEL)],
        out_specs=[row(n) for n, _ in out_cols],
        out_shape=[jax.ShapeDtypeStruct((SEQ, n), dt) for n, dt in out_cols],
        compiler_params=_cparams(),
        name="proj",
    )(x, ada, wq, wk, wv, wlr, wup, bup, wgo, wsq, wsk, wsv, wga, wgb)


GLA_TILE_M = 512
GLA_NSEG = 1 + len(GLA_LEVELS)
GLA_SPLIT = 3


def _gla_constants():
    c = GLA_CHUNK
    t = np.arange(c)[:, None]
    u = np.arange(c)[None, :]
    seg = [(u <= t)]
    mask = [(u == t)]
    for m in GLA_LEVELS:
        blk_t, blk_u = t // m, u // m
        odd = (blk_t % 2) == 1
        p_odd = blk_t * m
        p_even = (blk_t + 1) * m
        seg.append(np.where(odd, (u > p_odd) & (u <= t), (u > t) & (u <= p_even)))
        mask.append(odd & (blk_u == blk_t - 1))
    seg = np.concatenate(seg, axis=0).astype(np.float32)
    seg = np.concatenate([seg] * GLA_SPLIT, axis=1)
    mask = np.stack(mask).astype(np.float32)
    return seg, mask


def _gla_kernel(q_ref, k_ref, v_ref, la_ref, go_ref, seg_ref, mask_ref, gn_ref, o_ref, state_ref):
    c = GLA_CHUNK

    @pl.when(pl.program_id(0) == 0)
    def _():
        state_ref[...] = jnp.zeros_like(state_ref)

    def chunk(ci, carry):
        rows = pl.ds(pl.multiple_of(ci * c, c), c)
        g = la_ref[rows, :]
        parts = []
        rem = g
        for _ in range(GLA_SPLIT):
            p = rem.astype(BF16)
            parts.append(p)
            rem = rem - p.astype(F32)
        e_all = _dot(seg_ref[...], jnp.concatenate(parts, axis=0))
        b = e_all[0:c]
        b_last = b[c - 1:c]
        for hd in range(GLA_HEADS):
            ks = slice(hd * GLA_DK, (hd + 1) * GLA_DK)
            vs = slice(hd * GLA_DV, (hd + 1) * GLA_DV)
            q = q_ref[rows, ks]
            k = k_ref[rows, ks]
            v = v_ref[rows, vs]
            bh = b[:, ks]
            bl = b_last[:, ks]
            state = state_ref[hd]
            o = _dot((q * jnp.exp(bh)).astype(BF16), state.astype(BF16))
            a = _dot_t(q.astype(BF16), k.astype(BF16)) * mask_ref[0]
            for li in range(len(GLA_LEVELS)):
                e = jnp.exp(e_all[(li + 1) * c:(li + 2) * c, ks])
                a = a + _dot_t((q * e).astype(BF16), (k * e).astype(BF16)) * mask_ref[li + 1]
            o = o + _dot(a.astype(BF16), v)
            k_dec = k * jnp.exp(bl - bh)
            upd = _dot(k_dec.T.astype(BF16), v)
            d_col = jnp.exp(jnp.broadcast_to(bl, (8, GLA_DK))).T[:, 0:1]
            state_ref[hd] = d_col * state + upd
            ms = jnp.mean(o * o, axis=-1, keepdims=True)
            on = o * lax.rsqrt(ms + RMS_EPS) * gn_ref[...]
            o_ref[rows, vs] = (on * go_ref[rows, vs]).astype(BF16)
        return carry

    lax.fori_loop(0, GLA_TILE_M // c, chunk, 0)


def _gla(q, k, v, la, go, gn):
    tm = GLA_TILE_M
    seg, mask = _gla_constants()
    qk = GLA_HEADS * GLA_DK

    def row(n):
        return pl.BlockSpec((tm, n), lambda i: (i, 0))

    return pl.pallas_call(
        _gla_kernel,
        grid=(SEQ // tm,),
        in_specs=[row(qk), row(qk), row(D_MODEL), row(qk), row(D_MODEL),
                  _resident(seg.shape), _resident(mask.shape), _resident((1, GLA_DV))],
        out_specs=row(D_MODEL),
        out_shape=jax.ShapeDtypeStruct((SEQ, D_MODEL), BF16),
        scratch_shapes=[pltpu.VMEM((GLA_HEADS, GLA_DK, GLA_DV), F32)],
        compiler_params=_cparams(),
        name="gla",
    )(q, k, v, la, go, jnp.asarray(seg, BF16), jnp.asarray(mask, F32), gn)


def _swa_kernel(slope_ref, sink_ref, q_ref, kc_ref, kp_ref, vc_ref, vp_ref, o_ref):
    blk = SWA_BLOCK
    hd = SWA_HEAD_DIM
    group = SWA_HEADS // SWA_KV_HEADS
    k_min = jnp.where(pl.program_id(0) == 0, blk, 0)

    qi = lax.broadcasted_iota(jnp.int32, (blk, 2 * blk), 0)
    kj = lax.broadcasted_iota(jnp.int32, (blk, 2 * blk), 1)
    dist = qi + blk - kj
    valid = (dist >= 0) & (dist < blk) & (kj >= k_min)
    distf = dist.astype(F32)
    lane = lax.broadcasted_iota(jnp.int32, (blk, LANES), 1)
    low = lane < hd
    lane2 = lax.broadcasted_iota(jnp.int32, (2 * blk, LANES), 1)
    low2 = lane2 < hd

    k_all = jnp.concatenate([kp_ref[...], kc_ref[...]], axis=0).astype(F32)
    v_all = jnp.concatenate([vp_ref[...], vc_ref[...]], axis=0).astype(F32)

    def dup(x, kv):
        tile = x[:, (kv // 2) * LANES:(kv // 2 + 1) * LANES]
        rolled = pltpu.roll(tile, hd, 1)
        lo_half, hi_half = (tile, rolled) if kv % 2 == 0 else (rolled, tile)
        return jnp.where(low2, lo_half, hi_half).astype(BF16)

    for kv in range(SWA_KV_HEADS):
        k2 = dup(k_all, kv)
        v2 = dup(v_all, kv)
        zero = jnp.zeros((blk, LANES), BF16)
        lhs = []
        for pair in (2 * kv, 2 * kv + 1):
            x = q_ref[:, pair * LANES:(pair + 1) * LANES]
            lhs += [jnp.where(low, x, zero), jnp.where(low, zero, x)]
        s_all = _dot_t(jnp.concatenate(lhs, axis=0), k2) * (hd ** -0.5)
        probs = []
        for gi in range(group):
            head = kv * group + gi
            s = s_all[gi * blk:(gi + 1) * blk] - slope_ref[head] * distf
            s = jnp.where(valid, s, NEG_INF)
            sink = sink_ref[head]
            m = jnp.maximum(jnp.max(s, axis=-1, keepdims=True), sink)
            p = jnp.exp(s - m)
            denom = jnp.sum(p, axis=-1, keepdims=True) + jnp.exp(sink - m)
            probs.append((p * (1.0 / denom)).astype(BF16))
        r = _dot(jnp.concatenate(probs, axis=0), v2)
        for pi, pair in enumerate((2 * kv, 2 * kv + 1)):
            r0 = r[(2 * pi) * blk:(2 * pi + 1) * blk]
            r1 = r[(2 * pi + 1) * blk:(2 * pi + 2) * blk]
            o_ref[:, pair * LANES:(pair + 1) * LANES] = jnp.where(low, r0, r1).astype(BF16)


def _swa(sq, sk, sv, slopes, sinks):
    blk = SWA_BLOCK
    kvw = SWA_KV_HEADS * SWA_HEAD_DIM
    cur = lambda i: (i, 0)
    prev = lambda i: (jnp.maximum(i - 1, 0), 0)
    smem = pl.BlockSpec(memory_space=pltpu.SMEM)
    return pl.pallas_call(
        _swa_kernel,
        grid=(SEQ // blk,),
        in_specs=[smem, smem,
                  pl.BlockSpec((blk, D_MODEL), cur),
                  pl.BlockSpec((blk, kvw), cur), pl.BlockSpec((blk, kvw), prev),
                  pl.BlockSpec((blk, kvw), cur), pl.BlockSpec((blk, kvw), prev)],
        out_specs=pl.BlockSpec((blk, D_MODEL), cur),
        out_shape=jax.ShapeDtypeStruct((SEQ, D_MODEL), BF16),
        compiler_params=_cparams(),
        name="swa",
    )(slopes, sinks, sq, sk, sk, sv, sv)


MIX_TILE_M = 512


def _mixout_kernel(x_ref, ada_ref, o_ref, a_ref, ga_ref, gb_ref, wa_ref, wb_ref, wo_ref,
                   lng_ref, lnb_ref, out_ref):
    ya = _dot(o_ref[...], wa_ref[...])
    yb = _dot(a_ref[...], wb_ref[...])
    merged = (ga_ref[...] * ya + gb_ref[...] * yb).astype(BF16)
    y = _dot(merged, wo_ref[...])
    r = DEEPNORM_ALPHA * x_ref[...] + _ada_slice(ada_ref, 5) * y
    out_ref[...] = _layer_norm(r, lng_ref[...], lnb_ref[...])


def _mixout(x, ada, o, a, ga, gb, wa, wb, wo, ln_g, ln_b):
    tm = MIX_TILE_M
    row = pl.BlockSpec((tm, D_MODEL), lambda i: (i, 0))
    wsq = _resident((D_MODEL, D_MODEL))
    vec = _resident((1, D_MODEL))
    return pl.pallas_call(
        _mixout_kernel,
        grid=(SEQ // tm,),
        in_specs=[row, _resident((1, N_ADA * D_MODEL)), row, row, row, row, wsq, wsq, wsq, vec, vec],
        out_specs=row,
        out_shape=jax.ShapeDtypeStruct((SEQ, D_MODEL), F32),
        compiler_params=_cparams(),
        name="mixout",
    )(x, ada, o, a, ga, gb, wa, wb, wo, ln_g, ln_b)


def _alibi_slopes(n):
    return 2.0 ** (-8.0 * jnp.arange(1, n + 1, dtype=jnp.float32) / n)


def _split_w_in(w_in):
    offs = np.concatenate([[0], np.cumsum(MIX_SIZES)])
    cols = [w_in[:, offs[i]:offs[i + 1]].astype(BF16) for i in range(len(MIX_SIZES))]
    cols[3] = jnp.pad(cols[3], ((0, 0), (0, LR_PAD - GLA_GATE_RANK)))
    return cols


def kernel(x, c, w_ada, b_ada, ffn1_w_gate, ffn1_w_up, ffn1_w_down, ln1_g, ln1_b, w_in,
           w_gla_gate_up, b_gla_gate, gla_norm_g, w_branch_gla, swa_sinks, w_branch_swa, w_out,
           ln2_g, ln2_b, ffn2_w_gate, ffn2_w_up, ffn2_w_down, ln3_g, ln3_b):
    assert x.shape == (1, SEQ, D_MODEL) and w_ada.shape[0] == 1
    x2d = x.reshape(SEQ, D_MODEL)
    vec = lambda p: p.reshape(1, -1)

    ada = _ada(c, w_ada[0], b_ada[0])
    x1 = _ffn(x2d, ada, ffn1_w_gate[0].astype(BF16), ffn1_w_up[0].astype(BF16),
              ffn1_w_down[0].astype(BF16), vec(ln1_g), vec(ln1_b), 0)

    wup = jnp.pad(w_gla_gate_up[0], ((0, LR_PAD - GLA_GATE_RANK), (0, 0))).astype(BF16)
    q, k, v, la, go, sq, sk, sv, ga, gb = _proj(x1, ada, _split_w_in(w_in[0]), wup, vec(b_gla_gate))
    o = _gla(q, k, v, la, go, vec(gla_norm_g))
    a = _swa(sq, sk, sv, _alibi_slopes(SWA_HEADS), swa_sinks[0])
    x2 = _mixout(x1, ada, o, a, ga, gb, w_branch_gla[0].astype(BF16), w_branch_swa[0].astype(BF16),
                 w_out[0].astype(BF16), vec(ln2_g), vec(ln2_b))

    out = _ffn(x2, ada, ffn2_w_gate[0].astype(BF16), ffn2_w_up[0].astype(BF16),
               ffn2_w_down[0].astype(BF16), vec(ln3_g), vec(ln3_b), 6)
    return out.reshape(1, SEQ, D_MODEL)
```

```python
import functools

import numpy as np
import jax
import jax.numpy as jnp
from jax import lax
from jax.experimental import pallas as pl
from jax.experimental.pallas import tpu as pltpu

D_MODEL = 1024
SEQ = 16384
D_FF = 2816
N_ADA = 9
LN_EPS = 1e-5
RMS_EPS = 1e-6
NEG_INF = -1e30
DEEPNORM_ALPHA = 2.0 ** 0.25
MACARON_WEIGHT = 0.5

GLA_HEADS = 4
GLA_DK = 128
GLA_DV = 256
GLA_GATE_RANK = 16
GLA_GATE_NORMALIZER = 16.0
GLA_CHUNK = 64
GLA_LEVELS = (32, 16, 8, 4, 2, 1)

SWA_HEADS = 16
SWA_KV_HEADS = 4
SWA_HEAD_DIM = 64
SWA_BLOCK = 128

MIX_SIZES = (512, 512, 1024, 16, 1024, 1024, 256, 256, 1024, 1024)

LANES = 128
V7X_VMEM_LIMIT_BYTES = 56 * 1024 * 1024

BF16 = jnp.bfloat16
F32 = jnp.float32


def _cparams():
    return pltpu.CompilerParams(dimension_semantics=("arbitrary",),
                                vmem_limit_bytes=V7X_VMEM_LIMIT_BYTES)


def _resident(shape):
    return pl.BlockSpec(shape, lambda i: (0,) * len(shape), pipeline_mode=pl.Buffered(1))


def _dot(a, b):
    return jnp.dot(a, b, preferred_element_type=F32)


def _dot_t(a, b):
    return lax.dot_general(a, b, (((1,), (1,)), ((), ())), preferred_element_type=F32)


def _sigmoid(x):
    return 1.0 / (1.0 + jnp.exp(-x))


def _layer_norm(r, g, b):
    mu = jnp.mean(r, axis=-1, keepdims=True)
    d = r - mu
    var = jnp.mean(d * d, axis=-1, keepdims=True)
    return d * lax.rsqrt(var + LN_EPS) * g + b


def _ada_slice(ada_ref, idx):
    return ada_ref[:, idx * D_MODEL:(idx + 1) * D_MODEL]


ADA_TILE_N = 1536


def _ada_kernel(c_ref, w_ref, b_ref, o_ref):
    c = c_ref[...]
    s = (c * _sigmoid(c)).astype(BF16)
    s8 = jnp.broadcast_to(s, (8, D_MODEL))
    y = _dot(s8, w_ref[...].astype(BF16))
    o_ref[...] = y[0:1, :] + b_ref[...]


def _ada(c, w_ada, b_ada):
    n = N_ADA * D_MODEL
    return pl.pallas_call(
        _ada_kernel,
        grid=(n // ADA_TILE_N,),
        in_specs=[pl.BlockSpec((1, D_MODEL), lambda i: (0, 0)),
                  pl.BlockSpec((D_MODEL, ADA_TILE_N), lambda i: (0, i)),
                  pl.BlockSpec((1, ADA_TILE_N), lambda i: (0, i))],
        out_specs=pl.BlockSpec((1, ADA_TILE_N), lambda i: (0, i)),
        out_shape=jax.ShapeDtypeStruct((1, n), F32),
        compiler_params=_cparams(),
        name="ada",
    )(c, w_ada, b_ada.reshape(1, n))


FFN_TILE_M = 512
FFN_SPLIT = 1536


def _ffn_kernel(x_ref, ada_ref, wg_ref, wu_ref, wd_ref, lng_ref, lnb_ref, o_ref, *, ada_base):
    x = x_ref[...]
    sh = _ada_slice(ada_ref, ada_base)
    sc = _ada_slice(ada_ref, ada_base + 1)
    gt = _ada_slice(ada_ref, ada_base + 2)
    h = (x * (1.0 + sc) + sh).astype(BF16)
    y = None
    for lo, hi in ((0, FFN_SPLIT), (FFN_SPLIT, D_FF)):
        g = _dot(h, wg_ref[:, lo:hi])
        u = _dot(h, wu_ref[:, lo:hi])
        a = (g * _sigmoid(g) * u).astype(BF16)
        part = _dot(a, wd_ref[lo:hi, :])
        y = part if y is None else y + part
    r = DEEPNORM_ALPHA * x + (MACARON_WEIGHT * gt) * y
    o_ref[...] = _layer_norm(r, lng_ref[...], lnb_ref[...])


def _ffn(x, ada, wg, wu, wd, ln_g, ln_b, ada_base):
    tm = FFN_TILE_M
    row = pl.BlockSpec((tm, D_MODEL), lambda i: (i, 0))
    return pl.pallas_call(
        functools.partial(_ffn_kernel, ada_base=ada_base),
        grid=(SEQ // tm,),
        in_specs=[row, _resident((1, N_ADA * D_MODEL)),
                  _resident((D_MODEL, D_FF)), _resident((D_MODEL, D_FF)), _resident((D_FF, D_MODEL)),
                  _resident((1, D_MODEL)), _resident((1, D_MODEL))],
        out_specs=row,
        out_shape=jax.ShapeDtypeStruct((SEQ, D_MODEL), F32),
        compiler_params=_cparams(),
        name="ffn",
    )(x, ada, wg, wu, wd, ln_g, ln_b)


PROJ_TILE_M = 512
LR_PAD = LANES


def _log_sigmoid(z):
    return -(jnp.maximum(-z, 0.0) + jnp.log1p(jnp.exp(-jnp.abs(z))))


def _proj_kernel(x_ref, ada_ref, wq_ref, wk_ref, wv_ref, wlr_ref, wup_ref, bup_ref, wgo_ref,
                 wsq_ref, wsk_ref, wsv_ref, wga_ref, wgb_ref,
                 q_ref, k_ref, v_ref, la_ref, go_ref, sq_ref, sk_ref, sv_ref, ga_ref, gb_ref):
    x = x_ref[...]
    h = (x * (1.0 + _ada_slice(ada_ref, 4)) + _ada_slice(ada_ref, 3)).astype(BF16)
    q_ref[...] = _dot(h, wq_ref[...]) * (GLA_DK ** -0.5)
    k_ref[...] = _dot(h, wk_ref[...])
    v_ref[...] = _dot(h, wv_ref[...]).astype(BF16)
    lr = _dot(h, wlr_ref[...]).astype(BF16)
    z = _dot(lr, wup_ref[...]) + bup_ref[...]
    la_ref[...] = _log_sigmoid(z) / GLA_GATE_NORMALIZER
    go = _dot(h, wgo_ref[...])
    go_ref[...] = go * _sigmoid(go)
    sq_ref[...] = _dot(h, wsq_ref[...]).astype(BF16)
    sk_ref[...] = _dot(h, wsk_ref[...]).astype(BF16)
    sv_ref[...] = _dot(h, wsv_ref[...]).astype(BF16)
    ga_ref[...] = _sigmoid(_dot(h, wga_ref[...]))
    gb_ref[...] = _sigmoid(_dot(h, wgb_ref[...]))


def _proj(x, ada, ws, wup, bup):
    tm = PROJ_TILE_M
    wq, wk, wv, wlr, wgo, wsq, wsk, wsv, wga, wgb = ws
    qk = GLA_HEADS * GLA_DK
    kvw = SWA_KV_HEADS * SWA_HEAD_DIM

    def row(n):
        return pl.BlockSpec((tm, n), lambda i: (i, 0))

    def w(n):
        return _resident((D_MODEL, n))

    out_cols = ((qk, F32), (qk, F32), (D_MODEL, BF16), (qk, F32), (D_MODEL, F32),
                (D_MODEL, BF16), (kvw, BF16), (kvw, BF16), (D_MODEL, F32), (D_MODEL, F32))
    return pl.pallas_call(
        _proj_kernel,
        grid=(SEQ // tm,),
        in_specs=[row(D_MODEL), _resident((1, N_ADA * D_MODEL)),
                  w(qk), w(qk), w(D_MODEL), w(LR_PAD), _resident((LR_PAD, qk)), _resident((1, qk)),
                  w(D_MODEL), w(D_MODEL), w(kvw), w(kvw), w(D_MODEL), w(D_MODEL)],
        out_specs=[row(n) for n, _ in out_cols],
        out_shape=[jax.ShapeDtypeStruct((SEQ, n), dt) for n, dt in out_cols],
        compiler_params=_cparams(),
        name="proj",
    )(x, ada, wq, wk, wv, wlr, wup, bup, wgo, wsq, wsk, wsv, wga, wgb)


GLA_TILE_M = 512
GLA_NSEG = 1 + len(GLA_LEVELS)
GLA_SPLIT = 3
GLA_FAST_CHUNK = 128
GLA_SAFE_LOG_DECAY = -60.0


def _gla_constants():
    c = GLA_CHUNK
    t = np.arange(c)[:, None]
    u = np.arange(c)[None, :]
    seg = [(u <= t)]
    mask = [(u == t)]
    for m in GLA_LEVELS:
        blk_t, blk_u = t // m, u // m
        odd = (blk_t % 2) == 1
        p_odd = blk_t * m
        p_even = (blk_t + 1) * m
        seg.append(np.where(odd, (u > p_odd) & (u <= t), (u > t) & (u <= p_even)))
        mask.append(odd & (blk_u == blk_t - 1))
    seg = np.concatenate(seg, axis=0).astype(np.float32)
    seg = np.concatenate([seg] * GLA_SPLIT, axis=1)
    mask = np.stack(mask).astype(np.float32)
    fc = GLA_FAST_CHUNK
    tri = np.tril(np.ones((fc, fc), np.float32))
    tri = np.concatenate([tri] * GLA_SPLIT, axis=1)
    return seg, mask, tri


def _split_terms(g):
    parts = []
    rem = g
    for _ in range(GLA_SPLIT):
        p = rem.astype(BF16)
        parts.append(p)
        rem = rem - p.astype(F32)
    return jnp.concatenate(parts, axis=0)


def _decay_column(bl):
    return jnp.exp(jnp.broadcast_to(bl, (8, GLA_DK))).T[:, 0:1]


def _gla_finish(o, gn_ref, go, o_ref, rows, vs):
    ms = jnp.mean(o * o, axis=-1, keepdims=True)
    on = o * lax.rsqrt(ms + RMS_EPS) * gn_ref[...]
    o_ref[rows, vs] = (on * go).astype(BF16)


def _gla_tile_fast(q_ref, k_ref, v_ref, b_ref, go_ref, gn_ref, o_ref, state_ref):
    c = GLA_FAST_CHUNK
    ti = lax.broadcasted_iota(jnp.int32, (c, c), 0)
    si = lax.broadcasted_iota(jnp.int32, (c, c), 1)
    causal = si <= ti
    for ci in range(GLA_TILE_M // c):
        rows = slice(ci * c, (ci + 1) * c)
        b = b_ref[rows, :]
        b_last = b[c - 1:c]
        e_pos = jnp.exp(b)
        e_neg = jnp.exp(-b)
        e_last = jnp.exp(b_last)
        for hd in range(GLA_HEADS):
            ks = slice(hd * GLA_DK, (hd + 1) * GLA_DK)
            vs = slice(hd * GLA_DV, (hd + 1) * GLA_DV)
            v = v_ref[rows, vs]
            q_in = (q_ref[rows, ks] * e_pos[:, ks]).astype(BF16)
            k_out = k_ref[rows, ks] * e_neg[:, ks]
            a = jnp.where(causal, _dot_t(q_in, k_out.astype(BF16)), 0.0).astype(BF16)
            state = state_ref[hd]
            o = _dot(jnp.concatenate([q_in, a], axis=1),
                     jnp.concatenate([state.astype(BF16), v], axis=0))
            k_dec = k_out * e_last[:, ks]
            state_ref[hd] = _decay_column(b_last[:, ks]) * state + _dot(k_dec.T.astype(BF16), v)
            _gla_finish(o, gn_ref, go_ref[rows, vs], o_ref, rows, vs)


def _gla_tile_any_decay(q_ref, k_ref, v_ref, la_ref, go_ref, seg_ref, mask_ref, gn_ref, o_ref,
                        state_ref):
    c = GLA_CHUNK

    def chunk(ci, carry):
        rows = pl.ds(pl.multiple_of(ci * c, c), c)
        e_all = _dot(seg_ref[...], _split_terms(la_ref[rows, :]))
        b = e_all[0:c]
        b_last = b[c - 1:c]
        for hd in range(GLA_HEADS):
            ks = slice(hd * GLA_DK, (hd + 1) * GLA_DK)
            vs = slice(hd * GLA_DV, (hd + 1) * GLA_DV)
            q = q_ref[rows, ks]
            k = k_ref[rows, ks]
            v = v_ref[rows, vs]
            bh = b[:, ks]
            bl = b_last[:, ks]
            state = state_ref[hd]
            o = _dot((q * jnp.exp(bh)).astype(BF16), state.astype(BF16))
            a = _dot_t(q.astype(BF16), k.astype(BF16)) * mask_ref[0]
            for li in range(len(GLA_LEVELS)):
                e = jnp.exp(e_all[(li + 1) * c:(li + 2) * c, ks])
                a = a + _dot_t((q * e).astype(BF16), (k * e).astype(BF16)) * mask_ref[li + 1]
            o = o + _dot(a.astype(BF16), v)
            k_dec = k * jnp.exp(bl - bh)
            state_ref[hd] = _decay_column(bl) * state + _dot(k_dec.T.astype(BF16), v)
            _gla_finish(o, gn_ref, go_ref[rows, vs], o_ref, rows, vs)
        return carry

    lax.fori_loop(0, GLA_TILE_M // c, chunk, 0)


def _gla_kernel(q_ref, k_ref, v_ref, la_ref, go_ref, seg_ref, mask_ref, tri_ref, gn_ref, o_ref,
                state_ref, b_ref):
    @pl.when(pl.program_id(0) == 0)
    def _():
        state_ref[...] = jnp.zeros_like(state_ref)

    c = GLA_FAST_CHUNK
    lowest = None
    for ci in range(GLA_TILE_M // c):
        rows = slice(ci * c, (ci + 1) * c)
        b = _dot(tri_ref[...], _split_terms(la_ref[rows, :]))
        b_ref[rows, :] = b
        total = jnp.min(b[c - 1:c])
        lowest = total if lowest is None else jnp.minimum(lowest, total)
    safe = lowest >= GLA_SAFE_LOG_DECAY

    @pl.when(safe)
    def _():
        _gla_tile_fast(q_ref, k_ref, v_ref, b_ref, go_ref, gn_ref, o_ref, state_ref)

    @pl.when(jnp.logical_not(safe))
    def _():
        _gla_tile_any_decay(q_ref, k_ref, v_ref, la_ref, go_ref, seg_ref, mask_ref, gn_ref, o_ref,
                            state_ref)


def _gla(q, k, v, la, go, gn):
    tm = GLA_TILE_M
    seg, mask, tri = _gla_constants()
    qk = GLA_HEADS * GLA_DK

    def row(n):
        return pl.BlockSpec((tm, n), lambda i: (i, 0))

    return pl.pallas_call(
        _gla_kernel,
        grid=(SEQ // tm,),
        in_specs=[row(qk), row(qk), row(D_MODEL), row(qk), row(D_MODEL),
                  _resident(seg.shape), _resident(mask.shape), _resident(tri.shape),
                  _resident((1, GLA_DV))],
        out_specs=row(D_MODEL),
        out_shape=jax.ShapeDtypeStruct((SEQ, D_MODEL), BF16),
        scratch_shapes=[pltpu.VMEM((GLA_HEADS, GLA_DK, GLA_DV), F32),
                        pltpu.VMEM((tm, qk), F32)],
        compiler_params=_cparams(),
        name="gla",
    )(q, k, v, la, go, jnp.asarray(seg, BF16), jnp.asarray(mask, F32), jnp.asarray(tri, BF16), gn)


def _swa_kernel(slope_ref, sink_ref, q_ref, kc_ref, kp_ref, vc_ref, vp_ref, o_ref):
    blk = SWA_BLOCK
    hd = SWA_HEAD_DIM
    group = SWA_HEADS // SWA_KV_HEADS
    k_min = jnp.where(pl.program_id(0) == 0, blk, 0)

    qi = lax.broadcasted_iota(jnp.int32, (blk, 2 * blk), 0)
    kj = lax.broadcasted_iota(jnp.int32, (blk, 2 * blk), 1)
    dist = qi + blk - kj
    valid = (dist >= 0) & (dist < blk) & (kj >= k_min)
    distf = dist.astype(F32)
    lane = lax.broadcasted_iota(jnp.int32, (blk, LANES), 1)
    low = lane < hd
    lane2 = lax.broadcasted_iota(jnp.int32, (2 * blk, LANES), 1)
    low2 = lane2 < hd

    k_all = jnp.concatenate([kp_ref[...], kc_ref[...]], axis=0).astype(F32)
    v_all = jnp.concatenate([vp_ref[...], vc_ref[...]], axis=0).astype(F32)

    def dup(x, kv):
        tile = x[:, (kv // 2) * LANES:(kv // 2 + 1) * LANES]
        rolled = pltpu.roll(tile, hd, 1)
        lo_half, hi_half = (tile, rolled) if kv % 2 == 0 else (rolled, tile)
        return jnp.where(low2, lo_half, hi_half).astype(BF16)

    for kv in range(SWA_KV_HEADS):
        k2 = dup(k_all, kv)
        v2 = dup(v_all, kv)
        zero = jnp.zeros((blk, LANES), BF16)
        lhs = []
        for pair in (2 * kv, 2 * kv + 1):
            x = q_ref[:, pair * LANES:(pair + 1) * LANES]
            lhs += [jnp.where(low, x, zero), jnp.where(low, zero, x)]
        s_all = _dot_t(jnp.concatenate(lhs, axis=0), k2) * (hd ** -0.5)
        probs = []
        for gi in range(group):
            head = kv * group + gi
            s = s_all[gi * blk:(gi + 1) * blk] - slope_ref[head] * distf
            s = jnp.where(valid, s, NEG_INF)
            sink = sink_ref[head]
            m = jnp.maximum(jnp.max(s, axis=-1, keepdims=True), sink)
            p = jnp.exp(s - m)
            denom = jnp.sum(p, axis=-1, keepdims=True) + jnp.exp(sink - m)
            probs.append((p * (1.0 / denom)).astype(BF16))
        r = _dot(jnp.concatenate(probs, axis=0), v2)
        for pi, pair in enumerate((2 * kv, 2 * kv + 1)):
            r0 = r[(2 * pi) * blk:(2 * pi + 1) * blk]
            r1 = r[(2 * pi + 1) * blk:(2 * pi + 2) * blk]
            o_ref[:, pair * LANES:(pair + 1) * LANES] = jnp.where(low, r0, r1).astype(BF16)


def _swa(sq, sk, sv, slopes, sinks):
    blk = SWA_BLOCK
    kvw = SWA_KV_HEADS * SWA_HEAD_DIM
    cur = lambda i: (i, 0)
    prev = lambda i: (jnp.maximum(i - 1, 0), 0)
    smem = pl.BlockSpec(memory_space=pltpu.SMEM)
    return pl.pallas_call(
        _swa_kernel,
        grid=(SEQ // blk,),
        in_specs=[smem, smem,
                  pl.BlockSpec((blk, D_MODEL), cur),
                  pl.BlockSpec((blk, kvw), cur), pl.BlockSpec((blk, kvw), prev),
                  pl.BlockSpec((blk, kvw), cur), pl.BlockSpec((blk, kvw), prev)],
        out_specs=pl.BlockSpec((blk, D_MODEL), cur),
        out_shape=jax.ShapeDtypeStruct((SEQ, D_MODEL), BF16),
        compiler_params=_cparams(),
        name="swa",
    )(slopes, sinks, sq, sk, sk, sv, sv)


MIX_TILE_M = 512


def _mixout_kernel(x_ref, ada_ref, o_ref, a_ref, ga_ref, gb_ref, wa_ref, wb_ref, wo_ref,
                   lng_ref, lnb_ref, out_ref):
    ya = _dot(o_ref[...], wa_ref[...])
    yb = _dot(a_ref[...], wb_ref[...])
    merged = (ga_ref[...] * ya + gb_ref[...] * yb).astype(BF16)
    y = _dot(merged, wo_ref[...])
    r = DEEPNORM_ALPHA * x_ref[...] + _ada_slice(ada_ref, 5) * y
    out_ref[...] = _layer_norm(r, lng_ref[...], lnb_ref[...])


def _mixout(x, ada, o, a, ga, gb, wa, wb, wo, ln_g, ln_b):
    tm = MIX_TILE_M
    row = pl.BlockSpec((tm, D_MODEL), lambda i: (i, 0))
    wsq = _resident((D_MODEL, D_MODEL))
    vec = _resident((1, D_MODEL))
    return pl.pallas_call(
        _mixout_kernel,
        grid=(SEQ // tm,),
        in_specs=[row, _resident((1, N_ADA * D_MODEL)), row, row, row, row, wsq, wsq, wsq, vec, vec],
        out_specs=row,
        out_shape=jax.ShapeDtypeStruct((SEQ, D_MODEL), F32),
        compiler_params=_cparams(),
        name="mixout",
    )(x, ada, o, a, ga, gb, wa, wb, wo, ln_g, ln_b)


def _alibi_slopes(n):
    return 2.0 ** (-8.0 * jnp.arange(1, n + 1, dtype=jnp.float32) / n)


def _split_w_in(w_in):
    offs = np.concatenate([[0], np.cumsum(MIX_SIZES)])
    cols = [w_in[:, offs[i]:offs[i + 1]].astype(BF16) for i in range(len(MIX_SIZES))]
    cols[3] = jnp.pad(cols[3], ((0, 0), (0, LR_PAD - GLA_GATE_RANK)))
    return cols


def kernel(x, c, w_ada, b_ada, ffn1_w_gate, ffn1_w_up, ffn1_w_down, ln1_g, ln1_b, w_in,
           w_gla_gate_up, b_gla_gate, gla_norm_g, w_branch_gla, swa_sinks, w_branch_swa, w_out,
           ln2_g, ln2_b, ffn2_w_gate, ffn2_w_up, ffn2_w_down, ln3_g, ln3_b):
    assert x.shape == (1, SEQ, D_MODEL) and w_ada.shape[0] == 1
    x2d = x.reshape(SEQ, D_MODEL)
    vec = lambda p: p.reshape(1, -1)

    ada = _ada(c, w_ada[0], b_ada[0])
    x1 = _ffn(x2d, ada, ffn1_w_gate[0].astype(BF16), ffn1_w_up[0].astype(BF16),
              ffn1_w_down[0].astype(BF16), vec(ln1_g), vec(ln1_b), 0)

    wup = jnp.pad(w_gla_gate_up[0], ((0, LR_PAD - GLA_GATE_RANK), (0, 0))).astype(BF16)
    q, k, v, la, go, sq, sk, sv, ga, gb = _proj(x1, ada, _split_w_in(w_in[0]), wup, vec(b_gla_gate))
    o = _gla(q, k, v, la, go, vec(gla_norm_g))
    a = _swa(sq, sk, sv, _alibi_slopes(SWA_HEADS), swa_sinks[0])
    x2 = _mixout(x1, ada, o, a, ga, gb, w_branch_gla[0].astype(BF16), w_branch_swa[0].astype(BF16),
                 w_out[0].astype(BF16), vec(ln2_g), vec(ln2_b))

    out = _ffn(x2, ada, ffn2_w_gate[0].astype(BF16), ffn2_w_up[0].astype(BF16),
               ffn2_w_down[0].astype(BF16), vec(ln3_g), vec(ln3_b), 6)
    return out.reshape(1, SEQ, D_MODEL)
```

```python
import functools

import numpy as np
import jax
import jax.numpy as jnp
from jax import lax
from jax.experimental import pallas as pl
from jax.experimental.pallas import tpu as pltpu

D_MODEL = 1024
SEQ = 16384
D_FF = 2816
N_ADA = 9
LN_EPS = 1e-5
RMS_EPS = 1e-6
NEG_INF = -1e30
DEEPNORM_ALPHA = 2.0 ** 0.25
MACARON_WEIGHT = 0.5

GLA_HEADS = 4
GLA_DK = 128
GLA_DV = 256
GLA_GATE_RANK = 16
GLA_GATE_NORMALIZER = 16.0
GLA_CHUNK = 64
GLA_LEVELS = (32, 16, 8, 4, 2, 1)

SWA_HEADS = 16
SWA_KV_HEADS = 4
SWA_HEAD_DIM = 64
SWA_BLOCK = 128

MIX_SIZES = (512, 512, 1024, 16, 1024, 1024, 256, 256, 1024, 1024)

LANES = 128
V7X_VMEM_LIMIT_BYTES = 56 * 1024 * 1024

BF16 = jnp.bfloat16
F32 = jnp.float32


def _cparams():
    return pltpu.CompilerParams(dimension_semantics=("arbitrary",),
                                vmem_limit_bytes=V7X_VMEM_LIMIT_BYTES)


def _resident(shape):
    return pl.BlockSpec(shape, lambda i: (0,) * len(shape), pipeline_mode=pl.Buffered(1))


def _dot(a, b):
    return jnp.dot(a, b, preferred_element_type=F32)


def _dot_t(a, b):
    return lax.dot_general(a, b, (((1,), (1,)), ((), ())), preferred_element_type=F32)


def _sigmoid(x):
    return 1.0 / (1.0 + jnp.exp(-x))


def _layer_norm(r, g, b):
    mu = jnp.mean(r, axis=-1, keepdims=True)
    d = r - mu
    var = jnp.mean(d * d, axis=-1, keepdims=True)
    return d * lax.rsqrt(var + LN_EPS) * g + b


def _ada_slice(ada_ref, idx):
    return ada_ref[:, idx * D_MODEL:(idx + 1) * D_MODEL]


ADA_TILE_N = 1536


def _ada_kernel(c_ref, w_ref, b_ref, o_ref):
    c = c_ref[...]
    s = (c * _sigmoid(c)).astype(BF16)
    s8 = jnp.broadcast_to(s, (8, D_MODEL))
    y = _dot(s8, w_ref[...].astype(BF16))
    o_ref[...] = y[0:1, :] + b_ref[...]


def _ada(c, w_ada, b_ada):
    n = N_ADA * D_MODEL
    return pl.pallas_call(
        _ada_kernel,
        grid=(n // ADA_TILE_N,),
        in_specs=[pl.BlockSpec((1, D_MODEL), lambda i: (0, 0)),
                  pl.BlockSpec((D_MODEL, ADA_TILE_N), lambda i: (0, i)),
                  pl.BlockSpec((1, ADA_TILE_N), lambda i: (0, i))],
        out_specs=pl.BlockSpec((1, ADA_TILE_N), lambda i: (0, i)),
        out_shape=jax.ShapeDtypeStruct((1, n), F32),
        compiler_params=_cparams(),
        name="ada",
    )(c, w_ada, b_ada.reshape(1, n))


FFN_TILE_M = 512
FFN_SPLIT = 1536


def _ffn_kernel(x_ref, ada_ref, wg_ref, wu_ref, wd_ref, lng_ref, lnb_ref, o_ref, *, ada_base):
    x = x_ref[...]
    sh = _ada_slice(ada_ref, ada_base)
    sc = _ada_slice(ada_ref, ada_base + 1)
    gt = _ada_slice(ada_ref, ada_base + 2)
    h = (x * (1.0 + sc) + sh).astype(BF16)
    y = None
    for lo, hi in ((0, FFN_SPLIT), (FFN_SPLIT, D_FF)):
        g = _dot(h, wg_ref[:, lo:hi])
        u = _dot(h, wu_ref[:, lo:hi])
        a = (g * _sigmoid(g) * u).astype(BF16)
        part = _dot(a, wd_ref[lo:hi, :])
        y = part if y is None else y + part
    r = DEEPNORM_ALPHA * x + (MACARON_WEIGHT * gt) * y
    o_ref[...] = _layer_norm(r, lng_ref[...], lnb_ref[...])


def _ffn(x, ada, wg, wu, wd, ln_g, ln_b, ada_base):
    tm = FFN_TILE_M
    row = pl.BlockSpec((tm, D_MODEL), lambda i: (i, 0))
    return pl.pallas_call(
        functools.partial(_ffn_kernel, ada_base=ada_base),
        grid=(SEQ // tm,),
        in_specs=[row, _resident((1, N_ADA * D_MODEL)),
                  _resident((D_MODEL, D_FF)), _resident((D_MODEL, D_FF)), _resident((D_FF, D_MODEL)),
                  _resident((1, D_MODEL)), _resident((1, D_MODEL))],
        out_specs=row,
        out_shape=jax.ShapeDtypeStruct((SEQ, D_MODEL), F32),
        compiler_params=_cparams(),
        name="ffn",
    )(x, ada, wg, wu, wd, ln_g, ln_b)


PROJ_TILE_M = 512
LR_PAD = LANES


def _log_sigmoid(z):
    return -(jnp.maximum(-z, 0.0) + jnp.log1p(jnp.exp(-jnp.abs(z))))


def _proj_kernel(x_ref, ada_ref, wq_ref, wk_ref, wv_ref, wlr_ref, wup_ref, bup_ref, wgo_ref,
                 wsq_ref, wsk_ref, wsv_ref, wga_ref, wgb_ref,
                 q_ref, k_ref, v_ref, la_ref, go_ref, sq_ref, sk_ref, sv_ref, ga_ref, gb_ref):
    x = x_ref[...]
    h = (x * (1.0 + _ada_slice(ada_ref, 4)) + _ada_slice(ada_ref, 3)).astype(BF16)
    q_ref[...] = _dot(h, wq_ref[...]) * (GLA_DK ** -0.5)
    k_ref[...] = _dot(h, wk_ref[...])
    v_ref[...] = _dot(h, wv_ref[...]).astype(BF16)
    lr = _dot(h, wlr_ref[...]).astype(BF16)
    z = _dot(lr, wup_ref[...]) + bup_ref[...]
    la_ref[...] = _log_sigmoid(z) / GLA_GATE_NORMALIZER
    go = _dot(h, wgo_ref[...])
    go_ref[...] = go * _sigmoid(go)
    sq_ref[...] = _dot(h, wsq_ref[...]).astype(BF16)
    sk_ref[...] = _dot(h, wsk_ref[...]).astype(BF16)
    sv_ref[...] = _dot(h, wsv_ref[...]).astype(BF16)
    ga_ref[...] = _sigmoid(_dot(h, wga_ref[...]))
    gb_ref[...] = _sigmoid(_dot(h, wgb_ref[...]))


def _proj(x, ada, ws, wup, bup):
    tm = PROJ_TILE_M
    wq, wk, wv, wlr, wgo, wsq, wsk, wsv, wga, wgb = ws
    qk = GLA_HEADS * GLA_DK
    kvw = SWA_KV_HEADS * SWA_HEAD_DIM

    def row(n):
        return pl.BlockSpec((tm, n), lambda i: (i, 0))

    def w(n):
        return _resident((D_MODEL, n))

    out_cols = ((qk, F32), (qk, F32), (D_MODEL, BF16), (qk, F32), (D_MODEL, F32),
                (D_MODEL, BF16), (kvw, BF16), (kvw, BF16), (D_MODEL, F32), (D_MODEL, F32))
    return pl.pallas_call(
        _proj_kernel,
        grid=(SEQ // tm,),
        in_specs=[row(D_MODEL), _resident((1, N_ADA * D_MODEL)),
                  w(qk), w(qk), w(D_MODEL), w(LR_PAD), _resident((LR_PAD, qk)), _resident((1, qk)),
                  w(D_MODEL), w(D_MODEL), w(kvw), w(kvw), w(D_MODEL), w(D_MODEL)],
        out_specs=[row(n) for n, _ in out_cols],
        out_shape=[jax.ShapeDtypeStruct((SEQ, n), dt) for n, dt in out_cols],
        compiler_params=_cparams(),
        name="proj",
    )(x, ada, wq, wk, wv, wlr, wup, bup, wgo, wsq, wsk, wsv, wga, wgb)


GLA_TILE_M = 512
GLA_NSEG = 1 + len(GLA_LEVELS)
GLA_SPLIT = 3
GLA_FAST_CHUNK = 128
GLA_SAFE_LOG_DECAY = -60.0


def _gla_constants():
    c = GLA_CHUNK
    t = np.arange(c)[:, None]
    u = np.arange(c)[None, :]
    seg = [(u <= t)]
    mask = [(u == t)]
    for m in GLA_LEVELS:
        blk_t, blk_u = t // m, u // m
        odd = (blk_t % 2) == 1
        p_odd = blk_t * m
        p_even = (blk_t + 1) * m
        seg.append(np.where(odd, (u > p_odd) & (u <= t), (u > t) & (u <= p_even)))
        mask.append(odd & (blk_u == blk_t - 1))
    seg = np.concatenate(seg, axis=0).astype(np.float32)
    seg = np.concatenate([seg] * GLA_SPLIT, axis=1)
    mask = np.stack(mask).astype(np.float32)
    fc = GLA_FAST_CHUNK
    tri = np.tril(np.ones((fc, fc), np.float32))
    tri = np.concatenate([tri] * GLA_SPLIT, axis=1)
    return seg, mask, tri


def _split_terms(g):
    parts = []
    rem = g
    for _ in range(GLA_SPLIT):
        p = rem.astype(BF16)
        parts.append(p)
        rem = rem - p.astype(F32)
    return jnp.concatenate(parts, axis=0)


def _decay_column(bl):
    return jnp.exp(jnp.broadcast_to(bl, (8, GLA_DK))).T[:, 0:1]


def _gla_finish(o, gn_ref, go, o_ref, rows, vs):
    ms = jnp.mean(o * o, axis=-1, keepdims=True)
    on = o * lax.rsqrt(ms + RMS_EPS) * gn_ref[...]
    o_ref[rows, vs] = (on * go).astype(BF16)


def _gla_tile_fast(q_ref, k_ref, v_ref, b_ref, go_ref, gn_ref, o_ref, state_ref):
    c = GLA_FAST_CHUNK
    ti = lax.broadcasted_iota(jnp.int32, (c, c), 0)
    si = lax.broadcasted_iota(jnp.int32, (c, c), 1)
    causal = si <= ti
    for ci in range(GLA_TILE_M // c):
        rows = slice(ci * c, (ci + 1) * c)
        b = b_ref[rows, :]
        b_last = b[c - 1:c]
        e_pos = jnp.exp(b)
        e_neg = jnp.exp(-b)
        e_last = jnp.exp(b_last)
        for hd in range(GLA_HEADS):
            ks = slice(hd * GLA_DK, (hd + 1) * GLA_DK)
            vs = slice(hd * GLA_DV, (hd + 1) * GLA_DV)
            v = v_ref[rows, vs]
            q_in = (q_ref[rows, ks] * e_pos[:, ks]).astype(BF16)
            k_out = k_ref[rows, ks] * e_neg[:, ks]
            a = jnp.where(causal, _dot_t(q_in, k_out.astype(BF16)), 0.0).astype(BF16)
            state = state_ref[hd]
            o = _dot(jnp.concatenate([q_in, a], axis=1),
                     jnp.concatenate([state.astype(BF16), v], axis=0))
            k_dec = k_out * e_last[:, ks]
            state_ref[hd] = _decay_column(b_last[:, ks]) * state + _dot(k_dec.T.astype(BF16), v)
            _gla_finish(o, gn_ref, go_ref[rows, vs], o_ref, rows, vs)


def _gla_tile_any_decay(q_ref, k_ref, v_ref, la_ref, go_ref, seg_ref, mask_ref, gn_ref, o_ref,
                        state_ref):
    c = GLA_CHUNK

    def chunk(ci, carry):
        rows = pl.ds(pl.multiple_of(ci * c, c), c)
        e_all = _dot(seg_ref[...], _split_terms(la_ref[rows, :]))
        b = e_all[0:c]
        b_last = b[c - 1:c]
        for hd in range(GLA_HEADS):
            ks = slice(hd * GLA_DK, (hd + 1) * GLA_DK)
            vs = slice(hd * GLA_DV, (hd + 1) * GLA_DV)
            q = q_ref[rows, ks]
            k = k_ref[rows, ks]
            v = v_ref[rows, vs]
            bh = b[:, ks]
            bl = b_last[:, ks]
            state = state_ref[hd]
            o = _dot((q * jnp.exp(bh)).astype(BF16), state.astype(BF16))
            a = _dot_t(q.astype(BF16), k.astype(BF16)) * mask_ref[0]
            for li in range(len(GLA_LEVELS)):
                e = jnp.exp(e_all[(li + 1) * c:(li + 2) * c, ks])
                a = a + _dot_t((q * e).astype(BF16), (k * e).astype(BF16)) * mask_ref[li + 1]
            o = o + _dot(a.astype(BF16), v)
            k_dec = k * jnp.exp(bl - bh)
            state_ref[hd] = _decay_column(bl) * state + _dot(k_dec.T.astype(BF16), v)
            _gla_finish(o, gn_ref, go_ref[rows, vs], o_ref, rows, vs)
        return carry

    lax.fori_loop(0, GLA_TILE_M // c, chunk, 0)


def _gla_kernel(q_ref, k_ref, v_ref, la_ref, go_ref, seg_ref, mask_ref, tri_ref, gn_ref, o_ref,
                state_ref, b_ref):
    @pl.when(pl.program_id(0) == 0)
    def _():
        state_ref[...] = jnp.zeros_like(state_ref)

    c = GLA_FAST_CHUNK
    lowest = None
    for ci in range(GLA_TILE_M // c):
        rows = slice(ci * c, (ci + 1) * c)
        b = _dot(tri_ref[...], _split_terms(la_ref[rows, :]))
        b_ref[rows, :] = b
        total = jnp.min(b[c - 1:c])
        lowest = total if lowest is None else jnp.minimum(lowest, total)
    safe = lowest >= GLA_SAFE_LOG_DECAY

    @pl.when(safe)
    def _():
        _gla_tile_fast(q_ref, k_ref, v_ref, b_ref, go_ref, gn_ref, o_ref, state_ref)

    @pl.when(jnp.logical_not(safe))
    def _():
        _gla_tile_any_decay(q_ref, k_ref, v_ref, la_ref, go_ref, seg_ref, mask_ref, gn_ref, o_ref,
                            state_ref)


def _gla(q, k, v, la, go, gn):
    tm = GLA_TILE_M
    seg, mask, tri = _gla_constants()
    qk = GLA_HEADS * GLA_DK

    def row(n):
        return pl.BlockSpec((tm, n), lambda i: (i, 0))

    return pl.pallas_call(
        _gla_kernel,
        grid=(SEQ // tm,),
        in_specs=[row(qk), row(qk), row(D_MODEL), row(qk), row(D_MODEL),
                  _resident(seg.shape), _resident(mask.shape), _resident(tri.shape),
                  _resident((1, GLA_DV))],
        out_specs=row(D_MODEL),
        out_shape=jax.ShapeDtypeStruct((SEQ, D_MODEL), BF16),
        scratch_shapes=[pltpu.VMEM((GLA_HEADS, GLA_DK, GLA_DV), F32),
                        pltpu.VMEM((tm, qk), F32)],
        compiler_params=_cparams(),
        name="gla",
    )(q, k, v, la, go, jnp.asarray(seg, BF16), jnp.asarray(mask, F32), jnp.asarray(tri, BF16), gn)


SWA_TILE_M = 512
LOG2E = 1.4426950408889634


def _swa_kernel(slope_ref, sink_ref, q_ref, k_ref, kp_ref, v_ref, vp_ref, o_ref, bias_ref):
    blk = SWA_BLOCK
    hd = SWA_HEAD_DIM
    group = SWA_HEADS // SWA_KV_HEADS
    first_tile = pl.program_id(0) == 0

    qi = lax.broadcasted_iota(jnp.int32, (blk, blk), 0)
    kj = lax.broadcasted_iota(jnp.int32, (blk, blk), 1)
    own = kj <= qi
    first_valid = (kj - qi) <= jnp.where(first_tile, 0, blk)
    lane = lax.broadcasted_iota(jnp.int32, (blk, LANES), 1)
    low = lane < hd

    @pl.when(first_tile)
    def _():
        dist = jnp.where(own, qi - kj, qi + blk - kj).astype(F32)
        for head in range(SWA_HEADS):
            bias_ref[head] = (slope_ref[head] * LOG2E) * dist

    def dup(x, kv):
        tile = x[:, (kv // 2) * LANES:(kv // 2 + 1) * LANES]
        rolled = pltpu.roll(tile, hd, 1)
        lo_half, hi_half = (tile, rolled) if kv % 2 == 0 else (rolled, tile)
        return jnp.where(low, lo_half, hi_half).astype(BF16)

    def dup_all(x):
        xf = x.astype(F32)
        return [dup(xf, kv) for kv in range(SWA_KV_HEADS)]

    zero = jnp.zeros((blk, LANES), BF16)
    prev_k = dup_all(kp_ref[...])
    prev_v = dup_all(vp_ref[...])
    for bi in range(SWA_TILE_M // blk):
        rows = slice(bi * blk, (bi + 1) * blk)
        cur_k = dup_all(k_ref[rows, :])
        cur_v = dup_all(v_ref[rows, :])
        for kv in range(SWA_KV_HEADS):
            lhs = []
            for pair in (2 * kv, 2 * kv + 1):
                x = q_ref[rows, pair * LANES:(pair + 1) * LANES]
                lhs += [jnp.where(low, x, zero), jnp.where(low, zero, x)]
            keys = jnp.concatenate([prev_k[kv], cur_k[kv]], axis=0)
            s_all = _dot_t(jnp.concatenate(lhs, axis=0), keys)
            probs = []
            for gi in range(group):
                head = kv * group + gi
                sc = s_all[gi * blk:(gi + 1) * blk]
                s = jnp.where(own, sc[:, blk:], sc[:, :blk]) * (hd ** -0.5 * LOG2E) - bias_ref[head]
                if bi == 0:
                    s = jnp.where(first_valid, s, NEG_INF)
                sink = sink_ref[head] * LOG2E
                m = jnp.maximum(jnp.max(s, axis=-1, keepdims=True), sink)
                p = jnp.exp2(s - m)
                denom = jnp.sum(p, axis=-1, keepdims=True) + jnp.exp2(sink - m)
                pn = p * (1.0 / denom)
                probs.append(jnp.concatenate([jnp.where(own, 0.0, pn), jnp.where(own, pn, 0.0)],
                                             axis=1).astype(BF16))
            vals = jnp.concatenate([prev_v[kv], cur_v[kv]], axis=0)
            r = _dot(jnp.concatenate(probs, axis=0), vals)
            for pi, pair in enumerate((2 * kv, 2 * kv + 1)):
                r0 = r[(2 * pi) * blk:(2 * pi + 1) * blk]
                r1 = r[(2 * pi + 1) * blk:(2 * pi + 2) * blk]
                o_ref[rows, pair * LANES:(pair + 1) * LANES] = jnp.where(low, r0, r1).astype(BF16)
        prev_k, prev_v = cur_k, cur_v


def _swa(sq, sk, sv, slopes, sinks):
    blk = SWA_BLOCK
    tm = SWA_TILE_M
    kvw = SWA_KV_HEADS * SWA_HEAD_DIM
    cur = lambda i: (i, 0)
    before = lambda i: (jnp.maximum(i * (tm // blk) - 1, 0), 0)
    smem = pl.BlockSpec(memory_space=pltpu.SMEM)
    return pl.pallas_call(
        _swa_kernel,
        grid=(SEQ // tm,),
        in_specs=[smem, smem,
                  pl.BlockSpec((tm, D_MODEL), cur),
                  pl.BlockSpec((tm, kvw), cur), pl.BlockSpec((blk, kvw), before),
                  pl.BlockSpec((tm, kvw), cur), pl.BlockSpec((blk, kvw), before)],
        out_specs=pl.BlockSpec((tm, D_MODEL), cur),
        out_shape=jax.ShapeDtypeStruct((SEQ, D_MODEL), BF16),
        scratch_shapes=[pltpu.VMEM((SWA_HEADS, blk, blk), F32)],
        compiler_params=_cparams(),
        name="swa",
    )(slopes, sinks, sq, sk, sk, sv, sv)


MIX_TILE_M = 512


def _mixout_kernel(x_ref, ada_ref, o_ref, a_ref, ga_ref, gb_ref, wa_ref, wb_ref, wo_ref,
                   lng_ref, lnb_ref, out_ref):
    ya = _dot(o_ref[...], wa_ref[...])
    yb = _dot(a_ref[...], wb_ref[...])
    merged = (ga_ref[...] * ya + gb_ref[...] * yb).astype(BF16)
    y = _dot(merged, wo_ref[...])
    r = DEEPNORM_ALPHA * x_ref[...] + _ada_slice(ada_ref, 5) * y
    out_ref[...] = _layer_norm(r, lng_ref[...], lnb_ref[...])


def _mixout(x, ada, o, a, ga, gb, wa, wb, wo, ln_g, ln_b):
    tm = MIX_TILE_M
    row = pl.BlockSpec((tm, D_MODEL), lambda i: (i, 0))
    wsq = _resident((D_MODEL, D_MODEL))
    vec = _resident((1, D_MODEL))
    return pl.pallas_call(
        _mixout_kernel,
        grid=(SEQ // tm,),
        in_specs=[row, _resident((1, N_ADA * D_MODEL)), row, row, row, row, wsq, wsq, wsq, vec, vec],
        out_specs=row,
        out_shape=jax.ShapeDtypeStruct((SEQ, D_MODEL), F32),
        compiler_params=_cparams(),
        name="mixout",
    )(x, ada, o, a, ga, gb, wa, wb, wo, ln_g, ln_b)


def _alibi_slopes(n):
    return 2.0 ** (-8.0 * jnp.arange(1, n + 1, dtype=jnp.float32) / n)


def _split_w_in(w_in):
    offs = np.concatenate([[0], np.cumsum(MIX_SIZES)])
    cols = [w_in[:, offs[i]:offs[i + 1]].astype(BF16) for i in range(len(MIX_SIZES))]
    cols[3] = jnp.pad(cols[3], ((0, 0), (0, LR_PAD - GLA_GATE_RANK)))
    return cols


def kernel(x, c, w_ada, b_ada, ffn1_w_gate, ffn1_w_up, ffn1_w_down, ln1_g, ln1_b, w_in,
           w_gla_gate_up, b_gla_gate, gla_norm_g, w_branch_gla, swa_sinks, w_branch_swa, w_out,
           ln2_g, ln2_b, ffn2_w_gate, ffn2_w_up, ffn2_w_down, ln3_g, ln3_b):
    assert x.shape == (1, SEQ, D_MODEL) and w_ada.shape[0] == 1
    x2d = x.reshape(SEQ, D_MODEL)
    vec = lambda p: p.reshape(1, -1)

    ada = _ada(c, w_ada[0], b_ada[0])
    x1 = _ffn(x2d, ada, ffn1_w_gate[0].astype(BF16), ffn1_w_up[0].astype(BF16),
              ffn1_w_down[0].astype(BF16), vec(ln1_g), vec(ln1_b), 0)

    wup = jnp.pad(w_gla_gate_up[0], ((0, LR_PAD - GLA_GATE_RANK), (0, 0))).astype(BF16)
    q, k, v, la, go, sq, sk, sv, ga, gb = _proj(x1, ada, _split_w_in(w_in[0]), wup, vec(b_gla_gate))
    o = _gla(q, k, v, la, go, vec(gla_norm_g))
    a = _swa(sq, sk, sv, _alibi_slopes(SWA_HEADS), swa_sinks[0])
    x2 = _mixout(x1, ada, o, a, ga, gb, w_branch_gla[0].astype(BF16), w_branch_swa[0].astype(BF16),
                 w_out[0].astype(BF16), vec(ln2_g), vec(ln2_b))

    out = _ffn(x2, ada, ffn2_w_gate[0].astype(BF16), ffn2_w_up[0].astype(BF16),
               ffn2_w_down[0].astype(BF16), vec(ln3_g), vec(ln3_b), 6)
    return out.reshape(1, SEQ, D_MODEL)
```

```python
import functools

import numpy as np
import jax
import jax.numpy as jnp
from jax import lax
from jax.experimental import pallas as pl
from jax.experimental.pallas import tpu as pltpu

D_MODEL = 1024
SEQ = 16384
D_FF = 2816
N_ADA = 9
LN_EPS = 1e-5
RMS_EPS = 1e-6
NEG_INF = -1e30
DEEPNORM_ALPHA = 2.0 ** 0.25
MACARON_WEIGHT = 0.5

GLA_HEADS = 4
GLA_DK = 128
GLA_DV = 256
GLA_GATE_RANK = 16
GLA_GATE_NORMALIZER = 16.0
GLA_CHUNK = 64
GLA_LEVELS = (32, 16, 8, 4, 2, 1)

SWA_HEADS = 16
SWA_KV_HEADS = 4
SWA_HEAD_DIM = 64
SWA_BLOCK = 128

MIX_SIZES = (512, 512, 1024, 16, 1024, 1024, 256, 256, 1024, 1024)

LANES = 128
V7X_VMEM_LIMIT_BYTES = 60 * 1024 * 1024

BF16 = jnp.bfloat16
F32 = jnp.float32


def _cparams():
    return pltpu.CompilerParams(dimension_semantics=("arbitrary",),
                                vmem_limit_bytes=V7X_VMEM_LIMIT_BYTES)


def _resident(shape):
    return pl.BlockSpec(shape, lambda i: (0,) * len(shape), pipeline_mode=pl.Buffered(1))


def _dot(a, b):
    return jnp.dot(a, b, preferred_element_type=F32)


def _dot_t(a, b):
    return lax.dot_general(a, b, (((1,), (1,)), ((), ())), preferred_element_type=F32)


def _sigmoid(x):
    return 1.0 / (1.0 + jnp.exp(-x))


def _layer_norm(r, g, b):
    mu = jnp.mean(r, axis=-1, keepdims=True)
    d = r - mu
    var = jnp.mean(d * d, axis=-1, keepdims=True)
    return d * lax.rsqrt(var + LN_EPS) * g + b


def _ada_slice(ada_ref, idx):
    return ada_ref[:, idx * D_MODEL:(idx + 1) * D_MODEL]


ADA_TILE_N = 1536


def _ada_kernel(c_ref, w_ref, b_ref, o_ref):
    c = c_ref[...]
    s = (c * _sigmoid(c)).astype(BF16)
    s8 = jnp.broadcast_to(s, (8, D_MODEL))
    y = _dot(s8, w_ref[...].astype(BF16))
    o_ref[...] = y[0:1, :] + b_ref[...]


def _ada(c, w_ada, b_ada):
    n = N_ADA * D_MODEL
    return pl.pallas_call(
        _ada_kernel,
        grid=(n // ADA_TILE_N,),
        in_specs=[pl.BlockSpec((1, D_MODEL), lambda i: (0, 0)),
                  pl.BlockSpec((D_MODEL, ADA_TILE_N), lambda i: (0, i)),
                  pl.BlockSpec((1, ADA_TILE_N), lambda i: (0, i))],
        out_specs=pl.BlockSpec((1, ADA_TILE_N), lambda i: (0, i)),
        out_shape=jax.ShapeDtypeStruct((1, n), F32),
        compiler_params=_cparams(),
        name="ada",
    )(c, w_ada, b_ada.reshape(1, n))


FFN_TILE_M = 512
FFN_SPLIT = 1536


def _ffn_kernel(x_ref, ada_ref, wg_ref, wu_ref, wd_ref, lng_ref, lnb_ref, o_ref, *, ada_base):
    x = x_ref[...]
    sh = _ada_slice(ada_ref, ada_base)
    sc = _ada_slice(ada_ref, ada_base + 1)
    gt = _ada_slice(ada_ref, ada_base + 2)
    h = (x * (1.0 + sc) + sh).astype(BF16)
    y = None
    for lo, hi in ((0, FFN_SPLIT), (FFN_SPLIT, D_FF)):
        g = _dot(h, wg_ref[:, lo:hi])
        u = _dot(h, wu_ref[:, lo:hi])
        a = (g * _sigmoid(g) * u).astype(BF16)
        part = _dot(a, wd_ref[lo:hi, :])
        y = part if y is None else y + part
    r = DEEPNORM_ALPHA * x + (MACARON_WEIGHT * gt) * y
    o_ref[...] = _layer_norm(r, lng_ref[...], lnb_ref[...])


def _ffn(x, ada, wg, wu, wd, ln_g, ln_b, ada_base):
    tm = FFN_TILE_M
    row = pl.BlockSpec((tm, D_MODEL), lambda i: (i, 0))
    return pl.pallas_call(
        functools.partial(_ffn_kernel, ada_base=ada_base),
        grid=(SEQ // tm,),
        in_specs=[row, _resident((1, N_ADA * D_MODEL)),
                  _resident((D_MODEL, D_FF)), _resident((D_MODEL, D_FF)), _resident((D_FF, D_MODEL)),
                  _resident((1, D_MODEL)), _resident((1, D_MODEL))],
        out_specs=row,
        out_shape=jax.ShapeDtypeStruct((SEQ, D_MODEL), F32),
        compiler_params=_cparams(),
        name="ffn",
    )(x, ada, wg, wu, wd, ln_g, ln_b)


MIXER_TILE_M = 512
LR_PAD = LANES
GLA_NSEG = 1 + len(GLA_LEVELS)
GLA_SPLIT = 2
GLA_FAST_CHUNK = 128
GLA_SAFE_LOG_DECAY = -60.0


def _log_sigmoid(z):
    return -(jnp.maximum(-z, 0.0) + jnp.log1p(jnp.exp(-jnp.abs(z))))


def _gla_constants():
    c = GLA_CHUNK
    t = np.arange(c)[:, None]
    u = np.arange(c)[None, :]
    seg = [(u <= t)]
    mask = [(u == t)]
    for m in GLA_LEVELS:
        blk_t, blk_u = t // m, u // m
        odd = (blk_t % 2) == 1
        p_odd = blk_t * m
        p_even = (blk_t + 1) * m
        seg.append(np.where(odd, (u > p_odd) & (u <= t), (u > t) & (u <= p_even)))
        mask.append(odd & (blk_u == blk_t - 1))
    seg = np.concatenate(seg, axis=0).astype(np.float32)
    seg = np.concatenate([seg] * GLA_SPLIT, axis=1)
    mask = np.stack(mask).astype(np.float32)
    fc = GLA_FAST_CHUNK
    tri = np.tril(np.ones((fc, fc), np.float32))
    tri = np.concatenate([tri] * GLA_SPLIT, axis=1)
    return seg, mask, tri


def _split_terms(g):
    parts = []
    rem = g
    for _ in range(GLA_SPLIT):
        p = rem.astype(BF16)
        parts.append(p)
        rem = rem - p.astype(F32)
    return jnp.concatenate(parts, axis=0)


def _decay_column(bl):
    return jnp.exp(jnp.broadcast_to(bl, (8, GLA_DK))).T[:, 0:1]


def _gla_finish(o, gn_ref, go, o_ref, rows, vs):
    ms = jnp.mean(o * o, axis=-1, keepdims=True)
    on = o * lax.rsqrt(ms + RMS_EPS) * gn_ref[...]
    o_ref[rows, vs] = (on * go).astype(BF16)


def _gla_cumsum(la_ref, tri_ref, b_ref):
    c = GLA_FAST_CHUNK
    lowest = None
    for ci in range(MIXER_TILE_M // c):
        rows = slice(ci * c, (ci + 1) * c)
        b = _dot(tri_ref[...], _split_terms(la_ref[rows, :]))
        b_ref[rows, :] = b
        total = jnp.min(b[c - 1:c])
        lowest = total if lowest is None else jnp.minimum(lowest, total)
    return lowest >= GLA_SAFE_LOG_DECAY


def _gla_tile_fast(q_ref, k_ref, v_ref, b_ref, go_ref, gn_ref, o_ref, state_in_ref, state_out_ref):
    c = GLA_FAST_CHUNK
    ti = lax.broadcasted_iota(jnp.int32, (c, c), 0)
    si = lax.broadcasted_iota(jnp.int32, (c, c), 1)
    causal = si <= ti
    chunks = range(MIXER_TILE_M // c)
    heads = range(GLA_HEADS)
    rows = [slice(ci * c, (ci + 1) * c) for ci in chunks]
    kcols = [slice(hd * GLA_DK, (hd + 1) * GLA_DK) for hd in heads]
    vcols = [slice(hd * GLA_DV, (hd + 1) * GLA_DV) for hd in heads]

    q_in, k_out, k_dec_t, d_col = {}, {}, {}, {}
    for ci in chunks:
        b = b_ref[rows[ci], :]
        b_last = b[c - 1:c]
        e_pos = jnp.exp(b)
        e_neg = jnp.exp(-b)
        e_last = jnp.exp(b_last)
        for hd in heads:
            ks = kcols[hd]
            q_in[ci, hd] = (q_ref[rows[ci], ks] * e_pos[:, ks]).astype(BF16)
            ko = k_ref[rows[ci], ks] * e_neg[:, ks]
            k_out[ci, hd] = ko.astype(BF16)
            k_dec_t[ci, hd] = (ko * e_last[:, ks]).T.astype(BF16)
            d_col[ci, hd] = _decay_column(b_last[:, ks])
    a = {u: jnp.where(causal, _dot_t(q_in[u], k_out[u]), 0.0).astype(BF16) for u in q_in}
    upd = {(ci, hd): _dot(k_dec_t[ci, hd], v_ref[rows[ci], vcols[hd]]) for (ci, hd) in q_in}

    for hd in heads:
        state = state_in_ref[hd]
        for ci in chunks:
            v = v_ref[rows[ci], vcols[hd]]
            o = _dot(jnp.concatenate([q_in[ci, hd], a[ci, hd]], axis=1),
                     jnp.concatenate([state.astype(BF16), v], axis=0))
            state = d_col[ci, hd] * state + upd[ci, hd]
            _gla_finish(o, gn_ref, go_ref[rows[ci], vcols[hd]], o_ref, rows[ci], vcols[hd])
        state_out_ref[hd] = state


def _gla_tile_any_decay(q_ref, k_ref, v_ref, la_ref, go_ref, seg_ref, mask_ref, gn_ref, o_ref,
                        state_ref):
    c = GLA_CHUNK

    def chunk(ci, carry):
        rows = pl.ds(pl.multiple_of(ci * c, c), c)
        e_all = _dot(seg_ref[...], _split_terms(la_ref[rows, :]))
        b = e_all[0:c]
        b_last = b[c - 1:c]
        for hd in range(GLA_HEADS):
            ks = slice(hd * GLA_DK, (hd + 1) * GLA_DK)
            vs = slice(hd * GLA_DV, (hd + 1) * GLA_DV)
            q = q_ref[rows, ks]
            k = k_ref[rows, ks]
            v = v_ref[rows, vs]
            bh = b[:, ks]
            bl = b_last[:, ks]
            state = state_ref[hd]
            o = _dot((q * jnp.exp(bh)).astype(BF16), state.astype(BF16))
            a = _dot_t(q.astype(BF16), k.astype(BF16)) * mask_ref[0]
            for li in range(len(GLA_LEVELS)):
                e = jnp.exp(e_all[(li + 1) * c:(li + 2) * c, ks])
                a = a + _dot_t((q * e).astype(BF16), (k * e).astype(BF16)) * mask_ref[li + 1]
            o = o + _dot(a.astype(BF16), v)
            k_dec = k * jnp.exp(bl - bh)
            state_ref[hd] = _decay_column(bl) * state + _dot(k_dec.T.astype(BF16), v)
            _gla_finish(o, gn_ref, go_ref[rows, vs], o_ref, rows, vs)
        return carry

    lax.fori_loop(0, MIXER_TILE_M // c, chunk, 0)


LOG2E = 1.4426950408889634


def _swa_bias_table(slope_ref, bias_ref):
    blk = SWA_BLOCK
    qi = lax.broadcasted_iota(jnp.int32, (blk, blk), 0)
    kj = lax.broadcasted_iota(jnp.int32, (blk, blk), 1)
    dist = jnp.where(kj <= qi, qi - kj, qi + blk - kj).astype(F32)
    for head in range(SWA_HEADS):
        bias_ref[head] = (slope_ref[head] * LOG2E) * dist


def _swa_tile(sink_ref, q, k, v, kprev_ref, vprev_ref, bias_ref, first_tile, o_ref):
    blk = SWA_BLOCK
    hd = SWA_HEAD_DIM
    group = SWA_HEADS // SWA_KV_HEADS
    kvs = range(SWA_KV_HEADS)

    qi = lax.broadcasted_iota(jnp.int32, (blk, blk), 0)
    kj = lax.broadcasted_iota(jnp.int32, (blk, blk), 1)
    own = kj <= qi
    first_valid = (kj - qi) <= jnp.where(first_tile, 0, blk)
    lane = lax.broadcasted_iota(jnp.int32, (blk, LANES), 1)
    low = lane < hd

    def dup(x, kv):
        tile = x[:, (kv // 2) * LANES:(kv // 2 + 1) * LANES]
        rolled = pltpu.roll(tile, hd, 1)
        lo_half, hi_half = (tile, rolled) if kv % 2 == 0 else (rolled, tile)
        return jnp.where(low, lo_half, hi_half).astype(BF16)

    def dup_all(x):
        xf = x.astype(F32)
        return [dup(xf, kv) for kv in kvs]

    zero = jnp.zeros((blk, LANES), BF16)
    prev_k = [kprev_ref[kv] for kv in kvs]
    prev_v = [vprev_ref[kv] for kv in kvs]
    for bi in range(MIXER_TILE_M // blk):
        rows = slice(bi * blk, (bi + 1) * blk)
        cur_k = dup_all(k[rows, :])
        cur_v = dup_all(v[rows, :])
        for kv in kvs:
            lhs = []
            for pair in (2 * kv, 2 * kv + 1):
                x = q[rows, pair * LANES:(pair + 1) * LANES]
                lhs += [jnp.where(low, x, zero), jnp.where(low, zero, x)]
            keys = jnp.concatenate([prev_k[kv], cur_k[kv]], axis=0)
            s_all = _dot_t(jnp.concatenate(lhs, axis=0), keys)
            probs = []
            for gi in range(group):
                head = kv * group + gi
                sc = s_all[gi * blk:(gi + 1) * blk]
                s = jnp.where(own, sc[:, blk:], sc[:, :blk]) * (hd ** -0.5 * LOG2E) - bias_ref[head]
                if bi == 0:
                    s = jnp.where(first_valid, s, NEG_INF)
                sink = sink_ref[head] * LOG2E
                m = jnp.maximum(jnp.max(s, axis=-1, keepdims=True), sink)
                p = jnp.exp2(s - m)
                denom = jnp.sum(p, axis=-1, keepdims=True) + jnp.exp2(sink - m)
                pn = p * (1.0 / denom)
                probs.append(jnp.concatenate([jnp.where(own, 0.0, pn), jnp.where(own, pn, 0.0)],
                                             axis=1).astype(BF16))
            vals = jnp.concatenate([prev_v[kv], cur_v[kv]], axis=0)
            r = _dot(jnp.concatenate(probs, axis=0), vals)
            for pi, pair in enumerate((2 * kv, 2 * kv + 1)):
                r0 = r[(2 * pi) * blk:(2 * pi + 1) * blk]
                r1 = r[(2 * pi + 1) * blk:(2 * pi + 2) * blk]
                o_ref[rows, pair * LANES:(pair + 1) * LANES] = jnp.where(low, r0, r1).astype(BF16)
        prev_k, prev_v = cur_k, cur_v
    for kv in kvs:
        kprev_ref[kv] = prev_k[kv]
        vprev_ref[kv] = prev_v[kv]


def _mixer_kernel(slope_ref, sink_ref, x_ref, ada_ref,
                  wq_ref, wk_ref, wv_ref, wlr_ref, wgo_ref, wsq_ref, wsk_ref, wsv_ref, wga_ref, wgb_ref,
                  wup_ref, bup_ref, seg_ref, mask_ref, tri_ref, gn_ref,
                  wa_ref, wb_ref, wo_ref, lng_ref, lnb_ref,
                  out_ref,
                  q_s, k_s, v_s, la_s, go_s, b_s, o_s, a_s, state_ref, state_new, kprev_s, vprev_s,
                  bias_ref):
    first_tile = pl.program_id(0) == 0

    @pl.when(first_tile)
    def _():
        state_ref[...] = jnp.zeros_like(state_ref)
        kprev_s[...] = jnp.zeros_like(kprev_s)
        vprev_s[...] = jnp.zeros_like(vprev_s)
        _swa_bias_table(slope_ref, bias_ref)

    x = x_ref[...]
    h = (x * (1.0 + _ada_slice(ada_ref, 4)) + _ada_slice(ada_ref, 3)).astype(BF16)

    sq = _dot(h, wsq_ref[...]).astype(BF16)
    sk = _dot(h, wsk_ref[...]).astype(BF16)
    sv = _dot(h, wsv_ref[...]).astype(BF16)
    _swa_tile(sink_ref, sq, sk, sv, kprev_s, vprev_s, bias_ref, first_tile, a_s)

    q_s[...] = _dot(h, wq_ref[...]) * (GLA_DK ** -0.5)
    k_s[...] = _dot(h, wk_ref[...])
    v_s[...] = _dot(h, wv_ref[...]).astype(BF16)
    lr = _dot(h, wlr_ref[...]).astype(BF16)
    z = _dot(lr, wup_ref[...]) + bup_ref[...]
    la_s[...] = _log_sigmoid(z) / GLA_GATE_NORMALIZER
    go = _dot(h, wgo_ref[...])
    go_s[...] = go * _sigmoid(go)
    safe = _gla_cumsum(la_s, tri_ref, b_s)
    _gla_tile_fast(q_s, k_s, v_s, b_s, go_s, gn_ref, o_s, state_ref, state_new)

    @pl.when(jnp.logical_not(safe))
    def _():
        state_new[...] = state_ref[...]
        _gla_tile_any_decay(q_s, k_s, v_s, la_s, go_s, seg_ref, mask_ref, gn_ref, o_s, state_new)

    state_ref[...] = state_new[...]

    ga = _sigmoid(_dot(h, wga_ref[...]))
    gb = _sigmoid(_dot(h, wgb_ref[...]))
    ya = _dot(o_s[...], wa_ref[...])
    yb = _dot(a_s[...], wb_ref[...])
    merged = (ga * ya + gb * yb).astype(BF16)
    y = _dot(merged, wo_ref[...])
    r = DEEPNORM_ALPHA * x + _ada_slice(ada_ref, 5) * y
    out_ref[...] = _layer_norm(r, lng_ref[...], lnb_ref[...])


def _mixer(x, ada, ws, wup, bup, gn, slopes, sinks, wa, wb, wo, ln_g, ln_b):
    tm = MIXER_TILE_M
    wq, wk, wv, wlr, wgo, wsq, wsk, wsv, wga, wgb = ws
    seg, mask, tri = _gla_constants()
    qk = GLA_HEADS * GLA_DK
    kvw = SWA_KV_HEADS * SWA_HEAD_DIM
    row = pl.BlockSpec((tm, D_MODEL), lambda i: (i, 0))
    smem = pl.BlockSpec(memory_space=pltpu.SMEM)

    def w(n):
        return _resident((D_MODEL, n))

    vec = _resident((1, D_MODEL))
    return pl.pallas_call(
        _mixer_kernel,
        grid=(SEQ // tm,),
        in_specs=[smem, smem, row, _resident((1, N_ADA * D_MODEL)),
                  w(qk), w(qk), w(D_MODEL), w(LR_PAD), w(D_MODEL), w(D_MODEL), w(kvw), w(kvw),
                  w(D_MODEL), w(D_MODEL),
                  _resident((LR_PAD, qk)), _resident((1, qk)),
                  _resident(seg.shape), _resident(mask.shape), _resident(tri.shape),
                  _resident((1, GLA_DV)),
                  w(D_MODEL), w(D_MODEL), w(D_MODEL), vec, vec],
        out_specs=row,
        out_shape=jax.ShapeDtypeStruct((SEQ, D_MODEL), F32),
        scratch_shapes=[pltpu.VMEM((tm, qk), F32), pltpu.VMEM((tm, qk), F32),
                        pltpu.VMEM((tm, D_MODEL), BF16), pltpu.VMEM((tm, qk), F32),
                        pltpu.VMEM((tm, D_MODEL), F32), pltpu.VMEM((tm, qk), F32),
                        pltpu.VMEM((tm, D_MODEL), BF16), pltpu.VMEM((tm, D_MODEL), BF16),
                        pltpu.VMEM((GLA_HEADS, GLA_DK, GLA_DV), F32),
                        pltpu.VMEM((GLA_HEADS, GLA_DK, GLA_DV), F32),
                        pltpu.VMEM((SWA_KV_HEADS, SWA_BLOCK, LANES), BF16),
                        pltpu.VMEM((SWA_KV_HEADS, SWA_BLOCK, LANES), BF16),
                        pltpu.VMEM((SWA_HEADS, SWA_BLOCK, SWA_BLOCK), F32)],
        compiler_params=_cparams(),
        name="mixer",
    )(slopes, sinks, x, ada, wq, wk, wv, wlr, wgo, wsq, wsk, wsv, wga, wgb, wup, bup,
      jnp.asarray(seg, BF16), jnp.asarray(mask, F32), jnp.asarray(tri, BF16), gn,
      wa, wb, wo, ln_g, ln_b)


def _alibi_slopes(n):
    return 2.0 ** (-8.0 * jnp.arange(1, n + 1, dtype=jnp.float32) / n)


def _split_w_in(w_in):
    offs = np.concatenate([[0], np.cumsum(MIX_SIZES)])
    cols = [w_in[:, offs[i]:offs[i + 1]].astype(BF16) for i in range(len(MIX_SIZES))]
    cols[3] = jnp.pad(cols[3], ((0, 0), (0, LR_PAD - GLA_GATE_RANK)))
    return cols


def kernel(x, c, w_ada, b_ada, ffn1_w_gate, ffn1_w_up, ffn1_w_down, ln1_g, ln1_b, w_in,
           w_gla_gate_up, b_gla_gate, gla_norm_g, w_branch_gla, swa_sinks, w_branch_swa, w_out,
           ln2_g, ln2_b, ffn2_w_gate, ffn2_w_up, ffn2_w_down, ln3_g, ln3_b):
    assert x.shape == (1, SEQ, D_MODEL) and w_ada.shape[0] == 1
    x2d = x.reshape(SEQ, D_MODEL)
    vec = lambda p: p.reshape(1, -1)

    ada = _ada(c, w_ada[0], b_ada[0])
    x1 = _ffn(x2d, ada, ffn1_w_gate[0].astype(BF16), ffn1_w_up[0].astype(BF16),
              ffn1_w_down[0].astype(BF16), vec(ln1_g), vec(ln1_b), 0)

    wup = jnp.pad(w_gla_gate_up[0], ((0, LR_PAD - GLA_GATE_RANK), (0, 0))).astype(BF16)
    x2 = _mixer(x1, ada, _split_w_in(w_in[0]), wup, vec(b_gla_gate), vec(gla_norm_g),
                _alibi_slopes(SWA_HEADS), swa_sinks[0],
                w_branch_gla[0].astype(BF16), w_branch_swa[0].astype(BF16), w_out[0].astype(BF16),
                vec(ln2_g), vec(ln2_b))

    out = _ffn(x2, ada, ffn2_w_gate[0].astype(BF16), ffn2_w_up[0].astype(BF16),
               ffn2_w_down[0].astype(BF16), vec(ln3_g), vec(ln3_b), 6)
    return out.reshape(1, SEQ, D_MODEL)
```

```python
import functools

import numpy as np
import jax
import jax.numpy as jnp
from jax import lax
from jax.experimental import pallas as pl
from jax.experimental.pallas import tpu as pltpu

D_MODEL = 1024
SEQ = 16384
D_FF = 2816
N_ADA = 9
LN_EPS = 1e-5
RMS_EPS = 1e-6
NEG_INF = -1e30
DEEPNORM_ALPHA = 2.0 ** 0.25
MACARON_WEIGHT = 0.5

GLA_HEADS = 4
GLA_DK = 128
GLA_DV = 256
GLA_GATE_RANK = 16
GLA_GATE_NORMALIZER = 16.0
GLA_CHUNK = 64
GLA_LEVELS = (32, 16, 8, 4, 2, 1)

SWA_HEADS = 16
SWA_KV_HEADS = 4
SWA_HEAD_DIM = 64
SWA_BLOCK = 128

MIX_SIZES = (512, 512, 1024, 16, 1024, 1024, 256, 256, 1024, 1024)

LANES = 128
V7X_VMEM_LIMIT_BYTES = 60 * 1024 * 1024

BF16 = jnp.bfloat16
F32 = jnp.float32


def _cparams():
    return pltpu.CompilerParams(dimension_semantics=("arbitrary",),
                                vmem_limit_bytes=V7X_VMEM_LIMIT_BYTES)


def _resident(shape):
    return pl.BlockSpec(shape, lambda i: (0,) * len(shape), pipeline_mode=pl.Buffered(1))


def _dot(a, b):
    return jnp.dot(a, b, preferred_element_type=F32)


def _dot_t(a, b):
    return lax.dot_general(a, b, (((1,), (1,)), ((), ())), preferred_element_type=F32)


def _sigmoid(x):
    return 1.0 / (1.0 + jnp.exp(-x))


def _layer_norm(r, g, b):
    mu = jnp.mean(r, axis=-1, keepdims=True)
    d = r - mu
    var = jnp.mean(d * d, axis=-1, keepdims=True)
    return d * lax.rsqrt(var + LN_EPS) * g + b


def _ada_slice(ada_ref, idx):
    return ada_ref[:, idx * D_MODEL:(idx + 1) * D_MODEL]


ADA_TILE_N = 1536


def _ada_kernel(c_ref, w_ref, b_ref, o_ref):
    c = c_ref[...]
    s = (c * _sigmoid(c)).astype(BF16)
    s8 = jnp.broadcast_to(s, (8, D_MODEL))
    y = _dot(s8, w_ref[...].astype(BF16))
    o_ref[...] = y[0:1, :] + b_ref[...]


def _ada(c, w_ada, b_ada):
    n = N_ADA * D_MODEL
    return pl.pallas_call(
        _ada_kernel,
        grid=(n // ADA_TILE_N,),
        in_specs=[pl.BlockSpec((1, D_MODEL), lambda i: (0, 0)),
                  pl.BlockSpec((D_MODEL, ADA_TILE_N), lambda i: (0, i)),
                  pl.BlockSpec((1, ADA_TILE_N), lambda i: (0, i))],
        out_specs=pl.BlockSpec((1, ADA_TILE_N), lambda i: (0, i)),
        out_shape=jax.ShapeDtypeStruct((1, n), F32),
        compiler_params=_cparams(),
        name="ada",
    )(c, w_ada, b_ada.reshape(1, n))


FFN_TILE_M = 512
FFN_SPLIT = 1536


def _ffn_kernel(x_ref, ada_ref, wg_ref, wu_ref, wd_ref, lng_ref, lnb_ref, o_ref, *, ada_base):
    x = x_ref[...]
    sh = _ada_slice(ada_ref, ada_base)
    sc = _ada_slice(ada_ref, ada_base + 1)
    gt = _ada_slice(ada_ref, ada_base + 2)
    h = (x * (1.0 + sc) + sh).astype(BF16)
    y = None
    for lo, hi in ((0, FFN_SPLIT), (FFN_SPLIT, D_FF)):
        g = _dot(h, wg_ref[:, lo:hi])
        u = _dot(h, wu_ref[:, lo:hi])
        a = (g * _sigmoid(g) * u).astype(BF16)
        part = _dot(a, wd_ref[lo:hi, :])
        y = part if y is None else y + part
    r = DEEPNORM_ALPHA * x + (MACARON_WEIGHT * gt) * y
    o_ref[...] = _layer_norm(r, lng_ref[...], lnb_ref[...])


def _ffn(x, ada, wg, wu, wd, ln_g, ln_b, ada_base):
    tm = FFN_TILE_M
    row = pl.BlockSpec((tm, D_MODEL), lambda i: (i, 0))
    return pl.pallas_call(
        functools.partial(_ffn_kernel, ada_base=ada_base),
        grid=(SEQ // tm,),
        in_specs=[row, _resident((1, N_ADA * D_MODEL)),
                  _resident((D_MODEL, D_FF)), _resident((D_MODEL, D_FF)), _resident((D_FF, D_MODEL)),
                  _resident((1, D_MODEL)), _resident((1, D_MODEL))],
        out_specs=row,
        out_shape=jax.ShapeDtypeStruct((SEQ, D_MODEL), F32),
        compiler_params=_cparams(),
        name="ffn",
    )(x, ada, wg, wu, wd, ln_g, ln_b)


MIXER_TILE_M = 512
MIXER_DOT_COLS = 512
LR_PAD = LANES
GLA_NSEG = 1 + len(GLA_LEVELS)
GLA_SPLIT = 2
GLA_FAST_CHUNK = 128
GLA_SAFE_LOG_DECAY = -60.0


def _log_sigmoid(z):
    return jnp.minimum(z, 0.0) - jnp.log(1.0 + jnp.exp(-jnp.abs(z)))


def _gla_constants():
    c = GLA_CHUNK
    t = np.arange(c)[:, None]
    u = np.arange(c)[None, :]
    seg = [(u <= t)]
    mask = [(u == t)]
    for m in GLA_LEVELS:
        blk_t, blk_u = t // m, u // m
        odd = (blk_t % 2) == 1
        p_odd = blk_t * m
        p_even = (blk_t + 1) * m
        seg.append(np.where(odd, (u > p_odd) & (u <= t), (u > t) & (u <= p_even)))
        mask.append(odd & (blk_u == blk_t - 1))
    seg = np.concatenate(seg, axis=0).astype(np.float32)
    seg = np.concatenate([seg] * GLA_SPLIT, axis=1)
    mask = np.stack(mask).astype(np.float32)
    fc = GLA_FAST_CHUNK
    tri = np.tril(np.ones((fc, fc), np.float32))
    tri = np.concatenate([tri] * GLA_SPLIT, axis=1)
    return seg, mask, tri


def _split_terms(g):
    parts = []
    rem = g
    for _ in range(GLA_SPLIT):
        p = rem.astype(BF16)
        parts.append(p)
        rem = rem - p.astype(F32)
    return jnp.concatenate(parts, axis=0)


def _decay_column(bl):
    return jnp.exp(jnp.broadcast_to(bl, (8, GLA_DK))).T[:, 0:1]


def _gla_finish(o, gn_ref, go, o_ref, rows, vs):
    ms = jnp.mean(o * o, axis=-1, keepdims=True)
    on = o * lax.rsqrt(ms + RMS_EPS) * gn_ref[...]
    o_ref[rows, vs] = (on * go).astype(BF16)


def _gla_cumsum(la_ref, tri_ref, b_ref, result):
    c = GLA_FAST_CHUNK
    lowest = None
    for ci in range(MIXER_TILE_M // c):
        rows = slice(ci * c, (ci + 1) * c)
        b = _dot(tri_ref[...], _split_terms(la_ref[rows, :]))
        b_ref[rows, :] = b
        total = jnp.min(b[c - 1:c])
        lowest = total if lowest is None else jnp.minimum(lowest, total)
        result["safe"] = lowest >= GLA_SAFE_LOG_DECAY
        yield


def _gla_tile_fast(q_ref, k_ref, v_ref, b_ref, go_ref, gn_ref, o_ref, state_in_ref, state_out_ref):
    c = GLA_FAST_CHUNK
    ti = lax.broadcasted_iota(jnp.int32, (c, c), 0)
    si = lax.broadcasted_iota(jnp.int32, (c, c), 1)
    causal = si <= ti
    chunks = range(MIXER_TILE_M // c)
    heads = range(GLA_HEADS)
    rows = [slice(ci * c, (ci + 1) * c) for ci in chunks]
    kcols = [slice(hd * GLA_DK, (hd + 1) * GLA_DK) for hd in heads]
    vcols = [slice(hd * GLA_DV, (hd + 1) * GLA_DV) for hd in heads]

    q_in, a, upd, d_col = {}, {}, {}, {}
    for ci in chunks:
        b = b_ref[rows[ci], :]
        b_last = b[c - 1:c]
        e_pos = jnp.exp(b)
        e_neg = jnp.exp(-b)
        e_last = jnp.exp(b_last)
        for hd in heads:
            ks = kcols[hd]
            q_in[ci, hd] = (q_ref[rows[ci], ks] * e_pos[:, ks]).astype(BF16)
            k_out = k_ref[rows[ci], ks] * e_neg[:, ks]
            a[ci, hd] = jnp.where(causal, _dot_t(q_in[ci, hd], k_out.astype(BF16)), 0.0).astype(BF16)
            k_dec_t = (k_out * e_last[:, ks]).T.astype(BF16)
            upd[ci, hd] = _dot(k_dec_t, v_ref[rows[ci], vcols[hd]])
            d_col[ci, hd] = _decay_column(b_last[:, ks])
        yield

    for hd in heads:
        state = state_in_ref[hd]
        for ci in chunks:
            v = v_ref[rows[ci], vcols[hd]]
            o = _dot(jnp.concatenate([q_in[ci, hd], a[ci, hd]], axis=1),
                     jnp.concatenate([state.astype(BF16), v], axis=0))
            state = d_col[ci, hd] * state + upd[ci, hd]
            _gla_finish(o, gn_ref, go_ref[rows[ci], vcols[hd]], o_ref, rows[ci], vcols[hd])
        state_out_ref[hd] = state
        yield


def _gla_tile_any_decay(q_ref, k_ref, v_ref, la_ref, go_ref, seg_ref, mask_ref, gn_ref, o_ref,
                        state_ref):
    c = GLA_CHUNK

    def chunk(ci, carry):
        rows = pl.ds(pl.multiple_of(ci * c, c), c)
        e_all = _dot(seg_ref[...], _split_terms(la_ref[rows, :]))
        b = e_all[0:c]
        b_last = b[c - 1:c]
        for hd in range(GLA_HEADS):
            ks = slice(hd * GLA_DK, (hd + 1) * GLA_DK)
            vs = slice(hd * GLA_DV, (hd + 1) * GLA_DV)
            q = q_ref[rows, ks]
            k = k_ref[rows, ks]
            v = v_ref[rows, vs]
            bh = b[:, ks]
            bl = b_last[:, ks]
            state = state_ref[hd]
            o = _dot((q * jnp.exp(bh)).astype(BF16), state.astype(BF16))
            a = _dot_t(q.astype(BF16), k.astype(BF16)) * mask_ref[0]
            for li in range(len(GLA_LEVELS)):
                e = jnp.exp(e_all[(li + 1) * c:(li + 2) * c, ks])
                a = a + _dot_t((q * e).astype(BF16), (k * e).astype(BF16)) * mask_ref[li + 1]
            o = o + _dot(a.astype(BF16), v)
            k_dec = k * jnp.exp(bl - bh)
            state_ref[hd] = _decay_column(bl) * state + _dot(k_dec.T.astype(BF16), v)
            _gla_finish(o, gn_ref, go_ref[rows, vs], o_ref, rows, vs)
        return carry

    lax.fori_loop(0, MIXER_TILE_M // c, chunk, 0)


LOG2E = 1.4426950408889634


def _swa_bias_table(slope_ref, bias_ref):
    blk = SWA_BLOCK
    qi = lax.broadcasted_iota(jnp.int32, (blk, blk), 0)
    kj = lax.broadcasted_iota(jnp.int32, (blk, blk), 1)
    dist = jnp.where(kj <= qi, qi - kj, qi + blk - kj).astype(F32)
    for head in range(SWA_HEADS):
        bias_ref[head] = (slope_ref[head] * LOG2E) * dist


def _swa_tile(sink_ref, q, k, v, kprev_ref, vprev_ref, bias_ref, first_tile, o_ref):
    blk = SWA_BLOCK
    hd = SWA_HEAD_DIM
    group = SWA_HEADS // SWA_KV_HEADS
    kvs = range(SWA_KV_HEADS)

    qi = lax.broadcasted_iota(jnp.int32, (blk, blk), 0)
    kj = lax.broadcasted_iota(jnp.int32, (blk, blk), 1)
    own = kj <= qi
    first_valid = (kj - qi) <= jnp.where(first_tile, 0, blk)
    lane = lax.broadcasted_iota(jnp.int32, (blk, LANES), 1)
    low = lane < hd

    def dup(x, kv):
        tile = x[:, (kv // 2) * LANES:(kv // 2 + 1) * LANES]
        rolled = pltpu.roll(tile, hd, 1)
        lo_half, hi_half = (tile, rolled) if kv % 2 == 0 else (rolled, tile)
        return jnp.where(low, lo_half, hi_half).astype(BF16)

    def dup_all(x):
        xf = x.astype(F32)
        return [dup(xf, kv) for kv in kvs]

    zero = jnp.zeros((blk, LANES), BF16)
    prev_k = [kprev_ref[kv] for kv in kvs]
    prev_v = [vprev_ref[kv] for kv in kvs]
    for bi in range(MIXER_TILE_M // blk):
        rows = slice(bi * blk, (bi + 1) * blk)
        cur_k = dup_all(k[rows, :])
        cur_v = dup_all(v[rows, :])
        for kv in kvs:
            lhs = []
            for pair in (2 * kv, 2 * kv + 1):
                x = q[rows, pair * LANES:(pair + 1) * LANES]
                lhs += [jnp.where(low, x, zero), jnp.where(low, zero, x)]
            keys = jnp.concatenate([prev_k[kv], cur_k[kv]], axis=0)
            s_all = _dot_t(jnp.concatenate(lhs, axis=0), keys)
            probs = []
            for gi in range(group):
                head = kv * group + gi
                sc = s_all[gi * blk:(gi + 1) * blk]
                s = jnp.where(own, sc[:, blk:], sc[:, :blk]) * (hd ** -0.5 * LOG2E) - bias_ref[head]
                if bi == 0:
                    s = jnp.where(first_valid, s, NEG_INF)
                sink = sink_ref[head] * LOG2E
                m = jnp.maximum(jnp.max(s, axis=-1, keepdims=True), sink)
                p = jnp.exp2(s - m)
                denom = jnp.sum(p, axis=-1, keepdims=True) + jnp.exp2(sink - m)
                pn = p * (1.0 / denom)
                probs.append(jnp.concatenate([jnp.where(own, 0.0, pn), jnp.where(own, pn, 0.0)],
                                             axis=1).astype(BF16))
            vals = jnp.concatenate([prev_v[kv], cur_v[kv]], axis=0)
            r = _dot(jnp.concatenate(probs, axis=0), vals)
            for pi, pair in enumerate((2 * kv, 2 * kv + 1)):
                r0 = r[(2 * pi) * blk:(2 * pi + 1) * blk]
                r1 = r[(2 * pi + 1) * blk:(2 * pi + 2) * blk]
                o_ref[rows, pair * LANES:(pair + 1) * LANES] = jnp.where(low, r0, r1).astype(BF16)
            yield
        prev_k, prev_v = cur_k, cur_v
    for kv in kvs:
        kprev_ref[kv] = prev_k[kv]
        vprev_ref[kv] = prev_v[kv]


def _interleave(first, second, ratio):
    live_first, live_second = True, True
    while live_first or live_second:
        for _ in range(ratio):
            if live_first:
                live_first = next(first, "done") != "done"
        if live_second:
            live_second = next(second, "done") != "done"


def _mixer_kernel(slope_ref, sink_ref, x_ref, ada_ref,
                  wq_ref, wk_ref, wv_ref, wlr_ref, wgo_ref, wsq_ref, wsk_ref, wsv_ref, wga_ref, wgb_ref,
                  wup_ref, bup_ref, seg_ref, mask_ref, tri_ref, gn_ref,
                  wa_ref, wb_ref, wo_ref, lng_ref, lnb_ref,
                  out_ref,
                  q_s, k_s, v_s, la_s, go_s, b_s, o_s, a_s, state_ref, state_new, kprev_s, vprev_s,
                  bias_ref):
    first_tile = pl.program_id(0) == 0

    @pl.when(first_tile)
    def _():
        state_ref[...] = jnp.zeros_like(state_ref)
        kprev_s[...] = jnp.zeros_like(kprev_s)
        vprev_s[...] = jnp.zeros_like(vprev_s)
        _swa_bias_table(slope_ref, bias_ref)

    x = x_ref[...]
    h = (x * (1.0 + _ada_slice(ada_ref, 4)) + _ada_slice(ada_ref, 3)).astype(BF16)
    step = MIXER_DOT_COLS
    col_blocks = range(0, D_MODEL, step)

    sq = _dot(h, wsq_ref[...]).astype(BF16)
    sk = _dot(h, wsk_ref[...]).astype(BF16)
    sv = _dot(h, wsv_ref[...]).astype(BF16)

    gates = {}
    guard = {}

    def gate(name, w_ref, lo):
        gates[name, lo] = _sigmoid(_dot(h, w_ref[:, lo:lo + step]))

    def projections_1():
        lr = _dot(h, wlr_ref[...]).astype(BF16)
        z = _dot(lr, wup_ref[...]) + bup_ref[...]
        la_s[...] = _log_sigmoid(z) * (1.0 / GLA_GATE_NORMALIZER)
        yield
        cumsum = _gla_cumsum(la_s, tri_ref, b_s, guard)
        for lo in range(0, GLA_HEADS * GLA_DK, step):
            q_s[:, lo:lo + step] = _dot(h, wq_ref[:, lo:lo + step]) * (GLA_DK ** -0.5)
            next(cumsum)
            yield
            k_s[:, lo:lo + step] = _dot(h, wk_ref[:, lo:lo + step])
            next(cumsum)
            yield
        for lo in col_blocks:
            v_s[:, lo:lo + step] = _dot(h, wv_ref[:, lo:lo + step]).astype(BF16)
            next(cumsum, None)
            yield
        for _ in cumsum:
            pass
        for lo in col_blocks:
            go = _dot(h, wgo_ref[:, lo:lo + step])
            go_s[:, lo:lo + step] = go * _sigmoid(go)
            yield
        for lo in col_blocks:
            gate("a", wga_ref, lo)
            yield

    _interleave(_swa_tile(sink_ref, sq, sk, sv, kprev_s, vprev_s, bias_ref, first_tile, a_s),
                projections_1(), ratio=2)

    def projections_2():
        for lo in col_blocks:
            gate("b", wgb_ref, lo)
            yield
        for lo in col_blocks:
            gates["yb", lo] = _dot(a_s[...], wb_ref[:, lo:lo + step])
            yield

    _interleave(_gla_tile_fast(q_s, k_s, v_s, b_s, go_s, gn_ref, o_s, state_ref, state_new),
                projections_2(), ratio=2)
    safe = guard["safe"]

    @pl.when(jnp.logical_not(safe))
    def _():
        state_new[...] = state_ref[...]
        _gla_tile_any_decay(q_s, k_s, v_s, la_s, go_s, seg_ref, mask_ref, gn_ref, o_s, state_new)

    state_ref[...] = state_new[...]

    gate_a = jnp.concatenate([gates["a", lo] for lo in col_blocks], axis=1)
    gate_b_yb = jnp.concatenate([gates["b", lo] * gates["yb", lo] for lo in col_blocks], axis=1)
    rows_half = MIXER_TILE_M // 2
    for lo in (0, rows_half):
        rows = slice(lo, lo + rows_half)
        ya = _dot(o_s[rows, :], wa_ref[...])
        merged = (gate_a[rows, :] * ya + gate_b_yb[rows, :]).astype(BF16)
        y = _dot(merged, wo_ref[...])
        r = DEEPNORM_ALPHA * x[rows, :] + _ada_slice(ada_ref, 5) * y
        out_ref[rows, :] = _layer_norm(r, lng_ref[...], lnb_ref[...])


def _mixer(x, ada, ws, wup, bup, gn, slopes, sinks, wa, wb, wo, ln_g, ln_b):
    tm = MIXER_TILE_M
    wq, wk, wv, wlr, wgo, wsq, wsk, wsv, wga, wgb = ws
    seg, mask, tri = _gla_constants()
    qk = GLA_HEADS * GLA_DK
    kvw = SWA_KV_HEADS * SWA_HEAD_DIM
    row = pl.BlockSpec((tm, D_MODEL), lambda i: (i, 0))
    smem = pl.BlockSpec(memory_space=pltpu.SMEM)

    def w(n):
        return _resident((D_MODEL, n))

    vec = _resident((1, D_MODEL))
    return pl.pallas_call(
        _mixer_kernel,
        grid=(SEQ // tm,),
        in_specs=[smem, smem, row, _resident((1, N_ADA * D_MODEL)),
                  w(qk), w(qk), w(D_MODEL), w(LR_PAD), w(D_MODEL), w(D_MODEL), w(kvw), w(kvw),
                  w(D_MODEL), w(D_MODEL),
                  _resident((LR_PAD, qk)), _resident((1, qk)),
                  _resident(seg.shape), _resident(mask.shape), _resident(tri.shape),
                  _resident((1, GLA_DV)),
                  w(D_MODEL), w(D_MODEL), w(D_MODEL), vec, vec],
        out_specs=row,
        out_shape=jax.ShapeDtypeStruct((SEQ, D_MODEL), F32),
        scratch_shapes=[pltpu.VMEM((tm, qk), F32), pltpu.VMEM((tm, qk), F32),
                        pltpu.VMEM((tm, D_MODEL), BF16), pltpu.VMEM((tm, qk), F32),
                        pltpu.VMEM((tm, D_MODEL), F32), pltpu.VMEM((tm, qk), F32),
                        pltpu.VMEM((tm, D_MODEL), BF16), pltpu.VMEM((tm, D_MODEL), BF16),
                        pltpu.VMEM((GLA_HEADS, GLA_DK, GLA_DV), F32),
                        pltpu.VMEM((GLA_HEADS, GLA_DK, GLA_DV), F32),
                        pltpu.VMEM((SWA_KV_HEADS, SWA_BLOCK, LANES), BF16),
                        pltpu.VMEM((SWA_KV_HEADS, SWA_BLOCK, LANES), BF16),
                        pltpu.VMEM((SWA_HEADS, SWA_BLOCK, SWA_BLOCK), F32)],
        compiler_params=_cparams(),
        name="mixer",
    )(slopes, sinks, x, ada, wq, wk, wv, wlr, wgo, wsq, wsk, wsv, wga, wgb, wup, bup,
      jnp.asarray(seg, BF16), jnp.asarray(mask, F32), jnp.asarray(tri, BF16), gn,
      wa, wb, wo, ln_g, ln_b)


def _alibi_slopes(n):
    return 2.0 ** (-8.0 * jnp.arange(1, n + 1, dtype=jnp.float32) / n)


def _split_w_in(w_in):
    offs = np.concatenate([[0], np.cumsum(MIX_SIZES)])
    cols = [w_in[:, offs[i]:offs[i + 1]].astype(BF16) for i in range(len(MIX_SIZES))]
    cols[3] = jnp.pad(cols[3], ((0, 0), (0, LR_PAD - GLA_GATE_RANK)))
    return cols


def kernel(x, c, w_ada, b_ada, ffn1_w_gate, ffn1_w_up, ffn1_w_down, ln1_g, ln1_b, w_in,
           w_gla_gate_up, b_gla_gate, gla_norm_g, w_branch_gla, swa_sinks, w_branch_swa, w_out,
           ln2_g, ln2_b, ffn2_w_gate, ffn2_w_up, ffn2_w_down, ln3_g, ln3_b):
    assert x.shape == (1, SEQ, D_MODEL) and w_ada.shape[0] == 1
    x2d = x.reshape(SEQ, D_MODEL)
    vec = lambda p: p.reshape(1, -1)

    ada = _ada(c, w_ada[0], b_ada[0])
    x1 = _ffn(x2d, ada, ffn1_w_gate[0].astype(BF16), ffn1_w_up[0].astype(BF16),
              ffn1_w_down[0].astype(BF16), vec(ln1_g), vec(ln1_b), 0)

    wup = jnp.pad(w_gla_gate_up[0], ((0, LR_PAD - GLA_GATE_RANK), (0, 0))).astype(BF16)
    x2 = _mixer(x1, ada, _split_w_in(w_in[0]), wup, vec(b_gla_gate), vec(gla_norm_g),
                _alibi_slopes(SWA_HEADS), swa_sinks[0],
                w_branch_gla[0].astype(BF16), w_branch_swa[0].astype(BF16), w_out[0].astype(BF16),
                vec(ln2_g), vec(ln2_b))

    out = _ffn(x2, ada, ffn2_w_gate[0].astype(BF16), ffn2_w_up[0].astype(BF16),
               ffn2_w_down[0].astype(BF16), vec(ln3_g), vec(ln3_b), 6)
    return out.reshape(1, SEQ, D_MODEL)
```

```python
import functools

import numpy as np
import jax
import jax.numpy as jnp
from jax import lax
from jax.experimental import pallas as pl
from jax.experimental.pallas import tpu as pltpu

D_MODEL = 1024
SEQ = 16384
D_FF = 2816
N_ADA = 9
LN_EPS = 1e-5
RMS_EPS = 1e-6
NEG_INF = -1e30
DEEPNORM_ALPHA = 2.0 ** 0.25
MACARON_WEIGHT = 0.5

GLA_HEADS = 4
GLA_DK = 128
GLA_DV = 256
GLA_GATE_RANK = 16
GLA_GATE_NORMALIZER = 16.0
GLA_CHUNK = 64
GLA_LEVELS = (32, 16, 8, 4, 2, 1)

SWA_HEADS = 16
SWA_KV_HEADS = 4
SWA_HEAD_DIM = 64
SWA_BLOCK = 128

MIX_SIZES = (512, 512, 1024, 16, 1024, 1024, 256, 256, 1024, 1024)

LANES = 128
V7X_VMEM_LIMIT_BYTES = 60 * 1024 * 1024

BF16 = jnp.bfloat16
F32 = jnp.float32


def _cparams():
    return pltpu.CompilerParams(dimension_semantics=("arbitrary",),
                                vmem_limit_bytes=V7X_VMEM_LIMIT_BYTES)


def _resident(shape):
    return pl.BlockSpec(shape, lambda i: (0,) * len(shape), pipeline_mode=pl.Buffered(1))


def _dot(a, b):
    return jnp.dot(a, b, preferred_element_type=F32)


def _dot_t(a, b):
    return lax.dot_general(a, b, (((1,), (1,)), ((), ())), preferred_element_type=F32)


def _sigmoid(x):
    return 1.0 / (1.0 + jnp.exp(-x))


def _layer_norm(r, g, b):
    mu = jnp.mean(r, axis=-1, keepdims=True)
    d = r - mu
    var = jnp.mean(d * d, axis=-1, keepdims=True)
    return d * lax.rsqrt(var + LN_EPS) * g + b


def _ada_slice(ada_ref, idx):
    return ada_ref[:, idx * D_MODEL:(idx + 1) * D_MODEL]


ADA_TILE_N = 1536


def _ada_kernel(c_ref, w_ref, b_ref, o_ref):
    c = c_ref[...]
    s = (c * _sigmoid(c)).astype(BF16)
    s8 = jnp.broadcast_to(s, (8, D_MODEL))
    y = _dot(s8, w_ref[...].astype(BF16))
    o_ref[...] = y[0:1, :] + b_ref[...]


def _ada(c, w_ada, b_ada):
    n = N_ADA * D_MODEL
    return pl.pallas_call(
        _ada_kernel,
        grid=(n // ADA_TILE_N,),
        in_specs=[pl.BlockSpec((1, D_MODEL), lambda i: (0, 0)),
                  pl.BlockSpec((D_MODEL, ADA_TILE_N), lambda i: (0, i)),
                  pl.BlockSpec((1, ADA_TILE_N), lambda i: (0, i))],
        out_specs=pl.BlockSpec((1, ADA_TILE_N), lambda i: (0, i)),
        out_shape=jax.ShapeDtypeStruct((1, n), F32),
        compiler_params=_cparams(),
        name="ada",
    )(c, w_ada, b_ada.reshape(1, n))


FFN_TILE_M = 512
FFN_SPLIT = 1536


def _ffn_kernel(x_ref, ada_ref, wg_ref, wu_ref, wd_ref, lng_ref, lnb_ref, o_ref, *, ada_base):
    x = x_ref[...]
    sh = _ada_slice(ada_ref, ada_base)
    sc = _ada_slice(ada_ref, ada_base + 1)
    gt = _ada_slice(ada_ref, ada_base + 2)
    h = (x * (1.0 + sc) + sh).astype(BF16)
    y = None
    for lo, hi in ((0, FFN_SPLIT), (FFN_SPLIT, D_FF)):
        g = _dot(h, wg_ref[:, lo:hi])
        u = _dot(h, wu_ref[:, lo:hi])
        a = (g * _sigmoid(g) * u).astype(BF16)
        part = _dot(a, wd_ref[lo:hi, :])
        y = part if y is None else y + part
    r = DEEPNORM_ALPHA * x + (MACARON_WEIGHT * gt) * y
    o_ref[...] = _layer_norm(r, lng_ref[...], lnb_ref[...])


def _ffn(x, ada, wg, wu, wd, ln_g, ln_b, ada_base):
    tm = FFN_TILE_M
    row = pl.BlockSpec((tm, D_MODEL), lambda i: (i, 0))
    return pl.pallas_call(
        functools.partial(_ffn_kernel, ada_base=ada_base),
        grid=(SEQ // tm,),
        in_specs=[row, _resident((1, N_ADA * D_MODEL)),
                  _resident((D_MODEL, D_FF)), _resident((D_MODEL, D_FF)), _resident((D_FF, D_MODEL)),
                  _resident((1, D_MODEL)), _resident((1, D_MODEL))],
        out_specs=row,
        out_shape=jax.ShapeDtypeStruct((SEQ, D_MODEL), F32),
        compiler_params=_cparams(),
        name="ffn",
    )(x, ada, wg, wu, wd, ln_g, ln_b)


MIXER_TILE_M = 512
MIXER_DOT_COLS = 512
LR_PAD = LANES
GLA_NSEG = 1 + len(GLA_LEVELS)
GLA_SPLIT = 2
GLA_FAST_CHUNK = 128
GLA_SAFE_LOG_DECAY = -60.0


def _log_sigmoid(z):
    return jnp.minimum(z, 0.0) - jnp.log(1.0 + jnp.exp(-jnp.abs(z)))


def _gla_constants():
    c = GLA_CHUNK
    t = np.arange(c)[:, None]
    u = np.arange(c)[None, :]
    seg = [(u <= t)]
    mask = [(u == t)]
    for m in GLA_LEVELS:
        blk_t, blk_u = t // m, u // m
        odd = (blk_t % 2) == 1
        p_odd = blk_t * m
        p_even = (blk_t + 1) * m
        seg.append(np.where(odd, (u > p_odd) & (u <= t), (u > t) & (u <= p_even)))
        mask.append(odd & (blk_u == blk_t - 1))
    seg = np.concatenate(seg, axis=0).astype(np.float32)
    seg = np.concatenate([seg] * GLA_SPLIT, axis=1)
    mask = np.stack(mask).astype(np.float32)
    fc = GLA_FAST_CHUNK
    tri = np.tril(np.ones((fc, fc), np.float32))
    tri = np.concatenate([tri] * GLA_SPLIT, axis=1)
    return seg, mask, tri


def _split_terms(g):
    parts = []
    rem = g
    for _ in range(GLA_SPLIT):
        p = rem.astype(BF16)
        parts.append(p)
        rem = rem - p.astype(F32)
    return jnp.concatenate(parts, axis=0)


def _decay_column(bl):
    return jnp.exp(jnp.broadcast_to(bl, (8, GLA_DK))).T[:, 0:1]


def _gla_finish(o, gn_ref, go, o_ref, rows, vs):
    ms = jnp.mean(o * o, axis=-1, keepdims=True)
    on = o * lax.rsqrt(ms + RMS_EPS) * gn_ref[...]
    o_ref[rows, vs] = (on * go).astype(BF16)


def _gla_cumsum(la_ref, tri_ref, b_ref, result):
    c = GLA_FAST_CHUNK
    lowest = None
    for ci in range(MIXER_TILE_M // c):
        rows = slice(ci * c, (ci + 1) * c)
        b = _dot(tri_ref[...], _split_terms(la_ref[rows, :]))
        b_ref[rows, :] = b
        total = jnp.min(b[c - 1:c])
        lowest = total if lowest is None else jnp.minimum(lowest, total)
        result["safe"] = lowest >= GLA_SAFE_LOG_DECAY
        yield


def _gla_tile_fast(q_ref, k_ref, v_ref, b_ref, go_ref, gn_ref, o_ref, state_in_ref, state_out_ref):
    c = GLA_FAST_CHUNK
    ti = lax.broadcasted_iota(jnp.int32, (c, c), 0)
    si = lax.broadcasted_iota(jnp.int32, (c, c), 1)
    causal = si <= ti
    chunks = range(MIXER_TILE_M // c)
    heads = range(GLA_HEADS)
    rows = [slice(ci * c, (ci + 1) * c) for ci in chunks]
    kcols = [slice(hd * GLA_DK, (hd + 1) * GLA_DK) for hd in heads]
    vcols = [slice(hd * GLA_DV, (hd + 1) * GLA_DV) for hd in heads]

    q_in, a, upd, d_col = {}, {}, {}, {}
    for ci in chunks:
        b = b_ref[rows[ci], :]
        b_last = b[c - 1:c]
        e_pos = jnp.exp(b)
        e_neg = jnp.exp(-b)
        e_last = jnp.exp(b_last)
        for hd in heads:
            ks = kcols[hd]
            q_in[ci, hd] = (q_ref[rows[ci], ks] * e_pos[:, ks]).astype(BF16)
            k_out = k_ref[rows[ci], ks] * e_neg[:, ks]
            a[ci, hd] = jnp.where(causal, _dot_t(q_in[ci, hd], k_out.astype(BF16)), 0.0).astype(BF16)
            k_dec_t = (k_out * e_last[:, ks]).T.astype(BF16)
            upd[ci, hd] = _dot(k_dec_t, v_ref[rows[ci], vcols[hd]])
            d_col[ci, hd] = _decay_column(b_last[:, ks])
        yield

    for hd in heads:
        state = state_in_ref[hd]
        for ci in chunks:
            v = v_ref[rows[ci], vcols[hd]]
            o = _dot(jnp.concatenate([q_in[ci, hd], a[ci, hd]], axis=1),
                     jnp.concatenate([state.astype(BF16), v], axis=0))
            state = d_col[ci, hd] * state + upd[ci, hd]
            _gla_finish(o, gn_ref, go_ref[rows[ci], vcols[hd]], o_ref, rows[ci], vcols[hd])
        state_out_ref[hd] = state
        yield


def _gla_tile_any_decay(q_ref, k_ref, v_ref, la_ref, go_ref, seg_ref, mask_ref, gn_ref, o_ref,
                        state_ref):
    c = GLA_CHUNK

    def chunk(ci, carry):
        rows = pl.ds(pl.multiple_of(ci * c, c), c)
        e_all = _dot(seg_ref[...], _split_terms(la_ref[rows, :]))
        b = e_all[0:c]
        b_last = b[c - 1:c]
        for hd in range(GLA_HEADS):
            ks = slice(hd * GLA_DK, (hd + 1) * GLA_DK)
            vs = slice(hd * GLA_DV, (hd + 1) * GLA_DV)
            q = q_ref[rows, ks]
            k = k_ref[rows, ks]
            v = v_ref[rows, vs]
            bh = b[:, ks]
            bl = b_last[:, ks]
            state = state_ref[hd]
            o = _dot((q * jnp.exp(bh)).astype(BF16), state.astype(BF16))
            a = _dot_t(q.astype(BF16), k.astype(BF16)) * mask_ref[0]
            for li in range(len(GLA_LEVELS)):
                e = jnp.exp(e_all[(li + 1) * c:(li + 2) * c, ks])
                a = a + _dot_t((q * e).astype(BF16), (k * e).astype(BF16)) * mask_ref[li + 1]
            o = o + _dot(a.astype(BF16), v)
            k_dec = k * jnp.exp(bl - bh)
            state_ref[hd] = _decay_column(bl) * state + _dot(k_dec.T.astype(BF16), v)
            _gla_finish(o, gn_ref, go_ref[rows, vs], o_ref, rows, vs)
        return carry

    lax.fori_loop(0, MIXER_TILE_M // c, chunk, 0)


LOG2E = 1.4426950408889634


def _swa_bias_table(slope_ref, bias_ref):
    blk = SWA_BLOCK
    qi = lax.broadcasted_iota(jnp.int32, (blk, blk), 0)
    kj = lax.broadcasted_iota(jnp.int32, (blk, blk), 1)
    dist = jnp.where(kj <= qi, qi - kj, qi + blk - kj).astype(F32)
    for head in range(SWA_HEADS):
        bias_ref[head] = (slope_ref[head] * LOG2E) * dist


def _swa_tile(sink_ref, q, k, v, kprev_ref, vprev_ref, bias_ref, first_tile, o_ref):
    blk = SWA_BLOCK
    hd = SWA_HEAD_DIM
    group = SWA_HEADS // SWA_KV_HEADS
    kvs = range(SWA_KV_HEADS)

    qi = lax.broadcasted_iota(jnp.int32, (blk, blk), 0)
    kj = lax.broadcasted_iota(jnp.int32, (blk, blk), 1)
    own = kj <= qi
    first_valid = (kj - qi) <= jnp.where(first_tile, 0, blk)
    lane = lax.broadcasted_iota(jnp.int32, (blk, LANES), 1)
    low = lane < hd

    def dup(x, kv):
        tile = x[:, (kv // 2) * LANES:(kv // 2 + 1) * LANES]
        rolled = pltpu.roll(tile, hd, 1)
        lo_half, hi_half = (tile, rolled) if kv % 2 == 0 else (rolled, tile)
        return jnp.where(low, lo_half, hi_half).astype(BF16)

    def dup_all(x):
        xf = x.astype(F32)
        return [dup(xf, kv) for kv in kvs]

    zero = jnp.zeros((blk, LANES), BF16)
    prev_k = [kprev_ref[kv] for kv in kvs]
    prev_v = [vprev_ref[kv] for kv in kvs]
    for bi in range(MIXER_TILE_M // blk):
        rows = slice(bi * blk, (bi + 1) * blk)
        cur_k = dup_all(k[rows, :])
        cur_v = dup_all(v[rows, :])
        for kv in kvs:
            lhs = []
            for pair in (2 * kv, 2 * kv + 1):
                x = q[rows, pair * LANES:(pair + 1) * LANES]
                lhs += [jnp.where(low, x, zero), jnp.where(low, zero, x)]
            keys = jnp.concatenate([prev_k[kv], cur_k[kv]], axis=0)
            s_all = _dot_t(jnp.concatenate(lhs, axis=0), keys)
            probs = []
            for gi in range(group):
                head = kv * group + gi
                sc = s_all[gi * blk:(gi + 1) * blk]
                s = jnp.where(own, sc[:, blk:], sc[:, :blk]) * (hd ** -0.5 * LOG2E) - bias_ref[head]
                if bi == 0:
                    s = jnp.where(first_valid, s, NEG_INF)
                sink = sink_ref[head] * LOG2E
                m = jnp.maximum(jnp.max(s, axis=-1, keepdims=True), sink)
                p = jnp.exp2(s - m)
                denom = jnp.sum(p, axis=-1, keepdims=True) + jnp.exp2(sink - m)
                pn = p * (1.0 / denom)
                probs.append(jnp.concatenate([jnp.where(own, 0.0, pn), jnp.where(own, pn, 0.0)],
                                             axis=1).astype(BF16))
            vals = jnp.concatenate([prev_v[kv], cur_v[kv]], axis=0)
            r = _dot(jnp.concatenate(probs, axis=0), vals)
            for pi, pair in enumerate((2 * kv, 2 * kv + 1)):
                r0 = r[(2 * pi) * blk:(2 * pi + 1) * blk]
                r1 = r[(2 * pi + 1) * blk:(2 * pi + 2) * blk]
                o_ref[rows, pair * LANES:(pair + 1) * LANES] = jnp.where(low, r0, r1).astype(BF16)
            yield
        prev_k, prev_v = cur_k, cur_v
    for kv in kvs:
        kprev_ref[kv] = prev_k[kv]
        vprev_ref[kv] = prev_v[kv]


def _column_views(w_ref, widths):
    views, lo = [], 0
    for width in widths:
        views.append(w_ref.at[:, lo:lo + width])
        lo += width
    return views


def _interleave(first, second, ratio):
    live_first, live_second = True, True
    while live_first or live_second:
        for _ in range(ratio):
            if live_first:
                live_first = next(first, "done") != "done"
        if live_second:
            live_second = next(second, "done") != "done"


def _mixer_kernel(slope_ref, sink_ref, x_ref, ada_ref,
                  w_gla_ref, wlr_ref, w_rest_ref,
                  wup_ref, bup_ref, seg_ref, mask_ref, tri_ref, gn_ref,
                  wa_ref, wb_ref, wo_ref, lng_ref, lnb_ref,
                  next_wg_ref, next_wu_ref, next_wd_ref,
                  out_ref, next_wg_out, next_wu_out, next_wd_out,
                  q_s, k_s, v_s, la_s, go_s, b_s, o_s, a_s, state_ref, state_new, kprev_s, vprev_s,
                  bias_ref):
    first_tile = pl.program_id(0) == 0
    for src, dst in ((next_wg_ref, next_wg_out), (next_wu_ref, next_wu_out), (next_wd_ref, next_wd_out)):
        dst[...] = src[...].astype(BF16)
    wq_ref, wk_ref, wv_ref = _column_views(w_gla_ref, MIX_SIZES[0:3])
    wgo_ref, wsq_ref, wsk_ref, wsv_ref, wga_ref, wgb_ref = _column_views(w_rest_ref, MIX_SIZES[4:10])

    @pl.when(first_tile)
    def _():
        state_ref[...] = jnp.zeros_like(state_ref)
        kprev_s[...] = jnp.zeros_like(kprev_s)
        vprev_s[...] = jnp.zeros_like(vprev_s)
        _swa_bias_table(slope_ref, bias_ref)

    x = x_ref[...]
    h = (x * (1.0 + _ada_slice(ada_ref, 4)) + _ada_slice(ada_ref, 3)).astype(BF16)
    step = MIXER_DOT_COLS
    col_blocks = range(0, D_MODEL, step)

    sq = _dot(h, wsq_ref[...]).astype(BF16)
    sk = _dot(h, wsk_ref[...]).astype(BF16)
    sv = _dot(h, wsv_ref[...]).astype(BF16)

    gates = {}
    guard = {}

    def gate(name, w_ref, lo):
        gates[name, lo] = _sigmoid(_dot(h, w_ref[:, lo:lo + step]))

    def projections_1():
        lr = _dot(h, wlr_ref[...]).astype(BF16)
        z = _dot(lr, wup_ref[...]) + bup_ref[...]
        la_s[...] = _log_sigmoid(z) * (1.0 / GLA_GATE_NORMALIZER)
        yield
        cumsum = _gla_cumsum(la_s, tri_ref, b_s, guard)
        for lo in range(0, GLA_HEADS * GLA_DK, step):
            q_s[:, lo:lo + step] = _dot(h, wq_ref[:, lo:lo + step]) * (GLA_DK ** -0.5)
            next(cumsum)
            yield
            k_s[:, lo:lo + step] = _dot(h, wk_ref[:, lo:lo + step])
            next(cumsum)
            yield
        for lo in col_blocks:
            v_s[:, lo:lo + step] = _dot(h, wv_ref[:, lo:lo + step]).astype(BF16)
            next(cumsum, None)
            yield
        for _ in cumsum:
            pass
        for lo in col_blocks:
            go = _dot(h, wgo_ref[:, lo:lo + step])
            go_s[:, lo:lo + step] = go * _sigmoid(go)
            yield
        for lo in col_blocks:
            gate("a", wga_ref, lo)
            yield

    _interleave(_swa_tile(sink_ref, sq, sk, sv, kprev_s, vprev_s, bias_ref, first_tile, a_s),
                projections_1(), ratio=2)

    def projections_2():
        for lo in col_blocks:
            gate("b", wgb_ref, lo)
            yield
        for lo in col_blocks:
            gates["yb", lo] = _dot(a_s[...], wb_ref[:, lo:lo + step])
            yield

    _interleave(_gla_tile_fast(q_s, k_s, v_s, b_s, go_s, gn_ref, o_s, state_ref, state_new),
                projections_2(), ratio=2)
    safe = guard["safe"]

    @pl.when(jnp.logical_not(safe))
    def _():
        state_new[...] = state_ref[...]
        _gla_tile_any_decay(q_s, k_s, v_s, la_s, go_s, seg_ref, mask_ref, gn_ref, o_s, state_new)

    state_ref[...] = state_new[...]

    gate_a = jnp.concatenate([gates["a", lo] for lo in col_blocks], axis=1)
    gate_b_yb = jnp.concatenate([gates["b", lo] * gates["yb", lo] for lo in col_blocks], axis=1)
    rows_half = MIXER_TILE_M // 2
    for lo in (0, rows_half):
        rows = slice(lo, lo + rows_half)
        ya = _dot(o_s[rows, :], wa_ref[...])
        merged = (gate_a[rows, :] * ya + gate_b_yb[rows, :]).astype(BF16)
        y = _dot(merged, wo_ref[...])
        r = DEEPNORM_ALPHA * x[rows, :] + _ada_slice(ada_ref, 5) * y
        out_ref[rows, :] = _layer_norm(r, lng_ref[...], lnb_ref[...])


def _mixer(x, ada, ws, wup, bup, gn, slopes, sinks, wa, wb, wo, ln_g, ln_b, next_ffn):
    tm = MIXER_TILE_M
    steps = SEQ // tm
    w_gla, wlr, w_rest = ws
    seg, mask, tri = _gla_constants()
    qk = GLA_HEADS * GLA_DK
    row = pl.BlockSpec((tm, D_MODEL), lambda i: (i, 0))
    smem = pl.BlockSpec(memory_space=pltpu.SMEM)
    up_rows = pl.BlockSpec((D_MODEL // steps, D_FF), lambda i: (i, 0))
    down_rows = pl.BlockSpec((2 * D_FF // steps, D_MODEL), lambda i: (i // 2, 0))

    def w(n):
        return _resident((D_MODEL, n))

    vec = _resident((1, D_MODEL))
    return pl.pallas_call(
        _mixer_kernel,
        grid=(steps,),
        in_specs=[smem, smem, row, _resident((1, N_ADA * D_MODEL)),
                  w(sum(MIX_SIZES[0:3])), w(LR_PAD), w(sum(MIX_SIZES[4:10])),
                  _resident((LR_PAD, qk)), _resident((1, qk)),
                  _resident(seg.shape), _resident(mask.shape), _resident(tri.shape),
                  _resident((1, GLA_DV)),
                  w(D_MODEL), w(D_MODEL), w(D_MODEL), vec, vec,
                  up_rows, up_rows, down_rows],
        out_specs=[row, up_rows, up_rows, down_rows],
        out_shape=[jax.ShapeDtypeStruct((SEQ, D_MODEL), F32),
                   jax.ShapeDtypeStruct((D_MODEL, D_FF), BF16),
                   jax.ShapeDtypeStruct((D_MODEL, D_FF), BF16),
                   jax.ShapeDtypeStruct((D_FF, D_MODEL), BF16)],
        scratch_shapes=[pltpu.VMEM((tm, qk), F32), pltpu.VMEM((tm, qk), F32),
                        pltpu.VMEM((tm, D_MODEL), BF16), pltpu.VMEM((tm, qk), F32),
                        pltpu.VMEM((tm, D_MODEL), F32), pltpu.VMEM((tm, qk), F32),
                        pltpu.VMEM((tm, D_MODEL), BF16), pltpu.VMEM((tm, D_MODEL), BF16),
                        pltpu.VMEM((GLA_HEADS, GLA_DK, GLA_DV), F32),
                        pltpu.VMEM((GLA_HEADS, GLA_DK, GLA_DV), F32),
                        pltpu.VMEM((SWA_KV_HEADS, SWA_BLOCK, LANES), BF16),
                        pltpu.VMEM((SWA_KV_HEADS, SWA_BLOCK, LANES), BF16),
                        pltpu.VMEM((SWA_HEADS, SWA_BLOCK, SWA_BLOCK), F32)],
        compiler_params=_cparams(),
        name="mixer",
    )(slopes, sinks, x, ada, w_gla, wlr, w_rest, wup, bup,
      jnp.asarray(seg, BF16), jnp.asarray(mask, F32), jnp.asarray(tri, BF16), gn,
      wa, wb, wo, ln_g, ln_b, *next_ffn)


def _alibi_slopes(n):
    return 2.0 ** (-8.0 * jnp.arange(1, n + 1, dtype=jnp.float32) / n)


def _split_w_in(w_in):
    lo = sum(MIX_SIZES[0:3])
    hi = lo + GLA_GATE_RANK
    wlr = jnp.pad(w_in[:, lo:hi], ((0, 0), (0, LR_PAD - GLA_GATE_RANK)))
    return w_in[:, :lo].astype(BF16), wlr.astype(BF16), w_in[:, hi:].astype(BF16)


def kernel(x, c, w_ada, b_ada, ffn1_w_gate, ffn1_w_up, ffn1_w_down, ln1_g, ln1_b, w_in,
           w_gla_gate_up, b_gla_gate, gla_norm_g, w_branch_gla, swa_sinks, w_branch_swa, w_out,
           ln2_g, ln2_b, ffn2_w_gate, ffn2_w_up, ffn2_w_down, ln3_g, ln3_b):
    assert x.shape == (1, SEQ, D_MODEL) and w_ada.shape[0] == 1
    x2d = x.reshape(SEQ, D_MODEL)
    vec = lambda p: p.reshape(1, -1)

    ada = _ada(c, w_ada[0], b_ada[0])
    x1 = _ffn(x2d, ada, ffn1_w_gate[0].astype(BF16), ffn1_w_up[0].astype(BF16),
              ffn1_w_down[0].astype(BF16), vec(ln1_g), vec(ln1_b), 0)

    wup = jnp.pad(w_gla_gate_up[0], ((0, LR_PAD - GLA_GATE_RANK), (0, 0))).astype(BF16)
    x2, wg2, wu2, wd2 = _mixer(
        x1, ada, _split_w_in(w_in[0]), wup, vec(b_gla_gate), vec(gla_norm_g),
        _alibi_slopes(SWA_HEADS), swa_sinks[0],
        w_branch_gla[0].astype(BF16), w_branch_swa[0].astype(BF16), w_out[0].astype(BF16),
        vec(ln2_g), vec(ln2_b), (ffn2_w_gate[0], ffn2_w_up[0], ffn2_w_down[0]))

    out = _ffn(x2, ada, wg2, wu2, wd2, vec(ln3_g), vec(ln3_b), 6)
    return out.reshape(1, SEQ, D_MODEL)
```

```python
import functools

import numpy as np
import jax
import jax.numpy as jnp
from jax import lax
from jax.experimental import pallas as pl
from jax.experimental.pallas import tpu as pltpu

D_MODEL = 1024
SEQ = 16384
D_FF = 2816
N_ADA = 9
LN_EPS = 1e-5
RMS_EPS = 1e-6
NEG_INF = -1e30
DEEPNORM_ALPHA = 2.0 ** 0.25
MACARON_WEIGHT = 0.5

GLA_HEADS = 4
GLA_DK = 128
GLA_DV = 256
GLA_GATE_RANK = 16
GLA_GATE_NORMALIZER = 16.0
GLA_CHUNK = 64
GLA_LEVELS = (32, 16, 8, 4, 2, 1)

SWA_HEADS = 16
SWA_KV_HEADS = 4
SWA_HEAD_DIM = 64
SWA_BLOCK = 128

MIX_SIZES = (512, 512, 1024, 16, 1024, 1024, 256, 256, 1024, 1024)

LANES = 128
V7X_VMEM_LIMIT_BYTES = 60 * 1024 * 1024

BF16 = jnp.bfloat16
F32 = jnp.float32


def _cparams():
    return pltpu.CompilerParams(dimension_semantics=("arbitrary",),
                                vmem_limit_bytes=V7X_VMEM_LIMIT_BYTES)


def _resident(shape):
    return pl.BlockSpec(shape, lambda i: (0,) * len(shape), pipeline_mode=pl.Buffered(1))


def _dot(a, b):
    return jnp.dot(a, b, preferred_element_type=F32)


def _dot_t(a, b):
    return lax.dot_general(a, b, (((1,), (1,)), ((), ())), preferred_element_type=F32)


def _sigmoid(x):
    return 1.0 / (1.0 + jnp.exp(-x))


def _layer_norm(r, g, b):
    mu = jnp.mean(r, axis=-1, keepdims=True)
    d = r - mu
    var = jnp.mean(d * d, axis=-1, keepdims=True)
    return d * lax.rsqrt(var + LN_EPS) * g + b


def _ada_slice(ada_ref, idx):
    return ada_ref[:, idx * D_MODEL:(idx + 1) * D_MODEL]


ADA_TILE_N = 1536


def _ada_kernel(c_ref, w_ref, b_ref, o_ref):
    c = c_ref[...]
    s = (c * _sigmoid(c)).astype(BF16)
    s8 = jnp.broadcast_to(s, (8, D_MODEL))
    y = _dot(s8, w_ref[...].astype(BF16))
    o_ref[...] = y[0:1, :] + b_ref[...]


def _ada(c, w_ada, b_ada):
    n = N_ADA * D_MODEL
    return pl.pallas_call(
        _ada_kernel,
        grid=(n // ADA_TILE_N,),
        in_specs=[pl.BlockSpec((1, D_MODEL), lambda i: (0, 0)),
                  pl.BlockSpec((D_MODEL, ADA_TILE_N), lambda i: (0, i)),
                  pl.BlockSpec((1, ADA_TILE_N), lambda i: (0, i))],
        out_specs=pl.BlockSpec((1, ADA_TILE_N), lambda i: (0, i)),
        out_shape=jax.ShapeDtypeStruct((1, n), F32),
        compiler_params=_cparams(),
        name="ada",
    )(c, w_ada, b_ada.reshape(1, n))


FFN_TILE_M = 512
FFN_SPLIT = 1536


def _cast_mixer_weight_rows(w_in_ref, wa_ref, wb_ref, wo_ref, w_gla_out, wlr_out, w_rest_out,
                            wa_out, wb_out, wo_out):
    w = w_in_ref[...]
    lo = sum(MIX_SIZES[0:3])
    hi = lo + GLA_GATE_RANK
    w_gla_out[...] = w[:, :lo].astype(BF16)
    lane = lax.broadcasted_iota(jnp.int32, (w.shape[0], LANES), 1)
    wlr_out[...] = jnp.where(lane < GLA_GATE_RANK, w[:, lo:lo + LANES], 0.0).astype(BF16)
    w_rest_out[...] = w[:, hi:].astype(BF16)
    for src, dst in ((wa_ref, wa_out), (wb_ref, wb_out), (wo_ref, wo_out)):
        dst[...] = src[...].astype(BF16)


def _ffn_kernel(x_ref, ada_ref, wg_ref, wu_ref, wd_ref, lng_ref, lnb_ref, *rest, ada_base):
    o_ref = rest[-1] if len(rest) == 1 else rest[4]
    if len(rest) > 1:
        _cast_mixer_weight_rows(*rest[0:4], *rest[5:])
    x = x_ref[...]
    sh = _ada_slice(ada_ref, ada_base)
    sc = _ada_slice(ada_ref, ada_base + 1)
    gt = _ada_slice(ada_ref, ada_base + 2)
    h = (x * (1.0 + sc) + sh).astype(BF16)
    y = None
    for lo, hi in ((0, FFN_SPLIT), (FFN_SPLIT, D_FF)):
        g = _dot(h, wg_ref[:, lo:hi])
        u = _dot(h, wu_ref[:, lo:hi])
        a = (g * _sigmoid(g) * u).astype(BF16)
        part = _dot(a, wd_ref[lo:hi, :])
        y = part if y is None else y + part
    r = DEEPNORM_ALPHA * x + (MACARON_WEIGHT * gt) * y
    o_ref[...] = _layer_norm(r, lng_ref[...], lnb_ref[...])


def _ffn(x, ada, wg, wu, wd, ln_g, ln_b, ada_base, mixer_weights=None):
    tm = FFN_TILE_M
    steps = SEQ // tm
    row = pl.BlockSpec((tm, D_MODEL), lambda i: (i, 0))
    in_specs = [row, _resident((1, N_ADA * D_MODEL)),
                _resident((D_MODEL, D_FF)), _resident((D_MODEL, D_FF)), _resident((D_FF, D_MODEL)),
                _resident((1, D_MODEL)), _resident((1, D_MODEL))]
    out_specs = [row]
    out_shape = [jax.ShapeDtypeStruct((SEQ, D_MODEL), F32)]
    operands = [x, ada, wg, wu, wd, ln_g, ln_b]
    if mixer_weights is not None:
        rows = D_MODEL // steps
        blocks = lambda n: pl.BlockSpec((rows, n), lambda i: (i, 0))
        n_gla = sum(MIX_SIZES[0:3])
        n_rest = sum(MIX_SIZES[4:10])
        in_specs += [blocks(sum(MIX_SIZES)), blocks(D_MODEL), blocks(D_MODEL), blocks(D_MODEL)]
        operands += list(mixer_weights)
        for n in (n_gla, LR_PAD, n_rest, D_MODEL, D_MODEL, D_MODEL):
            out_specs.append(blocks(n))
            out_shape.append(jax.ShapeDtypeStruct((D_MODEL, n), BF16))
    outs = pl.pallas_call(
        functools.partial(_ffn_kernel, ada_base=ada_base),
        grid=(steps,),
        in_specs=in_specs,
        out_specs=out_specs,
        out_shape=out_shape,
        compiler_params=_cparams(),
        name="ffn",
    )(*operands)
    return outs[0] if mixer_weights is None else outs


MIXER_TILE_M = 512
MIXER_DOT_COLS = 512
LR_PAD = LANES
GLA_NSEG = 1 + len(GLA_LEVELS)
GLA_SPLIT = 2
GLA_FAST_CHUNK = 128
GLA_SAFE_LOG_DECAY = -60.0


def _log_sigmoid(z):
    return jnp.minimum(z, 0.0) - jnp.log(1.0 + jnp.exp(-jnp.abs(z)))


def _gla_constants():
    c = GLA_CHUNK
    t = np.arange(c)[:, None]
    u = np.arange(c)[None, :]
    seg = [(u <= t)]
    mask = [(u == t)]
    for m in GLA_LEVELS:
        blk_t, blk_u = t // m, u // m
        odd = (blk_t % 2) == 1
        p_odd = blk_t * m
        p_even = (blk_t + 1) * m
        seg.append(np.where(odd, (u > p_odd) & (u <= t), (u > t) & (u <= p_even)))
        mask.append(odd & (blk_u == blk_t - 1))
    seg = np.concatenate(seg, axis=0).astype(np.float32)
    seg = np.concatenate([seg] * GLA_SPLIT, axis=1)
    mask = np.stack(mask).astype(np.float32)
    fc = GLA_FAST_CHUNK
    tri = np.tril(np.ones((fc, fc), np.float32))
    tri = np.concatenate([tri] * GLA_SPLIT, axis=1)
    return seg, mask, tri


def _split_terms(g):
    parts = []
    rem = g
    for _ in range(GLA_SPLIT):
        p = rem.astype(BF16)
        parts.append(p)
        rem = rem - p.astype(F32)
    return jnp.concatenate(parts, axis=0)


def _decay_column(bl):
    return jnp.exp(jnp.broadcast_to(bl, (8, GLA_DK))).T[:, 0:1]


def _gla_finish(o, gn_ref, go, o_ref, rows, vs):
    ms = jnp.mean(o * o, axis=-1, keepdims=True)
    on = o * lax.rsqrt(ms + RMS_EPS) * gn_ref[...]
    o_ref[rows, vs] = (on * go).astype(BF16)


def _gla_cumsum(la_ref, tri_ref, b_ref, result):
    c = GLA_FAST_CHUNK
    lowest = None
    for ci in range(MIXER_TILE_M // c):
        rows = slice(ci * c, (ci + 1) * c)
        b = _dot(tri_ref[...], _split_terms(la_ref[rows, :]))
        b_ref[rows, :] = b
        total = jnp.min(b[c - 1:c])
        lowest = total if lowest is None else jnp.minimum(lowest, total)
        result["safe"] = lowest >= GLA_SAFE_LOG_DECAY
        yield


def _gla_tile_fast(q_ref, k_ref, v_ref, b_ref, go_ref, gn_ref, o_ref, state_in_ref, state_out_ref):
    c = GLA_FAST_CHUNK
    ti = lax.broadcasted_iota(jnp.int32, (c, c), 0)
    si = lax.broadcasted_iota(jnp.int32, (c, c), 1)
    causal = si <= ti
    chunks = range(MIXER_TILE_M // c)
    heads = range(GLA_HEADS)
    rows = [slice(ci * c, (ci + 1) * c) for ci in chunks]
    kcols = [slice(hd * GLA_DK, (hd + 1) * GLA_DK) for hd in heads]
    vcols = [slice(hd * GLA_DV, (hd + 1) * GLA_DV) for hd in heads]

    q_in, a, upd, d_col = {}, {}, {}, {}
    for ci in chunks:
        b = b_ref[rows[ci], :]
        b_last = b[c - 1:c]
        e_pos = jnp.exp(b)
        e_neg = jnp.exp(-b)
        e_last = jnp.exp(b_last)
        for hd in heads:
            ks = kcols[hd]
            q_in[ci, hd] = (q_ref[rows[ci], ks] * e_pos[:, ks]).astype(BF16)
            k_out = k_ref[rows[ci], ks] * e_neg[:, ks]
            a[ci, hd] = jnp.where(causal, _dot_t(q_in[ci, hd], k_out.astype(BF16)), 0.0).astype(BF16)
            k_dec_t = (k_out * e_last[:, ks]).T.astype(BF16)
            upd[ci, hd] = _dot(k_dec_t, v_ref[rows[ci], vcols[hd]])
            d_col[ci, hd] = _decay_column(b_last[:, ks])
        yield

    for hd in heads:
        state = state_in_ref[hd]
        for ci in chunks:
            v = v_ref[rows[ci], vcols[hd]]
            o = _dot(jnp.concatenate([q_in[ci, hd], a[ci, hd]], axis=1),
                     jnp.concatenate([state.astype(BF16), v], axis=0))
            state = d_col[ci, hd] * state + upd[ci, hd]
            _gla_finish(o, gn_ref, go_ref[rows[ci], vcols[hd]], o_ref, rows[ci], vcols[hd])
        state_out_ref[hd] = state
        yield


def _gla_tile_any_decay(q_ref, k_ref, v_ref, la_ref, go_ref, seg_ref, mask_ref, gn_ref, o_ref,
                        state_ref):
    c = GLA_CHUNK

    def chunk(ci, carry):
        rows = pl.ds(pl.multiple_of(ci * c, c), c)
        e_all = _dot(seg_ref[...], _split_terms(la_ref[rows, :]))
        b = e_all[0:c]
        b_last = b[c - 1:c]
        for hd in range(GLA_HEADS):
            ks = slice(hd * GLA_DK, (hd + 1) * GLA_DK)
            vs = slice(hd * GLA_DV, (hd + 1) * GLA_DV)
            q = q_ref[rows, ks]
            k = k_ref[rows, ks]
            v = v_ref[rows, vs]
            bh = b[:, ks]
            bl = b_last[:, ks]
            state = state_ref[hd]
            o = _dot((q * jnp.exp(bh)).astype(BF16), state.astype(BF16))
            a = _dot_t(q.astype(BF16), k.astype(BF16)) * mask_ref[0]
            for li in range(len(GLA_LEVELS)):
                e = jnp.exp(e_all[(li + 1) * c:(li + 2) * c, ks])
                a = a + _dot_t((q * e).astype(BF16), (k * e).astype(BF16)) * mask_ref[li + 1]
            o = o + _dot(a.astype(BF16), v)
            k_dec = k * jnp.exp(bl - bh)
            state_ref[hd] = _decay_column(bl) * state + _dot(k_dec.T.astype(BF16), v)
            _gla_finish(o, gn_ref, go_ref[rows, vs], o_ref, rows, vs)
        return carry

    lax.fori_loop(0, MIXER_TILE_M // c, chunk, 0)


LOG2E = 1.4426950408889634


def _swa_bias_table(slope_ref, bias_ref):
    blk = SWA_BLOCK
    qi = lax.broadcasted_iota(jnp.int32, (blk, blk), 0)
    kj = lax.broadcasted_iota(jnp.int32, (blk, blk), 1)
    dist = jnp.where(kj <= qi, qi - kj, qi + blk - kj).astype(F32)
    for head in range(SWA_HEADS):
        bias_ref[head] = (slope_ref[head] * LOG2E) * dist


def _swa_tile(sink_ref, q, k, v, kprev_ref, vprev_ref, bias_ref, first_tile, o_ref):
    blk = SWA_BLOCK
    hd = SWA_HEAD_DIM
    group = SWA_HEADS // SWA_KV_HEADS
    kvs = range(SWA_KV_HEADS)

    qi = lax.broadcasted_iota(jnp.int32, (blk, blk), 0)
    kj = lax.broadcasted_iota(jnp.int32, (blk, blk), 1)
    own = kj <= qi
    first_valid = (kj - qi) <= jnp.where(first_tile, 0, blk)
    lane = lax.broadcasted_iota(jnp.int32, (blk, LANES), 1)
    low = lane < hd

    def dup(x, kv):
        tile = x[:, (kv // 2) * LANES:(kv // 2 + 1) * LANES]
        rolled = pltpu.roll(tile, hd, 1)
        lo_half, hi_half = (tile, rolled) if kv % 2 == 0 else (rolled, tile)
        return jnp.where(low, lo_half, hi_half).astype(BF16)

    def dup_all(x):
        xf = x.astype(F32)
        return [dup(xf, kv) for kv in kvs]

    zero = jnp.zeros((blk, LANES), BF16)
    prev_k = [kprev_ref[kv] for kv in kvs]
    prev_v = [vprev_ref[kv] for kv in kvs]
    for bi in range(MIXER_TILE_M // blk):
        rows = slice(bi * blk, (bi + 1) * blk)
        cur_k = dup_all(k[rows, :])
        cur_v = dup_all(v[rows, :])
        for kv in kvs:
            lhs = []
            for pair in (2 * kv, 2 * kv + 1):
                x = q[rows, pair * LANES:(pair + 1) * LANES]
                lhs += [jnp.where(low, x, zero), jnp.where(low, zero, x)]
            keys = jnp.concatenate([prev_k[kv], cur_k[kv]], axis=0)
            s_all = _dot_t(jnp.concatenate(lhs, axis=0), keys)
            probs = []
            for gi in range(group):
                head = kv * group + gi
                sc = s_all[gi * blk:(gi + 1) * blk]
                s = jnp.where(own, sc[:, blk:], sc[:, :blk]) * (hd ** -0.5 * LOG2E) - bias_ref[head]
                if bi == 0:
                    s = jnp.where(first_valid, s, NEG_INF)
                sink = sink_ref[head] * LOG2E
                m = jnp.maximum(jnp.max(s, axis=-1, keepdims=True), sink)
                p = jnp.exp2(s - m)
                denom = jnp.sum(p, axis=-1, keepdims=True) + jnp.exp2(sink - m)
                pn = p * (1.0 / denom)
                probs.append(jnp.concatenate([jnp.where(own, 0.0, pn), jnp.where(own, pn, 0.0)],
                                             axis=1).astype(BF16))
            vals = jnp.concatenate([prev_v[kv], cur_v[kv]], axis=0)
            r = _dot(jnp.concatenate(probs, axis=0), vals)
            for pi, pair in enumerate((2 * kv, 2 * kv + 1)):
                r0 = r[(2 * pi) * blk:(2 * pi + 1) * blk]
                r1 = r[(2 * pi + 1) * blk:(2 * pi + 2) * blk]
                o_ref[rows, pair * LANES:(pair + 1) * LANES] = jnp.where(low, r0, r1).astype(BF16)
            yield
        prev_k, prev_v = cur_k, cur_v
    for kv in kvs:
        kprev_ref[kv] = prev_k[kv]
        vprev_ref[kv] = prev_v[kv]


def _column_views(w_ref, widths):
    views, lo = [], 0
    for width in widths:
        views.append(w_ref.at[:, lo:lo + width])
        lo += width
    return views


def _interleave(first, second, ratio):
    live_first, live_second = True, True
    while live_first or live_second:
        for _ in range(ratio):
            if live_first:
                live_first = next(first, "done") != "done"
        if live_second:
            live_second = next(second, "done") != "done"


def _mixer_kernel(slope_ref, sink_ref, x_ref, ada_ref,
                  w_gla_ref, wlr_ref, w_rest_ref,
                  wup_ref, bup_ref, seg_ref, mask_ref, tri_ref, gn_ref,
                  wa_ref, wb_ref, wo_ref, lng_ref, lnb_ref,
                  next_wg_ref, next_wu_ref, next_wd_ref,
                  out_ref, next_wg_out, next_wu_out, next_wd_out,
                  q_s, k_s, v_s, la_s, go_s, b_s, o_s, a_s, state_ref, state_new, kprev_s, vprev_s,
                  bias_ref):
    first_tile = pl.program_id(0) == 0
    for src, dst in ((next_wg_ref, next_wg_out), (next_wu_ref, next_wu_out), (next_wd_ref, next_wd_out)):
        dst[...] = src[...].astype(BF16)
    wq_ref, wk_ref, wv_ref = _column_views(w_gla_ref, MIX_SIZES[0:3])
    wgo_ref, wsq_ref, wsk_ref, wsv_ref, wga_ref, wgb_ref = _column_views(w_rest_ref, MIX_SIZES[4:10])

    @pl.when(first_tile)
    def _():
        state_ref[...] = jnp.zeros_like(state_ref)
        kprev_s[...] = jnp.zeros_like(kprev_s)
        vprev_s[...] = jnp.zeros_like(vprev_s)
        _swa_bias_table(slope_ref, bias_ref)

    x = x_ref[...]
    h = (x * (1.0 + _ada_slice(ada_ref, 4)) + _ada_slice(ada_ref, 3)).astype(BF16)
    step = MIXER_DOT_COLS
    col_blocks = range(0, D_MODEL, step)

    sq = _dot(h, wsq_ref[...]).astype(BF16)
    sk = _dot(h, wsk_ref[...]).astype(BF16)
    sv = _dot(h, wsv_ref[...]).astype(BF16)

    gates = {}
    guard = {}

    def gate(name, w_ref, lo):
        gates[name, lo] = _sigmoid(_dot(h, w_ref[:, lo:lo + step]))

    def projections_1():
        lr = _dot(h, wlr_ref[...]).astype(BF16)
        z = _dot(lr, wup_ref[...]) + bup_ref[...]
        la_s[...] = _log_sigmoid(z) * (1.0 / GLA_GATE_NORMALIZER)
        yield
        cumsum = _gla_cumsum(la_s, tri_ref, b_s, guard)
        for lo in range(0, GLA_HEADS * GLA_DK, step):
            q_s[:, lo:lo + step] = _dot(h, wq_ref[:, lo:lo + step]) * (GLA_DK ** -0.5)
            next(cumsum)
            yield
            k_s[:, lo:lo + step] = _dot(h, wk_ref[:, lo:lo + step])
            next(cumsum)
            yield
        for lo in col_blocks:
            v_s[:, lo:lo + step] = _dot(h, wv_ref[:, lo:lo + step]).astype(BF16)
            next(cumsum, None)
            yield
        for _ in cumsum:
            pass
        for lo in col_blocks:
            go = _dot(h, wgo_ref[:, lo:lo + step])
            go_s[:, lo:lo + step] = go * _sigmoid(go)
            yield
        for lo in col_blocks:
            gate("a", wga_ref, lo)
            yield

    _interleave(_swa_tile(sink_ref, sq, sk, sv, kprev_s, vprev_s, bias_ref, first_tile, a_s),
                projections_1(), ratio=2)

    def projections_2():
        for lo in col_blocks:
            gate("b", wgb_ref, lo)
            yield
        for lo in col_blocks:
            gates["yb", lo] = _dot(a_s[...], wb_ref[:, lo:lo + step])
            yield

    _interleave(_gla_tile_fast(q_s, k_s, v_s, b_s, go_s, gn_ref, o_s, state_ref, state_new),
                projections_2(), ratio=2)
    safe = guard["safe"]

    @pl.when(jnp.logical_not(safe))
    def _():
        state_new[...] = state_ref[...]
        _gla_tile_any_decay(q_s, k_s, v_s, la_s, go_s, seg_ref, mask_ref, gn_ref, o_s, state_new)

    state_ref[...] = state_new[...]

    gate_a = jnp.concatenate([gates["a", lo] for lo in col_blocks], axis=1)
    gate_b_yb = jnp.concatenate([gates["b", lo] * gates["yb", lo] for lo in col_blocks], axis=1)
    rows_half = MIXER_TILE_M // 2
    for lo in (0, rows_half):
        rows = slice(lo, lo + rows_half)
        ya = _dot(o_s[rows, :], wa_ref[...])
        merged = (gate_a[rows, :] * ya + gate_b_yb[rows, :]).astype(BF16)
        y = _dot(merged, wo_ref[...])
        r = DEEPNORM_ALPHA * x[rows, :] + _ada_slice(ada_ref, 5) * y
        out_ref[rows, :] = _layer_norm(r, lng_ref[...], lnb_ref[...])


def _mixer(x, ada, ws, wup, bup, gn, slopes, sinks, wa, wb, wo, ln_g, ln_b, next_ffn):
    tm = MIXER_TILE_M
    steps = SEQ // tm
    w_gla, wlr, w_rest = ws
    seg, mask, tri = _gla_constants()
    qk = GLA_HEADS * GLA_DK
    row = pl.BlockSpec((tm, D_MODEL), lambda i: (i, 0))
    smem = pl.BlockSpec(memory_space=pltpu.SMEM)
    up_rows = pl.BlockSpec((D_MODEL // steps, D_FF), lambda i: (i, 0))
    down_rows = pl.BlockSpec((2 * D_FF // steps, D_MODEL), lambda i: (i // 2, 0))

    def w(n):
        return _resident((D_MODEL, n))

    vec = _resident((1, D_MODEL))
    return pl.pallas_call(
        _mixer_kernel,
        grid=(steps,),
        in_specs=[smem, smem, row, _resident((1, N_ADA * D_MODEL)),
                  w(sum(MIX_SIZES[0:3])), w(LR_PAD), w(sum(MIX_SIZES[4:10])),
                  _resident((LR_PAD, qk)), _resident((1, qk)),
                  _resident(seg.shape), _resident(mask.shape), _resident(tri.shape),
                  _resident((1, GLA_DV)),
                  w(D_MODEL), w(D_MODEL), w(D_MODEL), vec, vec,
                  up_rows, up_rows, down_rows],
        out_specs=[row, up_rows, up_rows, down_rows],
        out_shape=[jax.ShapeDtypeStruct((SEQ, D_MODEL), F32),
                   jax.ShapeDtypeStruct((D_MODEL, D_FF), BF16),
                   jax.ShapeDtypeStruct((D_MODEL, D_FF), BF16),
                   jax.ShapeDtypeStruct((D_FF, D_MODEL), BF16)],
        scratch_shapes=[pltpu.VMEM((tm, qk), F32), pltpu.VMEM((tm, qk), F32),
                        pltpu.VMEM((tm, D_MODEL), BF16), pltpu.VMEM((tm, qk), F32),
                        pltpu.VMEM((tm, D_MODEL), F32), pltpu.VMEM((tm, qk), F32),
                        pltpu.VMEM((tm, D_MODEL), BF16), pltpu.VMEM((tm, D_MODEL), BF16),
                        pltpu.VMEM((GLA_HEADS, GLA_DK, GLA_DV), F32),
                        pltpu.VMEM((GLA_HEADS, GLA_DK, GLA_DV), F32),
                        pltpu.VMEM((SWA_KV_HEADS, SWA_BLOCK, LANES), BF16),
                        pltpu.VMEM((SWA_KV_HEADS, SWA_BLOCK, LANES), BF16),
                        pltpu.VMEM((SWA_HEADS, SWA_BLOCK, SWA_BLOCK), F32)],
        compiler_params=_cparams(),
        name="mixer",
    )(slopes, sinks, x, ada, w_gla, wlr, w_rest, wup, bup,
      jnp.asarray(seg, BF16), jnp.asarray(mask, F32), jnp.asarray(tri, BF16), gn,
      wa, wb, wo, ln_g, ln_b, *next_ffn)


def _alibi_slopes(n):
    return 2.0 ** (-8.0 * jnp.arange(1, n + 1, dtype=jnp.float32) / n)


def kernel(x, c, w_ada, b_ada, ffn1_w_gate, ffn1_w_up, ffn1_w_down, ln1_g, ln1_b, w_in,
           w_gla_gate_up, b_gla_gate, gla_norm_g, w_branch_gla, swa_sinks, w_branch_swa, w_out,
           ln2_g, ln2_b, ffn2_w_gate, ffn2_w_up, ffn2_w_down, ln3_g, ln3_b):
    assert x.shape == (1, SEQ, D_MODEL) and w_ada.shape[0] == 1
    x2d = x.reshape(SEQ, D_MODEL)
    vec = lambda p: p.reshape(1, -1)

    ada = _ada(c, w_ada[0], b_ada[0])
    x1, w_gla, wlr, w_rest, wa, wb, wo = _ffn(
        x2d, ada, ffn1_w_gate[0].astype(BF16), ffn1_w_up[0].astype(BF16),
        ffn1_w_down[0].astype(BF16), vec(ln1_g), vec(ln1_b), 0,
        mixer_weights=(w_in[0], w_branch_gla[0], w_branch_swa[0], w_out[0]))

    wup = jnp.pad(w_gla_gate_up[0], ((0, LR_PAD - GLA_GATE_RANK), (0, 0))).astype(BF16)
    x2, wg2, wu2, wd2 = _mixer(
        x1, ada, (w_gla, wlr, w_rest), wup, vec(b_gla_gate), vec(gla_norm_g),
        _alibi_slopes(SWA_HEADS), swa_sinks[0], wa, wb, wo,
        vec(ln2_g), vec(ln2_b), (ffn2_w_gate[0], ffn2_w_up[0], ffn2_w_down[0]))

    out = _ffn(x2, ada, wg2, wu2, wd2, vec(ln3_g), vec(ln3_b), 6)
    return out.reshape(1, SEQ, D_MODEL)
```

```python
import functools

import numpy as np
import jax
import jax.numpy as jnp
from jax import lax
from jax.experimental import pallas as pl
from jax.experimental.pallas import tpu as pltpu

D_MODEL = 1024
SEQ = 16384
D_FF = 2816
N_ADA = 9
LN_EPS = 1e-5
RMS_EPS = 1e-6
NEG_INF = -1e30
DEEPNORM_ALPHA = 2.0 ** 0.25
MACARON_WEIGHT = 0.5

GLA_HEADS = 4
GLA_DK = 128
GLA_DV = 256
GLA_GATE_RANK = 16
GLA_GATE_NORMALIZER = 16.0
GLA_CHUNK = 64
GLA_LEVELS = (32, 16, 8, 4, 2, 1)

SWA_HEADS = 16
SWA_KV_HEADS = 4
SWA_HEAD_DIM = 64
SWA_BLOCK = 128

MIX_SIZES = (512, 512, 1024, 16, 1024, 1024, 256, 256, 1024, 1024)

LANES = 128
BF16_ROW_TILE = 16
V7X_VMEM_LIMIT_BYTES = 60 * 1024 * 1024

BF16 = jnp.bfloat16
F32 = jnp.float32


def _cparams():
    return pltpu.CompilerParams(dimension_semantics=("arbitrary",),
                                vmem_limit_bytes=V7X_VMEM_LIMIT_BYTES)


def _resident(shape):
    return pl.BlockSpec(shape, lambda i: (0,) * len(shape), pipeline_mode=pl.Buffered(1))


def _round_up(n, multiple):
    return -(-n // multiple) * multiple


def _dot(a, b):
    return jnp.dot(a, b, preferred_element_type=F32)


def _dot_t(a, b):
    return lax.dot_general(a, b, (((1,), (1,)), ((), ())), preferred_element_type=F32)


def _sigmoid(x):
    return 1.0 / (1.0 + jnp.exp(-x))


def _layer_norm(r, g, b):
    mu = jnp.mean(r, axis=-1, keepdims=True)
    d = r - mu
    var = jnp.mean(d * d, axis=-1, keepdims=True)
    return d * lax.rsqrt(var + LN_EPS) * g + b


def _ada_slice(ada_ref, idx):
    return ada_ref[:, idx * D_MODEL:(idx + 1) * D_MODEL]


ADA_TILE_N = 1536


def _ada_kernel(c_ref, w_ref, b_ref, o_ref):
    c = c_ref[...]
    s = (c * _sigmoid(c)).astype(BF16)
    s8 = jnp.broadcast_to(s, (8, D_MODEL))
    y = _dot(s8, w_ref[...].astype(BF16))
    o_ref[...] = y[0:1, :] + b_ref[...]


def _ada(c, w_ada, b_ada):
    n = N_ADA * D_MODEL
    return pl.pallas_call(
        _ada_kernel,
        grid=(n // ADA_TILE_N,),
        in_specs=[pl.BlockSpec((1, D_MODEL), lambda i: (0, 0)),
                  pl.BlockSpec((D_MODEL, ADA_TILE_N), lambda i: (0, i)),
                  pl.BlockSpec((1, ADA_TILE_N), lambda i: (0, i))],
        out_specs=pl.BlockSpec((1, ADA_TILE_N), lambda i: (0, i)),
        out_shape=jax.ShapeDtypeStruct((1, n), F32),
        compiler_params=_cparams(),
        name="ada",
    )(c, w_ada, b_ada.reshape(1, n))


FFN_TILE_M = 512
FFN_SPLIT = 1536


def _ffn_kernel(x_ref, ada_ref, wg_ref, wu_ref, wd_ref, lng_ref, lnb_ref, *rest, ada_base):
    n_cast = len(rest) // 2
    o_ref = rest[n_cast]
    for src, dst in zip(rest[:n_cast], rest[n_cast + 1:]):
        dst[...] = src[...].astype(BF16)
    x = x_ref[...]
    sh = _ada_slice(ada_ref, ada_base)
    sc = _ada_slice(ada_ref, ada_base + 1)
    gt = _ada_slice(ada_ref, ada_base + 2)
    h = (x * (1.0 + sc) + sh).astype(BF16)
    y = None
    for lo, hi in ((0, FFN_SPLIT), (FFN_SPLIT, D_FF)):
        g = _dot(h, wg_ref[:, lo:hi])
        u = _dot(h, wu_ref[:, lo:hi])
        a = (g * _sigmoid(g) * u).astype(BF16)
        part = _dot(a, wd_ref[lo:hi, :])
        y = part if y is None else y + part
    r = DEEPNORM_ALPHA * x + (MACARON_WEIGHT * gt) * y
    o_ref[...] = _layer_norm(r, lng_ref[...], lnb_ref[...])


def _ffn(x, ada, wg, wu, wd, ln_g, ln_b, ada_base, mixer_weights=None):
    tm = FFN_TILE_M
    steps = SEQ // tm
    row = pl.BlockSpec((tm, D_MODEL), lambda i: (i, 0))
    in_specs = [row, _resident((1, N_ADA * D_MODEL)),
                _resident((D_MODEL, D_FF)), _resident((D_MODEL, D_FF)), _resident((D_FF, D_MODEL)),
                _resident((1, D_MODEL)), _resident((1, D_MODEL))]
    out_specs = [row]
    out_shape = [jax.ShapeDtypeStruct((SEQ, D_MODEL), F32)]
    operands = [x, ada, wg, wu, wd, ln_g, ln_b]
    for w in mixer_weights or ():
        rows = _round_up(pl.cdiv(w.shape[0], steps), BF16_ROW_TILE)
        last = pl.cdiv(w.shape[0], rows) - 1
        spec = pl.BlockSpec((rows, D_MODEL), lambda i, last=last: (jnp.minimum(i, last), 0))
        in_specs.append(spec)
        operands.append(w)
        out_specs.append(spec)
        out_shape.append(jax.ShapeDtypeStruct(w.shape, BF16))
    outs = pl.pallas_call(
        functools.partial(_ffn_kernel, ada_base=ada_base),
        grid=(steps,),
        in_specs=in_specs,
        out_specs=out_specs,
        out_shape=out_shape,
        compiler_params=_cparams(),
        name="ffn",
    )(*operands)
    return outs[0] if mixer_weights is None else outs


MIXER_TILE_M = 512
MIXER_DOT_COLS = 512
LR_PAD = LANES
GLA_NSEG = 1 + len(GLA_LEVELS)
GLA_SPLIT = 2
GLA_FAST_CHUNK = 128
GLA_SAFE_LOG_DECAY = -60.0


def _log_sigmoid(z):
    return jnp.minimum(z, 0.0) - jnp.log(1.0 + jnp.exp(-jnp.abs(z)))


def _gla_constants():
    c = GLA_CHUNK
    t = np.arange(c)[:, None]
    u = np.arange(c)[None, :]
    seg = [(u <= t)]
    mask = [(u == t)]
    for m in GLA_LEVELS:
        blk_t, blk_u = t // m, u // m
        odd = (blk_t % 2) == 1
        p_odd = blk_t * m
        p_even = (blk_t + 1) * m
        seg.append(np.where(odd, (u > p_odd) & (u <= t), (u > t) & (u <= p_even)))
        mask.append(odd & (blk_u == blk_t - 1))
    seg = np.concatenate(seg, axis=0).astype(np.float32)
    seg = np.concatenate([seg] * GLA_SPLIT, axis=1)
    mask = np.stack(mask).astype(np.float32)
    fc = GLA_FAST_CHUNK
    tri = np.tril(np.ones((fc, fc), np.float32))
    tri = np.concatenate([tri] * GLA_SPLIT, axis=1)
    return seg, mask, tri


def _split_terms(g):
    parts = []
    rem = g
    for _ in range(GLA_SPLIT):
        p = rem.astype(BF16)
        parts.append(p)
        rem = rem - p.astype(F32)
    return jnp.concatenate(parts, axis=0)


def _decay_column(bl):
    return jnp.exp(jnp.broadcast_to(bl, (8, GLA_DK))).T[:, 0:1]


def _gla_finish(o, gn_ref, go, o_ref, rows, vs):
    ms = jnp.mean(o * o, axis=-1, keepdims=True)
    on = o * lax.rsqrt(ms + RMS_EPS) * gn_ref[...]
    o_ref[rows, vs] = (on * go).astype(BF16)


def _gla_cumsum(la_ref, tri_ref, b_ref, result):
    c = GLA_FAST_CHUNK
    lowest = None
    for ci in range(MIXER_TILE_M // c):
        rows = slice(ci * c, (ci + 1) * c)
        b = _dot(tri_ref[...], _split_terms(la_ref[rows, :]))
        b_ref[rows, :] = b
        total = jnp.min(b[c - 1:c])
        lowest = total if lowest is None else jnp.minimum(lowest, total)
        result["safe"] = lowest >= GLA_SAFE_LOG_DECAY
        yield


def _gla_tile_fast(q_ref, k_ref, v_ref, b_ref, go_ref, gn_ref, o_ref, state_in_ref, state_out_ref):
    c = GLA_FAST_CHUNK
    ti = lax.broadcasted_iota(jnp.int32, (c, c), 0)
    si = lax.broadcasted_iota(jnp.int32, (c, c), 1)
    causal = si <= ti
    chunks = range(MIXER_TILE_M // c)
    heads = range(GLA_HEADS)
    rows = [slice(ci * c, (ci + 1) * c) for ci in chunks]
    kcols = [slice(hd * GLA_DK, (hd + 1) * GLA_DK) for hd in heads]
    vcols = [slice(hd * GLA_DV, (hd + 1) * GLA_DV) for hd in heads]

    q_in, a, upd, d_col = {}, {}, {}, {}
    for ci in chunks:
        b = b_ref[rows[ci], :]
        b_last = b[c - 1:c]
        e_pos = jnp.exp(b)
        e_neg = jnp.exp(-b)
        e_last = jnp.exp(b_last)
        for hd in heads:
            ks = kcols[hd]
            q_in[ci, hd] = (q_ref[rows[ci], ks] * e_pos[:, ks]).astype(BF16)
            k_out = k_ref[rows[ci], ks] * e_neg[:, ks]
            a[ci, hd] = jnp.where(causal, _dot_t(q_in[ci, hd], k_out.astype(BF16)), 0.0).astype(BF16)
            k_dec_t = (k_out * e_last[:, ks]).T.astype(BF16)
            upd[ci, hd] = _dot(k_dec_t, v_ref[rows[ci], vcols[hd]])
            d_col[ci, hd] = _decay_column(b_last[:, ks])
        yield

    for hd in heads:
        state = state_in_ref[hd]
        for ci in chunks:
            v = v_ref[rows[ci], vcols[hd]]
            o = _dot(jnp.concatenate([q_in[ci, hd], a[ci, hd]], axis=1),
                     jnp.concatenate([state.astype(BF16), v], axis=0))
            state = d_col[ci, hd] * state + upd[ci, hd]
            _gla_finish(o, gn_ref, go_ref[rows[ci], vcols[hd]], o_ref, rows[ci], vcols[hd])
        state_out_ref[hd] = state
        yield


def _gla_tile_any_decay(q_ref, k_ref, v_ref, la_ref, go_ref, seg_ref, mask_ref, gn_ref, o_ref,
                        state_ref):
    c = GLA_CHUNK

    def chunk(ci, carry):
        rows = pl.ds(pl.multiple_of(ci * c, c), c)
        e_all = _dot(seg_ref[...], _split_terms(la_ref[rows, :]))
        b = e_all[0:c]
        b_last = b[c - 1:c]
        for hd in range(GLA_HEADS):
            ks = slice(hd * GLA_DK, (hd + 1) * GLA_DK)
            vs = slice(hd * GLA_DV, (hd + 1) * GLA_DV)
            q = q_ref[rows, ks]
            k = k_ref[rows, ks]
            v = v_ref[rows, vs]
            bh = b[:, ks]
            bl = b_last[:, ks]
            state = state_ref[hd]
            o = _dot((q * jnp.exp(bh)).astype(BF16), state.astype(BF16))
            a = _dot_t(q.astype(BF16), k.astype(BF16)) * mask_ref[0]
            for li in range(len(GLA_LEVELS)):
                e = jnp.exp(e_all[(li + 1) * c:(li + 2) * c, ks])
                a = a + _dot_t((q * e).astype(BF16), (k * e).astype(BF16)) * mask_ref[li + 1]
            o = o + _dot(a.astype(BF16), v)
            k_dec = k * jnp.exp(bl - bh)
            state_ref[hd] = _decay_column(bl) * state + _dot(k_dec.T.astype(BF16), v)
            _gla_finish(o, gn_ref, go_ref[rows, vs], o_ref, rows, vs)
        return carry

    lax.fori_loop(0, MIXER_TILE_M // c, chunk, 0)


LOG2E = 1.4426950408889634


def _swa_bias_table(slope_ref, bias_ref):
    blk = SWA_BLOCK
    qi = lax.broadcasted_iota(jnp.int32, (blk, blk), 0)
    kj = lax.broadcasted_iota(jnp.int32, (blk, blk), 1)
    dist = jnp.where(kj <= qi, qi - kj, qi + blk - kj).astype(F32)
    for head in range(SWA_HEADS):
        bias_ref[head] = (slope_ref[head] * LOG2E) * dist


def _swa_tile(sink_ref, q, k, v, kprev_ref, vprev_ref, bias_ref, first_tile, o_ref):
    blk = SWA_BLOCK
    hd = SWA_HEAD_DIM
    group = SWA_HEADS // SWA_KV_HEADS
    kvs = range(SWA_KV_HEADS)

    qi = lax.broadcasted_iota(jnp.int32, (blk, blk), 0)
    kj = lax.broadcasted_iota(jnp.int32, (blk, blk), 1)
    own = kj <= qi
    first_valid = (kj - qi) <= jnp.where(first_tile, 0, blk)
    lane = lax.broadcasted_iota(jnp.int32, (blk, LANES), 1)
    low = lane < hd

    def dup(x, kv):
        tile = x[:, (kv // 2) * LANES:(kv // 2 + 1) * LANES]
        rolled = pltpu.roll(tile, hd, 1)
        lo_half, hi_half = (tile, rolled) if kv % 2 == 0 else (rolled, tile)
        return jnp.where(low, lo_half, hi_half).astype(BF16)

    def dup_all(x):
        xf = x.astype(F32)
        return [dup(xf, kv) for kv in kvs]

    zero = jnp.zeros((blk, LANES), BF16)
    prev_k = [kprev_ref[kv] for kv in kvs]
    prev_v = [vprev_ref[kv] for kv in kvs]
    for bi in range(MIXER_TILE_M // blk):
        rows = slice(bi * blk, (bi + 1) * blk)
        cur_k = dup_all(k[rows, :])
        cur_v = dup_all(v[rows, :])
        for kv in kvs:
            lhs = []
            for pair in (2 * kv, 2 * kv + 1):
                x = q[rows, pair * LANES:(pair + 1) * LANES]
                lhs += [jnp.where(low, x, zero), jnp.where(low, zero, x)]
            keys = jnp.concatenate([prev_k[kv], cur_k[kv]], axis=0)
            s_all = _dot_t(jnp.concatenate(lhs, axis=0), keys)
            probs = []
            for gi in range(group):
                head = kv * group + gi
                sc = s_all[gi * blk:(gi + 1) * blk]
                s = jnp.where(own, sc[:, blk:], sc[:, :blk]) * (hd ** -0.5 * LOG2E) - bias_ref[head]
                if bi == 0:
                    s = jnp.where(first_valid, s, NEG_INF)
                sink = sink_ref[head] * LOG2E
                m = jnp.maximum(jnp.max(s, axis=-1, keepdims=True), sink)
                p = jnp.exp2(s - m)
                denom = jnp.sum(p, axis=-1, keepdims=True) + jnp.exp2(sink - m)
                pn = p * (1.0 / denom)
                probs.append(jnp.concatenate([jnp.where(own, 0.0, pn), jnp.where(own, pn, 0.0)],
                                             axis=1).astype(BF16))
            vals = jnp.concatenate([prev_v[kv], cur_v[kv]], axis=0)
            r = _dot(jnp.concatenate(probs, axis=0), vals)
            for pi, pair in enumerate((2 * kv, 2 * kv + 1)):
                r0 = r[(2 * pi) * blk:(2 * pi + 1) * blk]
                r1 = r[(2 * pi + 1) * blk:(2 * pi + 2) * blk]
                o_ref[rows, pair * LANES:(pair + 1) * LANES] = jnp.where(low, r0, r1).astype(BF16)
            yield
        prev_k, prev_v = cur_k, cur_v
    for kv in kvs:
        kprev_ref[kv] = prev_k[kv]
        vprev_ref[kv] = prev_v[kv]


def _row_views(w_ref, offsets, widths):
    return [w_ref.at[lo:lo + width, :] for lo, width in zip(offsets, widths)]


def _interleave(first, second, ratio):
    live_first, live_second = True, True
    while live_first or live_second:
        for _ in range(ratio):
            if live_first:
                live_first = next(first, "done") != "done"
        if live_second:
            live_second = next(second, "done") != "done"


def _mixer_kernel(slope_ref, sink_ref, x_ref, ada_ref,
                  w_in_t_ref,
                  wup_ref, bup_ref, seg_ref, mask_ref, tri_ref, gn_ref,
                  wa_ref, wb_ref, wo_ref, lng_ref, lnb_ref,
                  next_wg_ref, next_wu_ref, next_wd_ref,
                  out_ref, next_wg_out, next_wu_out, next_wd_out,
                  q_s, k_s, v_s, la_s, go_s, b_s, o_s, a_s, state_ref, state_new, kprev_s, vprev_s,
                  bias_ref):
    first_tile = pl.program_id(0) == 0
    for src, dst in ((next_wg_ref, next_wg_out), (next_wu_ref, next_wu_out), (next_wd_ref, next_wd_out)):
        dst[...] = src[...].astype(BF16)
    offsets = [sum(MIX_SIZES[:i]) for i in range(len(MIX_SIZES))]
    widths = list(MIX_SIZES)
    widths[3] = LR_PAD
    (wq_ref, wk_ref, wv_ref, wlr_ref, wgo_ref, wsq_ref, wsk_ref, wsv_ref, wga_ref,
     wgb_ref) = _row_views(w_in_t_ref, offsets, widths)

    def project(w_ref, lo=None):
        return _dot_t(h, w_ref[...] if lo is None else w_ref[lo:lo + step, :])

    @pl.when(first_tile)
    def _():
        state_ref[...] = jnp.zeros_like(state_ref)
        kprev_s[...] = jnp.zeros_like(kprev_s)
        vprev_s[...] = jnp.zeros_like(vprev_s)
        _swa_bias_table(slope_ref, bias_ref)

    x = x_ref[...]
    h = (x * (1.0 + _ada_slice(ada_ref, 4)) + _ada_slice(ada_ref, 3)).astype(BF16)
    step = MIXER_DOT_COLS
    col_blocks = range(0, D_MODEL, step)

    sq = project(wsq_ref).astype(BF16)
    sk = project(wsk_ref).astype(BF16)
    sv = project(wsv_ref).astype(BF16)

    gates = {}
    guard = {}

    def gate(name, w_ref, lo):
        gates[name, lo] = _sigmoid(project(w_ref, lo))

    def projections_1():
        lr = project(wlr_ref).astype(BF16)
        z = _dot(lr, wup_ref[...]) + bup_ref[...]
        la_s[...] = _log_sigmoid(z) * (1.0 / GLA_GATE_NORMALIZER)
        yield
        cumsum = _gla_cumsum(la_s, tri_ref, b_s, guard)
        for lo in range(0, GLA_HEADS * GLA_DK, step):
            q_s[:, lo:lo + step] = project(wq_ref, lo) * (GLA_DK ** -0.5)
            next(cumsum)
            yield
            k_s[:, lo:lo + step] = project(wk_ref, lo)
            next(cumsum)
            yield
        for lo in col_blocks:
            v_s[:, lo:lo + step] = project(wv_ref, lo).astype(BF16)
            next(cumsum, None)
            yield
        for _ in cumsum:
            pass
        for lo in col_blocks:
            go = project(wgo_ref, lo)
            go_s[:, lo:lo + step] = go * _sigmoid(go)
            yield
        for lo in col_blocks:
            gate("a", wga_ref, lo)
            yield

    _interleave(_swa_tile(sink_ref, sq, sk, sv, kprev_s, vprev_s, bias_ref, first_tile, a_s),
                projections_1(), ratio=2)

    def projections_2():
        for lo in col_blocks:
            gate("b", wgb_ref, lo)
            yield
        for lo in col_blocks:
            gates["yb", lo] = _dot(a_s[...], wb_ref[:, lo:lo + step])
            yield

    _interleave(_gla_tile_fast(q_s, k_s, v_s, b_s, go_s, gn_ref, o_s, state_ref, state_new),
                projections_2(), ratio=2)
    safe = guard["safe"]

    @pl.when(jnp.logical_not(safe))
    def _():
        state_new[...] = state_ref[...]
        _gla_tile_any_decay(q_s, k_s, v_s, la_s, go_s, seg_ref, mask_ref, gn_ref, o_s, state_new)

    state_ref[...] = state_new[...]

    gate_a = jnp.concatenate([gates["a", lo] for lo in col_blocks], axis=1)
    gate_b_yb = jnp.concatenate([gates["b", lo] * gates["yb", lo] for lo in col_blocks], axis=1)
    rows_half = MIXER_TILE_M // 2
    for lo in (0, rows_half):
        rows = slice(lo, lo + rows_half)
        ya = _dot(o_s[rows, :], wa_ref[...])
        merged = (gate_a[rows, :] * ya + gate_b_yb[rows, :]).astype(BF16)
        y = _dot(merged, wo_ref[...])
        r = DEEPNORM_ALPHA * x[rows, :] + _ada_slice(ada_ref, 5) * y
        out_ref[rows, :] = _layer_norm(r, lng_ref[...], lnb_ref[...])


def _mixer(x, ada, w_in_t, wup, bup, gn, slopes, sinks, wa, wb, wo, ln_g, ln_b, next_ffn):
    tm = MIXER_TILE_M
    steps = SEQ // tm
    seg, mask, tri = _gla_constants()
    qk = GLA_HEADS * GLA_DK
    row = pl.BlockSpec((tm, D_MODEL), lambda i: (i, 0))
    smem = pl.BlockSpec(memory_space=pltpu.SMEM)
    up_rows = pl.BlockSpec((D_MODEL // steps, D_FF), lambda i: (i, 0))
    down_rows = pl.BlockSpec((2 * D_FF // steps, D_MODEL), lambda i: (i // 2, 0))

    def w(n):
        return _resident((D_MODEL, n))

    vec = _resident((1, D_MODEL))
    return pl.pallas_call(
        _mixer_kernel,
        grid=(steps,),
        in_specs=[smem, smem, row, _resident((1, N_ADA * D_MODEL)),
                  _resident((sum(MIX_SIZES), D_MODEL)),
                  _resident((LR_PAD, qk)), _resident((1, qk)),
                  _resident(seg.shape), _resident(mask.shape), _resident(tri.shape),
                  _resident((1, GLA_DV)),
                  w(D_MODEL), w(D_MODEL), w(D_MODEL), vec, vec,
                  up_rows, up_rows, down_rows],
        out_specs=[row, up_rows, up_rows, down_rows],
        out_shape=[jax.ShapeDtypeStruct((SEQ, D_MODEL), F32),
                   jax.ShapeDtypeStruct((D_MODEL, D_FF), BF16),
                   jax.ShapeDtypeStruct((D_MODEL, D_FF), BF16),
                   jax.ShapeDtypeStruct((D_FF, D_MODEL), BF16)],
        scratch_shapes=[pltpu.VMEM((tm, qk), F32), pltpu.VMEM((tm, qk), F32),
                        pltpu.VMEM((tm, D_MODEL), BF16), pltpu.VMEM((tm, qk), F32),
                        pltpu.VMEM((tm, D_MODEL), F32), pltpu.VMEM((tm, qk), F32),
                        pltpu.VMEM((tm, D_MODEL), BF16), pltpu.VMEM((tm, D_MODEL), BF16),
                        pltpu.VMEM((GLA_HEADS, GLA_DK, GLA_DV), F32),
                        pltpu.VMEM((GLA_HEADS, GLA_DK, GLA_DV), F32),
                        pltpu.VMEM((SWA_KV_HEADS, SWA_BLOCK, LANES), BF16),
                        pltpu.VMEM((SWA_KV_HEADS, SWA_BLOCK, LANES), BF16),
                        pltpu.VMEM((SWA_HEADS, SWA_BLOCK, SWA_BLOCK), F32)],
        compiler_params=_cparams(),
        name="mixer",
    )(slopes, sinks, x, ada, w_in_t, wup, bup,
      jnp.asarray(seg, BF16), jnp.asarray(mask, F32), jnp.asarray(tri, BF16), gn,
      wa, wb, wo, ln_g, ln_b, *next_ffn)


def _alibi_slopes(n):
    return 2.0 ** (-8.0 * jnp.arange(1, n + 1, dtype=jnp.float32) / n)


def kernel(x, c, w_ada, b_ada, ffn1_w_gate, ffn1_w_up, ffn1_w_down, ln1_g, ln1_b, w_in,
           w_gla_gate_up, b_gla_gate, gla_norm_g, w_branch_gla, swa_sinks, w_branch_swa, w_out,
           ln2_g, ln2_b, ffn2_w_gate, ffn2_w_up, ffn2_w_down, ln3_g, ln3_b):
    assert x.shape == (1, SEQ, D_MODEL) and w_ada.shape[0] == 1
    x2d = x.reshape(SEQ, D_MODEL)
    vec = lambda p: p.reshape(1, -1)

    ada = _ada(c, w_ada[0], b_ada[0])
    x1, w_in_t, wa, wb, wo = _ffn(
        x2d, ada, ffn1_w_gate[0].astype(BF16), ffn1_w_up[0].astype(BF16),
        ffn1_w_down[0].astype(BF16), vec(ln1_g), vec(ln1_b), 0,
        mixer_weights=(jnp.swapaxes(w_in[0], 0, 1), w_branch_gla[0], w_branch_swa[0], w_out[0]))

    wup = jnp.pad(w_gla_gate_up[0], ((0, LR_PAD - GLA_GATE_RANK), (0, 0))).astype(BF16)
    x2, wg2, wu2, wd2 = _mixer(
        x1, ada, w_in_t, wup, vec(b_gla_gate), vec(gla_norm_g),
        _alibi_slopes(SWA_HEADS), swa_sinks[0], wa, wb, wo,
        vec(ln2_g), vec(ln2_b), (ffn2_w_gate[0], ffn2_w_up[0], ffn2_w_down[0]))

    out = _ffn(x2, ada, wg2, wu2, wd2, vec(ln3_g), vec(ln3_b), 6)
    return out.reshape(1, SEQ, D_MODEL)
```

```python
import functools

import numpy as np
import jax
import jax.numpy as jnp
from jax import lax
from jax.experimental import pallas as pl
from jax.experimental.pallas import tpu as pltpu

D_MODEL = 1024
SEQ = 16384
D_FF = 2816
N_ADA = 9
LN_EPS = 1e-5
RMS_EPS = 1e-6
NEG_INF = -1e30
DEEPNORM_ALPHA = 2.0 ** 0.25
MACARON_WEIGHT = 0.5

GLA_HEADS = 4
GLA_DK = 128
GLA_DV = 256
GLA_GATE_RANK = 16
GLA_GATE_NORMALIZER = 16.0
GLA_CHUNK = 64
GLA_LEVELS = (32, 16, 8, 4, 2, 1)

SWA_HEADS = 16
SWA_KV_HEADS = 4
SWA_HEAD_DIM = 64
SWA_BLOCK = 128

MIX_SIZES = (512, 512, 1024, 16, 1024, 1024, 256, 256, 1024, 1024)

LANES = 128
BF16_ROW_TILE = 16
V7X_VMEM_LIMIT_BYTES = 60 * 1024 * 1024

BF16 = jnp.bfloat16
F32 = jnp.float32


def _cparams():
    return pltpu.CompilerParams(dimension_semantics=("arbitrary",),
                                vmem_limit_bytes=V7X_VMEM_LIMIT_BYTES)


def _resident(shape):
    return pl.BlockSpec(shape, lambda i: (0,) * len(shape), pipeline_mode=pl.Buffered(1))


def _round_up(n, multiple):
    return -(-n // multiple) * multiple


def _dot(a, b):
    return jnp.dot(a, b, preferred_element_type=F32)


def _dot_t(a, b):
    return lax.dot_general(a, b, (((1,), (1,)), ((), ())), preferred_element_type=F32)


def _sigmoid(x):
    return 1.0 / (1.0 + jnp.exp(-x))


def _layer_norm(r, g, b):
    mu = jnp.mean(r, axis=-1, keepdims=True)
    d = r - mu
    var = jnp.mean(d * d, axis=-1, keepdims=True)
    return d * lax.rsqrt(var + LN_EPS) * g + b


def _tie(value, other):
    bits = pltpu.bitcast(other, jnp.uint32)
    words = [bits[r:r + 8, c:c + LANES]
             for r in range(0, other.shape[0], 8) for c in range(0, other.shape[1], LANES)]
    while len(words) > 1:
        words = [a | b for a, b in zip(words[0::2], words[1::2])] + words[len(words) & ~1:]
    zero = lax.shift_right_logical(lax.shift_right_logical(words[0], jnp.uint32(16)), jnp.uint32(16))
    zero = jnp.concatenate([zero, zero], axis=0).astype(jnp.int32).astype(F32).astype(value.dtype)
    tile = value[0:BF16_ROW_TILE, 0:LANES] + zero
    top = jnp.concatenate([tile, value[0:BF16_ROW_TILE, LANES:]], axis=1)
    return jnp.concatenate([top, value[BF16_ROW_TILE:, :]], axis=0)


def _ada_slice(ada_ref, idx):
    return ada_ref[:, idx * D_MODEL:(idx + 1) * D_MODEL]


ADA_TILE_N = 1536


def _ada_kernel(c_ref, w_ref, b_ref, o_ref):
    c = c_ref[...]
    s = (c * _sigmoid(c)).astype(BF16)
    s8 = jnp.broadcast_to(s, (8, D_MODEL))
    y = _dot(s8, w_ref[...].astype(BF16))
    o_ref[...] = y[0:1, :] + b_ref[...]


def _ada(c, w_ada, b_ada):
    n = N_ADA * D_MODEL
    return pl.pallas_call(
        _ada_kernel,
        grid=(n // ADA_TILE_N,),
        in_specs=[pl.BlockSpec((1, D_MODEL), lambda i: (0, 0)),
                  pl.BlockSpec((D_MODEL, ADA_TILE_N), lambda i: (0, i)),
                  pl.BlockSpec((1, ADA_TILE_N), lambda i: (0, i))],
        out_specs=pl.BlockSpec((1, ADA_TILE_N), lambda i: (0, i)),
        out_shape=jax.ShapeDtypeStruct((1, n), F32),
        compiler_params=_cparams(),
        name="ada",
    )(c, w_ada, b_ada.reshape(1, n))


FFN_TILE_M = 512
FFN_SPLIT = 1536


def _ffn_kernel(x_ref, ada_ref, wg_ref, wu_ref, wd_ref, lng_ref, lnb_ref, *rest, ada_base):
    n_cast = len(rest) // 2
    o_ref = rest[n_cast]
    for src, dst in zip(rest[:n_cast], rest[n_cast + 1:]):
        dst[...] = src[...].astype(BF16)
    x = x_ref[...]
    sh = _ada_slice(ada_ref, ada_base)
    sc = _ada_slice(ada_ref, ada_base + 1)
    gt = _ada_slice(ada_ref, ada_base + 2)
    h = (x * (1.0 + sc) + sh).astype(BF16)
    y = None
    for lo, hi in ((0, FFN_SPLIT), (FFN_SPLIT, D_FF)):
        g = _dot(h, wg_ref[:, lo:hi])
        u = _dot(h, wu_ref[:, lo:hi])
        a = (g * _sigmoid(g) * u).astype(BF16)
        part = _dot(a, wd_ref[lo:hi, :])
        y = part if y is None else y + part
    r = DEEPNORM_ALPHA * x + (MACARON_WEIGHT * gt) * y
    o_ref[...] = _layer_norm(r, lng_ref[...], lnb_ref[...])


def _ffn(x, ada, wg, wu, wd, ln_g, ln_b, ada_base, mixer_weights=None):
    tm = FFN_TILE_M
    steps = SEQ // tm
    row = pl.BlockSpec((tm, D_MODEL), lambda i: (i, 0))
    in_specs = [row, _resident((1, N_ADA * D_MODEL)),
                _resident((D_MODEL, D_FF)), _resident((D_MODEL, D_FF)), _resident((D_FF, D_MODEL)),
                _resident((1, D_MODEL)), _resident((1, D_MODEL))]
    out_specs = [row]
    out_shape = [jax.ShapeDtypeStruct((SEQ, D_MODEL), F32)]
    operands = [x, ada, wg, wu, wd, ln_g, ln_b]
    for w in mixer_weights or ():
        rows = _round_up(pl.cdiv(w.shape[0], steps), BF16_ROW_TILE)
        last = pl.cdiv(w.shape[0], rows) - 1
        spec = pl.BlockSpec((rows, D_MODEL), lambda i, last=last: (jnp.minimum(i, last), 0))
        in_specs.append(spec)
        operands.append(w)
        out_specs.append(spec)
        out_shape.append(jax.ShapeDtypeStruct(w.shape, BF16))
    outs = pl.pallas_call(
        functools.partial(_ffn_kernel, ada_base=ada_base),
        grid=(steps,),
        in_specs=in_specs,
        out_specs=out_specs,
        out_shape=out_shape,
        compiler_params=_cparams(),
        name="ffn",
    )(*operands)
    return outs[0] if mixer_weights is None else outs


MIXER_TILE_M = 512
MIXER_DOT_COLS = 512
LR_PAD = LANES
GLA_NSEG = 1 + len(GLA_LEVELS)
GLA_SPLIT = 2
GLA_FAST_CHUNK = 128
GLA_SAFE_LOG_DECAY = -60.0


def _log_sigmoid(z):
    return jnp.minimum(z, 0.0) - jnp.log(1.0 + jnp.exp(-jnp.abs(z)))


def _gla_constants():
    c = GLA_CHUNK
    t = np.arange(c)[:, None]
    u = np.arange(c)[None, :]
    seg = [(u <= t)]
    mask = [(u == t)]
    for m in GLA_LEVELS:
        blk_t, blk_u = t // m, u // m
        odd = (blk_t % 2) == 1
        p_odd = blk_t * m
        p_even = (blk_t + 1) * m
        seg.append(np.where(odd, (u > p_odd) & (u <= t), (u > t) & (u <= p_even)))
        mask.append(odd & (blk_u == blk_t - 1))
    seg = np.concatenate(seg, axis=0).astype(np.float32)
    seg = np.concatenate([seg] * GLA_SPLIT, axis=1)
    mask = np.stack(mask).astype(np.float32)
    fc = GLA_FAST_CHUNK
    tri = np.tril(np.ones((fc, fc), np.float32))
    tri = np.concatenate([tri] * GLA_SPLIT, axis=1)
    return seg, mask, tri


def _split_terms(g):
    parts = []
    rem = g
    for _ in range(GLA_SPLIT):
        p = rem.astype(BF16)
        parts.append(p)
        rem = rem - p.astype(F32)
    return jnp.concatenate(parts, axis=0)


def _decay_column(bl):
    return jnp.exp(jnp.broadcast_to(bl, (8, GLA_DK))).T[:, 0:1]


def _gla_finish(o, gn_ref, go, o_ref, rows, vs):
    ms = jnp.mean(o * o, axis=-1, keepdims=True)
    on = o * lax.rsqrt(ms + RMS_EPS) * gn_ref[...]
    o_ref[rows, vs] = (on * go).astype(BF16)


def _gla_cumsum(la_ref, tri_ref, b_ref, result):
    c = GLA_FAST_CHUNK
    lowest = None
    for ci in range(MIXER_TILE_M // c):
        rows = slice(ci * c, (ci + 1) * c)
        b = _dot(tri_ref[...], _split_terms(la_ref[rows, :]))
        b_ref[rows, :] = b
        total = jnp.min(b[c - 1:c])
        lowest = total if lowest is None else jnp.minimum(lowest, total)
        result["safe"] = lowest >= GLA_SAFE_LOG_DECAY
        yield


def _gla_tile_fast(q_ref, k_ref, v_ref, b_ref, go_ref, gn_ref, o_ref, state_in_ref, state_out_ref):
    c = GLA_FAST_CHUNK
    ti = lax.broadcasted_iota(jnp.int32, (c, c), 0)
    si = lax.broadcasted_iota(jnp.int32, (c, c), 1)
    causal = si <= ti
    chunks = range(MIXER_TILE_M // c)
    heads = range(GLA_HEADS)
    rows = [slice(ci * c, (ci + 1) * c) for ci in chunks]
    kcols = [slice(hd * GLA_DK, (hd + 1) * GLA_DK) for hd in heads]
    vcols = [slice(hd * GLA_DV, (hd + 1) * GLA_DV) for hd in heads]

    q_in, a, upd, d_col = {}, {}, {}, {}
    for ci in chunks:
        b = b_ref[rows[ci], :]
        b_last = b[c - 1:c]
        e_pos = jnp.exp(b)
        e_neg = jnp.exp(-b)
        e_last = jnp.exp(b_last)
        for hd in heads:
            ks = kcols[hd]
            q_in[ci, hd] = (q_ref[rows[ci], ks] * e_pos[:, ks]).astype(BF16)
            k_out = k_ref[rows[ci], ks] * e_neg[:, ks]
            a[ci, hd] = jnp.where(causal, _dot_t(q_in[ci, hd], k_out.astype(BF16)), 0.0).astype(BF16)
            k_dec_t = (k_out * e_last[:, ks]).T.astype(BF16)
            upd[ci, hd] = _dot(k_dec_t, v_ref[rows[ci], vcols[hd]])
            d_col[ci, hd] = _decay_column(b_last[:, ks])
        yield

    for hd in heads:
        state = state_in_ref[hd]
        for ci in chunks:
            v = v_ref[rows[ci], vcols[hd]]
            o = _dot(jnp.concatenate([q_in[ci, hd], a[ci, hd]], axis=1),
                     jnp.concatenate([state.astype(BF16), v], axis=0))
            state = d_col[ci, hd] * state + upd[ci, hd]
            _gla_finish(o, gn_ref, go_ref[rows[ci], vcols[hd]], o_ref, rows[ci], vcols[hd])
        state_out_ref[hd] = state
        yield


def _gla_tile_any_decay(q_ref, k_ref, v_ref, la_ref, go_ref, seg_ref, mask_ref, gn_ref, o_ref,
                        state_ref):
    c = GLA_CHUNK

    def chunk(ci, carry):
        rows = pl.ds(pl.multiple_of(ci * c, c), c)
        e_all = _dot(seg_ref[...], _split_terms(la_ref[rows, :]))
        b = e_all[0:c]
        b_last = b[c - 1:c]
        for hd in range(GLA_HEADS):
            ks = slice(hd * GLA_DK, (hd + 1) * GLA_DK)
            vs = slice(hd * GLA_DV, (hd + 1) * GLA_DV)
            q = q_ref[rows, ks]
            k = k_ref[rows, ks]
            v = v_ref[rows, vs]
            bh = b[:, ks]
            bl = b_last[:, ks]
            state = state_ref[hd]
            o = _dot((q * jnp.exp(bh)).astype(BF16), state.astype(BF16))
            a = _dot_t(q.astype(BF16), k.astype(BF16)) * mask_ref[0]
            for li in range(len(GLA_LEVELS)):
                e = jnp.exp(e_all[(li + 1) * c:(li + 2) * c, ks])
                a = a + _dot_t((q * e).astype(BF16), (k * e).astype(BF16)) * mask_ref[li + 1]
            o = o + _dot(a.astype(BF16), v)
            k_dec = k * jnp.exp(bl - bh)
            state_ref[hd] = _decay_column(bl) * state + _dot(k_dec.T.astype(BF16), v)
            _gla_finish(o, gn_ref, go_ref[rows, vs], o_ref, rows, vs)
        return carry

    lax.fori_loop(0, MIXER_TILE_M // c, chunk, 0)


LOG2E = 1.4426950408889634


def _swa_bias_table(slope_ref, bias_ref):
    blk = SWA_BLOCK
    qi = lax.broadcasted_iota(jnp.int32, (blk, blk), 0)
    kj = lax.broadcasted_iota(jnp.int32, (blk, blk), 1)
    dist = jnp.where(kj <= qi, qi - kj, qi + blk - kj).astype(F32)
    for head in range(SWA_HEADS):
        bias_ref[head] = (slope_ref[head] * LOG2E) * dist


def _swa_tile(sink_ref, q, k, v, kprev_ref, vprev_ref, bias_ref, first_tile, o_ref):
    blk = SWA_BLOCK
    hd = SWA_HEAD_DIM
    group = SWA_HEADS // SWA_KV_HEADS
    kvs = range(SWA_KV_HEADS)

    qi = lax.broadcasted_iota(jnp.int32, (blk, blk), 0)
    kj = lax.broadcasted_iota(jnp.int32, (blk, blk), 1)
    own = kj <= qi
    first_valid = (kj - qi) <= jnp.where(first_tile, 0, blk)
    lane = lax.broadcasted_iota(jnp.int32, (blk, LANES), 1)
    low = lane < hd

    def dup(x, kv):
        tile = x[:, (kv // 2) * LANES:(kv // 2 + 1) * LANES]
        rolled = pltpu.roll(tile, hd, 1)
        lo_half, hi_half = (tile, rolled) if kv % 2 == 0 else (rolled, tile)
        return jnp.where(low, lo_half, hi_half).astype(BF16)

    def dup_all(x):
        xf = x.astype(F32)
        return [dup(xf, kv) for kv in kvs]

    zero = jnp.zeros((blk, LANES), BF16)
    prev_k = [kprev_ref[kv] for kv in kvs]
    prev_v = [vprev_ref[kv] for kv in kvs]
    for bi in range(MIXER_TILE_M // blk):
        rows = slice(bi * blk, (bi + 1) * blk)
        cur_k = dup_all(k[rows, :])
        cur_v = dup_all(v[rows, :])
        for kv in kvs:
            lhs = []
            for pair in (2 * kv, 2 * kv + 1):
                x = q[rows, pair * LANES:(pair + 1) * LANES]
                lhs += [jnp.where(low, x, zero), jnp.where(low, zero, x)]
            keys = jnp.concatenate([prev_k[kv], cur_k[kv]], axis=0)
            s_all = _dot_t(jnp.concatenate(lhs, axis=0), keys)
            probs = []
            for gi in range(group):
                head = kv * group + gi
                sc = s_all[gi * blk:(gi + 1) * blk]
                s = jnp.where(own, sc[:, blk:], sc[:, :blk]) * (hd ** -0.5 * LOG2E) - bias_ref[head]
                if bi == 0:
                    s = jnp.where(first_valid, s, NEG_INF)
                sink = sink_ref[head] * LOG2E
                m = jnp.maximum(jnp.max(s, axis=-1, keepdims=True), sink)
                p = jnp.exp2(s - m)
                denom = jnp.sum(p, axis=-1, keepdims=True) + jnp.exp2(sink - m)
                pn = p * (1.0 / denom)
                probs.append(jnp.concatenate([jnp.where(own, 0.0, pn), jnp.where(own, pn, 0.0)],
                                             axis=1).astype(BF16))
            vals = jnp.concatenate([prev_v[kv], cur_v[kv]], axis=0)
            r = _dot(jnp.concatenate(probs, axis=0), vals)
            for pi, pair in enumerate((2 * kv, 2 * kv + 1)):
                r0 = r[(2 * pi) * blk:(2 * pi + 1) * blk]
                r1 = r[(2 * pi + 1) * blk:(2 * pi + 2) * blk]
                o_ref[rows, pair * LANES:(pair + 1) * LANES] = jnp.where(low, r0, r1).astype(BF16)
            yield
        prev_k, prev_v = cur_k, cur_v
    for kv in kvs:
        kprev_ref[kv] = prev_k[kv]
        vprev_ref[kv] = prev_v[kv]


def _row_views(w_ref, offsets, widths):
    return [w_ref.at[lo:lo + width, :] for lo, width in zip(offsets, widths)]


def _interleave(first, second, ratio):
    live_first, live_second = True, True
    while live_first or live_second:
        for _ in range(ratio):
            if live_first:
                live_first = next(first, "done") != "done"
        if live_second:
            live_second = next(second, "done") != "done"


def _mixer_kernel(slope_ref, sink_ref, x_ref, ada_ref,
                  w_in_t_ref,
                  wup_ref, bup_ref, seg_ref, mask_ref, tri_ref, gn_ref,
                  wa_ref, wb_ref, wo_ref, lng_ref, lnb_ref,
                  next_wg_ref, next_wu_ref, next_wd_ref,
                  out_ref, next_wg_out, next_wu_out, next_wd_out,
                  q_s, k_s, v_s, la_s, go_s, b_s, o_s, a_s, state_ref, state_new, kprev_s, vprev_s,
                  bias_ref):
    first_tile = pl.program_id(0) == 0
    for src, dst in ((next_wg_ref, next_wg_out), (next_wu_ref, next_wu_out), (next_wd_ref, next_wd_out)):
        dst[...] = src[...].astype(BF16)
    offsets = [sum(MIX_SIZES[:i]) for i in range(len(MIX_SIZES))]
    widths = list(MIX_SIZES)
    widths[3] = LR_PAD
    (wq_ref, wk_ref, wv_ref, wlr_ref, wgo_ref, wsq_ref, wsk_ref, wsv_ref, wga_ref,
     wgb_ref) = _row_views(w_in_t_ref, offsets, widths)

    def project(w_ref, lo=None):
        return _dot_t(h, w_ref[...] if lo is None else w_ref[lo:lo + step, :])

    @pl.when(first_tile)
    def _():
        state_ref[...] = jnp.zeros_like(state_ref)
        kprev_s[...] = jnp.zeros_like(kprev_s)
        vprev_s[...] = jnp.zeros_like(vprev_s)
        _swa_bias_table(slope_ref, bias_ref)

    x = x_ref[...]
    h = (x * (1.0 + _ada_slice(ada_ref, 4)) + _ada_slice(ada_ref, 3)).astype(BF16)
    step = MIXER_DOT_COLS
    col_blocks = range(0, D_MODEL, step)

    guard = {}
    lr = project(wlr_ref).astype(BF16)
    z = _dot(lr, wup_ref[...]) + bup_ref[...]
    la_s[...] = _log_sigmoid(z) * (1.0 / GLA_GATE_NORMALIZER)
    for _ in _gla_cumsum(la_s, tri_ref, b_s, guard):
        pass
    sq = _tie(project(wsq_ref).astype(BF16), b_s[...])
    sk = project(wsk_ref).astype(BF16)
    sv = project(wsv_ref).astype(BF16)

    gates = {}

    def gate(name, w_ref, lo):
        gates[name, lo] = _sigmoid(project(w_ref, lo))

    def projections_1():
        for lo in range(0, GLA_HEADS * GLA_DK, step):
            q_s[:, lo:lo + step] = project(wq_ref, lo) * (GLA_DK ** -0.5)
            yield
            k_s[:, lo:lo + step] = project(wk_ref, lo)
            yield
        for lo in col_blocks:
            v_s[:, lo:lo + step] = project(wv_ref, lo).astype(BF16)
            yield
        for lo in col_blocks:
            go = project(wgo_ref, lo)
            go_s[:, lo:lo + step] = go * _sigmoid(go)
            yield
        for lo in col_blocks:
            gate("a", wga_ref, lo)
            yield

    _interleave(_swa_tile(sink_ref, sq, sk, sv, kprev_s, vprev_s, bias_ref, first_tile, a_s),
                projections_1(), ratio=2)

    def projections_2():
        for lo in col_blocks:
            gate("b", wgb_ref, lo)
            yield
        for lo in col_blocks:
            gates["yb", lo] = _dot(a_s[...], wb_ref[:, lo:lo + step])
            yield

    _interleave(_gla_tile_fast(q_s, k_s, v_s, b_s, go_s, gn_ref, o_s, state_ref, state_new),
                projections_2(), ratio=2)
    safe = guard["safe"]

    @pl.when(jnp.logical_not(safe))
    def _():
        state_new[...] = state_ref[...]
        _gla_tile_any_decay(q_s, k_s, v_s, la_s, go_s, seg_ref, mask_ref, gn_ref, o_s, state_new)

    state_ref[...] = state_new[...]

    gate_a = jnp.concatenate([gates["a", lo] for lo in col_blocks], axis=1)
    gate_b_yb = jnp.concatenate([gates["b", lo] * gates["yb", lo] for lo in col_blocks], axis=1)
    rows_half = MIXER_TILE_M // 2
    for lo in (0, rows_half):
        rows = slice(lo, lo + rows_half)
        ya = _dot(o_s[rows, :], wa_ref[...])
        merged = (gate_a[rows, :] * ya + gate_b_yb[rows, :]).astype(BF16)
        y = _dot(merged, wo_ref[...])
        r = DEEPNORM_ALPHA * x[rows, :] + _ada_slice(ada_ref, 5) * y
        out_ref[rows, :] = _layer_norm(r, lng_ref[...], lnb_ref[...])


def _mixer(x, ada, w_in_t, wup, bup, gn, slopes, sinks, wa, wb, wo, ln_g, ln_b, next_ffn):
    tm = MIXER_TILE_M
    steps = SEQ // tm
    seg, mask, tri = _gla_constants()
    qk = GLA_HEADS * GLA_DK
    row = pl.BlockSpec((tm, D_MODEL), lambda i: (i, 0))
    smem = pl.BlockSpec(memory_space=pltpu.SMEM)
    up_rows = pl.BlockSpec((D_MODEL // steps, D_FF), lambda i: (i, 0))
    down_rows = pl.BlockSpec((2 * D_FF // steps, D_MODEL), lambda i: (i // 2, 0))

    def w(n):
        return _resident((D_MODEL, n))

    vec = _resident((1, D_MODEL))
    return pl.pallas_call(
        _mixer_kernel,
        grid=(steps,),
        in_specs=[smem, smem, row, _resident((1, N_ADA * D_MODEL)),
                  _resident((sum(MIX_SIZES), D_MODEL)),
                  _resident((LR_PAD, qk)), _resident((1, qk)),
                  _resident(seg.shape), _resident(mask.shape), _resident(tri.shape),
                  _resident((1, GLA_DV)),
                  w(D_MODEL), w(D_MODEL), w(D_MODEL), vec, vec,
                  up_rows, up_rows, down_rows],
        out_specs=[row, up_rows, up_rows, down_rows],
        out_shape=[jax.ShapeDtypeStruct((SEQ, D_MODEL), F32),
                   jax.ShapeDtypeStruct((D_MODEL, D_FF), BF16),
                   jax.ShapeDtypeStruct((D_MODEL, D_FF), BF16),
                   jax.ShapeDtypeStruct((D_FF, D_MODEL), BF16)],
        scratch_shapes=[pltpu.VMEM((tm, qk), F32), pltpu.VMEM((tm, qk), F32),
                        pltpu.VMEM((tm, D_MODEL), BF16), pltpu.VMEM((tm, qk), F32),
                        pltpu.VMEM((tm, D_MODEL), F32), pltpu.VMEM((tm, qk), F32),
                        pltpu.VMEM((tm, D_MODEL), BF16), pltpu.VMEM((tm, D_MODEL), BF16),
                        pltpu.VMEM((GLA_HEADS, GLA_DK, GLA_DV), F32),
                        pltpu.VMEM((GLA_HEADS, GLA_DK, GLA_DV), F32),
                        pltpu.VMEM((SWA_KV_HEADS, SWA_BLOCK, LANES), BF16),
                        pltpu.VMEM((SWA_KV_HEADS, SWA_BLOCK, LANES), BF16),
                        pltpu.VMEM((SWA_HEADS, SWA_BLOCK, SWA_BLOCK), F32)],
        compiler_params=_cparams(),
        name="mixer",
    )(slopes, sinks, x, ada, w_in_t, wup, bup,
      jnp.asarray(seg, BF16), jnp.asarray(mask, F32), jnp.asarray(tri, BF16), gn,
      wa, wb, wo, ln_g, ln_b, *next_ffn)


def _alibi_slopes(n):
    return 2.0 ** (-8.0 * jnp.arange(1, n + 1, dtype=jnp.float32) / n)


def kernel(x, c, w_ada, b_ada, ffn1_w_gate, ffn1_w_up, ffn1_w_down, ln1_g, ln1_b, w_in,
           w_gla_gate_up, b_gla_gate, gla_norm_g, w_branch_gla, swa_sinks, w_branch_swa, w_out,
           ln2_g, ln2_b, ffn2_w_gate, ffn2_w_up, ffn2_w_down, ln3_g, ln3_b):
    assert x.shape == (1, SEQ, D_MODEL) and w_ada.shape[0] == 1
    x2d = x.reshape(SEQ, D_MODEL)
    vec = lambda p: p.reshape(1, -1)

    ada = _ada(c, w_ada[0], b_ada[0])
    x1, w_in_t, wa, wb, wo = _ffn(
        x2d, ada, ffn1_w_gate[0].astype(BF16), ffn1_w_up[0].astype(BF16),
        ffn1_w_down[0].astype(BF16), vec(ln1_g), vec(ln1_b), 0,
        mixer_weights=(jnp.swapaxes(w_in[0], 0, 1), w_branch_gla[0], w_branch_swa[0], w_out[0]))

    wup = jnp.pad(w_gla_gate_up[0], ((0, LR_PAD - GLA_GATE_RANK), (0, 0))).astype(BF16)
    x2, wg2, wu2, wd2 = _mixer(
        x1, ada, w_in_t, wup, vec(b_gla_gate), vec(gla_norm_g),
        _alibi_slopes(SWA_HEADS), swa_sinks[0], wa, wb, wo,
        vec(ln2_g), vec(ln2_b), (ffn2_w_gate[0], ffn2_w_up[0], ffn2_w_down[0]))

    out = _ffn(x2, ada, wg2, wu2, wd2, vec(ln3_g), vec(ln3_b), 6)
    return out.reshape(1, SEQ, D_MODEL)
```

```python
import functools

import numpy as np
import jax
import jax.numpy as jnp
from jax import lax
from jax.experimental import pallas as pl
from jax.experimental.pallas import tpu as pltpu

D_MODEL = 1024
SEQ = 16384
D_FF = 2816
N_ADA = 9
LN_EPS = 1e-5
RMS_EPS = 1e-6
NEG_INF = -1e30
DEEPNORM_ALPHA = 2.0 ** 0.25
MACARON_WEIGHT = 0.5

GLA_HEADS = 4
GLA_DK = 128
GLA_DV = 256
GLA_GATE_RANK = 16
GLA_GATE_NORMALIZER = 16.0
GLA_CHUNK = 64
GLA_LEVELS = (32, 16, 8, 4, 2, 1)

SWA_HEADS = 16
SWA_KV_HEADS = 4
SWA_HEAD_DIM = 64
SWA_BLOCK = 128

MIX_SIZES = (512, 512, 1024, 16, 1024, 1024, 256, 256, 1024, 1024)

LANES = 128
BF16_ROW_TILE = 16
V7X_VMEM_LIMIT_BYTES = 60 * 1024 * 1024

BF16 = jnp.bfloat16
F32 = jnp.float32


def _cparams():
    return pltpu.CompilerParams(dimension_semantics=("arbitrary",),
                                vmem_limit_bytes=V7X_VMEM_LIMIT_BYTES)


def _resident(shape):
    return pl.BlockSpec(shape, lambda i: (0,) * len(shape), pipeline_mode=pl.Buffered(1))


def _round_up(n, multiple):
    return -(-n // multiple) * multiple


def _dot(a, b):
    return jnp.dot(a, b, preferred_element_type=F32)


def _dot_t(a, b):
    return lax.dot_general(a, b, (((1,), (1,)), ((), ())), preferred_element_type=F32)


def _sigmoid(x):
    return 1.0 / (1.0 + jnp.exp(-x))


def _layer_norm(r, g, b):
    mu = jnp.mean(r, axis=-1, keepdims=True)
    d = r - mu
    var = jnp.mean(d * d, axis=-1, keepdims=True)
    return d * lax.rsqrt(var + LN_EPS) * g + b


def _tie(value, other):
    bits = pltpu.bitcast(other, jnp.uint32)
    words = [bits[r:r + 8, c:c + LANES]
             for r in range(0, other.shape[0], 8) for c in range(0, other.shape[1], LANES)]
    while len(words) > 1:
        words = [a | b for a, b in zip(words[0::2], words[1::2])] + words[len(words) & ~1:]
    zero = lax.shift_right_logical(lax.shift_right_logical(words[0], jnp.uint32(16)), jnp.uint32(16))
    zero = jnp.concatenate([zero, zero], axis=0).astype(jnp.int32).astype(F32).astype(value.dtype)
    tile = value[0:BF16_ROW_TILE, 0:LANES] + zero
    top = jnp.concatenate([tile, value[0:BF16_ROW_TILE, LANES:]], axis=1)
    return jnp.concatenate([top, value[BF16_ROW_TILE:, :]], axis=0)


def _ada_slice(ada_ref, idx):
    return ada_ref[:, idx * D_MODEL:(idx + 1) * D_MODEL]


ADA_TILE_N = 1536


def _ada_kernel(c_ref, w_ref, b_ref, o_ref):
    c = c_ref[...]
    s = (c * _sigmoid(c)).astype(BF16)
    s8 = jnp.broadcast_to(s, (8, D_MODEL))
    y = _dot(s8, w_ref[...].astype(BF16))
    o_ref[...] = y[0:1, :] + b_ref[...]


def _ada(c, w_ada, b_ada):
    n = N_ADA * D_MODEL
    return pl.pallas_call(
        _ada_kernel,
        grid=(n // ADA_TILE_N,),
        in_specs=[pl.BlockSpec((1, D_MODEL), lambda i: (0, 0)),
                  pl.BlockSpec((D_MODEL, ADA_TILE_N), lambda i: (0, i)),
                  pl.BlockSpec((1, ADA_TILE_N), lambda i: (0, i))],
        out_specs=pl.BlockSpec((1, ADA_TILE_N), lambda i: (0, i)),
        out_shape=jax.ShapeDtypeStruct((1, n), F32),
        compiler_params=_cparams(),
        name="ada",
    )(c, w_ada, b_ada.reshape(1, n))


FFN_TILE_M = 512
FFN_SPLIT = 1536
FFN_UP_STAGE_ROWS = 128
FFN_DOWN_STAGE_ROWS = 352


def _load_cast_rows(src_hbm, dst_ref, stage_ref, sems, rows):
    n = src_hbm.shape[0] // rows

    def copy(c):
        return pltpu.make_async_copy(src_hbm.at[pl.ds(c * rows, rows), :], stage_ref.at[c % 2],
                                     sems.at[c % 2])

    copy(0).start()
    for c in range(n):
        if c + 1 < n:
            copy(c + 1).start()
        copy(c).wait()
        dst_ref[c * rows:(c + 1) * rows, :] = stage_ref[c % 2].astype(BF16)


def _ffn_kernel(x_ref, ada_ref, wg_ref, wu_ref, wd_ref, lng_ref, lnb_ref, *rest, ada_base,
                f32_weights):
    if f32_weights:
        *rest, wg_s, wu_s, wd_s, stage_up, stage_down, sem_up, sem_down = rest

        @pl.when(pl.program_id(0) == 0)
        def _():
            _load_cast_rows(wg_ref, wg_s, stage_up, sem_up, FFN_UP_STAGE_ROWS)
            _load_cast_rows(wu_ref, wu_s, stage_up, sem_up, FFN_UP_STAGE_ROWS)
            _load_cast_rows(wd_ref, wd_s, stage_down, sem_down, FFN_DOWN_STAGE_ROWS)

        wg_ref, wu_ref, wd_ref = wg_s, wu_s, wd_s
    n_cast = len(rest) // 2
    o_ref = rest[n_cast]
    for src, dst in zip(rest[:n_cast], rest[n_cast + 1:]):
        dst[...] = src[...].astype(BF16)
    x = x_ref[...]
    sh = _ada_slice(ada_ref, ada_base)
    sc = _ada_slice(ada_ref, ada_base + 1)
    gt = _ada_slice(ada_ref, ada_base + 2)
    h = (x * (1.0 + sc) + sh).astype(BF16)
    y = None
    for lo, hi in ((0, FFN_SPLIT), (FFN_SPLIT, D_FF)):
        g = _dot(h, wg_ref[:, lo:hi])
        u = _dot(h, wu_ref[:, lo:hi])
        a = (g * _sigmoid(g) * u).astype(BF16)
        part = _dot(a, wd_ref[lo:hi, :])
        y = part if y is None else y + part
    r = DEEPNORM_ALPHA * x + (MACARON_WEIGHT * gt) * y
    o_ref[...] = _layer_norm(r, lng_ref[...], lnb_ref[...])


def _ffn(x, ada, wg, wu, wd, ln_g, ln_b, ada_base, mixer_weights=None):
    tm = FFN_TILE_M
    steps = SEQ // tm
    row = pl.BlockSpec((tm, D_MODEL), lambda i: (i, 0))
    f32_weights = wg.dtype == F32
    scratch = []
    if f32_weights:
        weight_specs = [pl.BlockSpec(memory_space=pl.ANY)] * 3
        scratch = [pltpu.VMEM((D_MODEL, D_FF), BF16), pltpu.VMEM((D_MODEL, D_FF), BF16),
                   pltpu.VMEM((D_FF, D_MODEL), BF16),
                   pltpu.VMEM((2, FFN_UP_STAGE_ROWS, D_FF), F32),
                   pltpu.VMEM((2, FFN_DOWN_STAGE_ROWS, D_MODEL), F32),
                   pltpu.SemaphoreType.DMA((2,)), pltpu.SemaphoreType.DMA((2,))]
    else:
        weight_specs = [_resident((D_MODEL, D_FF)), _resident((D_MODEL, D_FF)),
                        _resident((D_FF, D_MODEL))]
    in_specs = [row, _resident((1, N_ADA * D_MODEL)), *weight_specs,
                _resident((1, D_MODEL)), _resident((1, D_MODEL))]
    out_specs = [row]
    out_shape = [jax.ShapeDtypeStruct((SEQ, D_MODEL), F32)]
    operands = [x, ada, wg, wu, wd, ln_g, ln_b]
    for w in mixer_weights or ():
        rows = _round_up(pl.cdiv(w.shape[0], steps), BF16_ROW_TILE)
        last = pl.cdiv(w.shape[0], rows) - 1
        spec = pl.BlockSpec((rows, D_MODEL), lambda i, last=last: (jnp.minimum(i, last), 0))
        in_specs.append(spec)
        operands.append(w)
        out_specs.append(spec)
        out_shape.append(jax.ShapeDtypeStruct(w.shape, BF16))
    outs = pl.pallas_call(
        functools.partial(_ffn_kernel, ada_base=ada_base, f32_weights=f32_weights),
        grid=(steps,),
        in_specs=in_specs,
        out_specs=out_specs,
        out_shape=out_shape,
        scratch_shapes=scratch,
        compiler_params=_cparams(),
        name="ffn",
    )(*operands)
    return outs[0] if mixer_weights is None else outs


MIXER_TILE_M = 512
MIXER_DOT_COLS = 512
LR_PAD = LANES
GLA_NSEG = 1 + len(GLA_LEVELS)
GLA_SPLIT = 2
GLA_FAST_CHUNK = 128
GLA_SAFE_LOG_DECAY = -60.0


def _log_sigmoid(z):
    return jnp.minimum(z, 0.0) - jnp.log(1.0 + jnp.exp(-jnp.abs(z)))


def _gla_constants():
    c = GLA_CHUNK
    t = np.arange(c)[:, None]
    u = np.arange(c)[None, :]
    seg = [(u <= t)]
    mask = [(u == t)]
    for m in GLA_LEVELS:
        blk_t, blk_u = t // m, u // m
        odd = (blk_t % 2) == 1
        p_odd = blk_t * m
        p_even = (blk_t + 1) * m
        seg.append(np.where(odd, (u > p_odd) & (u <= t), (u > t) & (u <= p_even)))
        mask.append(odd & (blk_u == blk_t - 1))
    seg = np.concatenate(seg, axis=0).astype(np.float32)
    seg = np.concatenate([seg] * GLA_SPLIT, axis=1)
    mask = np.stack(mask).astype(np.float32)
    fc = GLA_FAST_CHUNK
    tri = np.tril(np.ones((fc, fc), np.float32))
    tri = np.concatenate([tri] * GLA_SPLIT, axis=1)
    return seg, mask, tri


def _split_terms(g):
    parts = []
    rem = g
    for _ in range(GLA_SPLIT):
        p = rem.astype(BF16)
        parts.append(p)
        rem = rem - p.astype(F32)
    return jnp.concatenate(parts, axis=0)


def _decay_column(bl):
    return jnp.exp(jnp.broadcast_to(bl, (8, GLA_DK))).T[:, 0:1]


def _gla_finish(o, gn_ref, go, o_ref, rows, vs):
    ms = jnp.mean(o * o, axis=-1, keepdims=True)
    on = o * lax.rsqrt(ms + RMS_EPS) * gn_ref[...]
    o_ref[rows, vs] = (on * go).astype(BF16)


def _gla_cumsum(la_ref, tri_ref, b_ref, result):
    c = GLA_FAST_CHUNK
    lowest = None
    for ci in range(MIXER_TILE_M // c):
        rows = slice(ci * c, (ci + 1) * c)
        b = _dot(tri_ref[...], _split_terms(la_ref[rows, :]))
        b_ref[rows, :] = b
        total = jnp.min(b[c - 1:c])
        lowest = total if lowest is None else jnp.minimum(lowest, total)
        result["safe"] = lowest >= GLA_SAFE_LOG_DECAY
        yield


def _gla_tile_fast(q_ref, k_ref, v_ref, b_ref, go_ref, gn_ref, o_ref, state_in_ref, state_out_ref):
    c = GLA_FAST_CHUNK
    ti = lax.broadcasted_iota(jnp.int32, (c, c), 0)
    si = lax.broadcasted_iota(jnp.int32, (c, c), 1)
    causal = si <= ti
    chunks = range(MIXER_TILE_M // c)
    heads = range(GLA_HEADS)
    rows = [slice(ci * c, (ci + 1) * c) for ci in chunks]
    kcols = [slice(hd * GLA_DK, (hd + 1) * GLA_DK) for hd in heads]
    vcols = [slice(hd * GLA_DV, (hd + 1) * GLA_DV) for hd in heads]

    q_in, a, upd, d_col = {}, {}, {}, {}
    for ci in chunks:
        b = b_ref[rows[ci], :]
        b_last = b[c - 1:c]
        e_pos = jnp.exp(b)
        e_neg = jnp.exp(-b)
        e_last = jnp.exp(b_last)
        for hd in heads:
            ks = kcols[hd]
            q_in[ci, hd] = (q_ref[rows[ci], ks] * e_pos[:, ks]).astype(BF16)
            k_out = k_ref[rows[ci], ks] * e_neg[:, ks]
            a[ci, hd] = jnp.where(causal, _dot_t(q_in[ci, hd], k_out.astype(BF16)), 0.0).astype(BF16)
            k_dec_t = (k_out * e_last[:, ks]).T.astype(BF16)
            upd[ci, hd] = _dot(k_dec_t, v_ref[rows[ci], vcols[hd]])
            d_col[ci, hd] = _decay_column(b_last[:, ks])
        yield

    for hd in heads:
        state = state_in_ref[hd]
        for ci in chunks:
            v = v_ref[rows[ci], vcols[hd]]
            o = _dot(jnp.concatenate([q_in[ci, hd], a[ci, hd]], axis=1),
                     jnp.concatenate([state.astype(BF16), v], axis=0))
            state = d_col[ci, hd] * state + upd[ci, hd]
            _gla_finish(o, gn_ref, go_ref[rows[ci], vcols[hd]], o_ref, rows[ci], vcols[hd])
        state_out_ref[hd] = state
        yield


def _gla_tile_any_decay(q_ref, k_ref, v_ref, la_ref, go_ref, seg_ref, mask_ref, gn_ref, o_ref,
                        state_ref):
    c = GLA_CHUNK

    def chunk(ci, carry):
        rows = pl.ds(pl.multiple_of(ci * c, c), c)
        e_all = _dot(seg_ref[...], _split_terms(la_ref[rows, :]))
        b = e_all[0:c]
        b_last = b[c - 1:c]
        for hd in range(GLA_HEADS):
            ks = slice(hd * GLA_DK, (hd + 1) * GLA_DK)
            vs = slice(hd * GLA_DV, (hd + 1) * GLA_DV)
            q = q_ref[rows, ks]
            k = k_ref[rows, ks]
            v = v_ref[rows, vs]
            bh = b[:, ks]
            bl = b_last[:, ks]
            state = state_ref[hd]
            o = _dot((q * jnp.exp(bh)).astype(BF16), state.astype(BF16))
            a = _dot_t(q.astype(BF16), k.astype(BF16)) * mask_ref[0]
            for li in range(len(GLA_LEVELS)):
                e = jnp.exp(e_all[(li + 1) * c:(li + 2) * c, ks])
                a = a + _dot_t((q * e).astype(BF16), (k * e).astype(BF16)) * mask_ref[li + 1]
            o = o + _dot(a.astype(BF16), v)
            k_dec = k * jnp.exp(bl - bh)
            state_ref[hd] = _decay_column(bl) * state + _dot(k_dec.T.astype(BF16), v)
            _gla_finish(o, gn_ref, go_ref[rows, vs], o_ref, rows, vs)
        return carry

    lax.fori_loop(0, MIXER_TILE_M // c, chunk, 0)


LOG2E = 1.4426950408889634


def _swa_bias_table(slope_ref, bias_ref):
    blk = SWA_BLOCK
    qi = lax.broadcasted_iota(jnp.int32, (blk, blk), 0)
    kj = lax.broadcasted_iota(jnp.int32, (blk, blk), 1)
    dist = jnp.where(kj <= qi, qi - kj, qi + blk - kj).astype(F32)
    for head in range(SWA_HEADS):
        bias_ref[head] = (slope_ref[head] * LOG2E) * dist


def _swa_tile(sink_ref, q, k, v, kprev_ref, vprev_ref, bias_ref, first_tile, o_ref):
    blk = SWA_BLOCK
    hd = SWA_HEAD_DIM
    group = SWA_HEADS // SWA_KV_HEADS
    kvs = range(SWA_KV_HEADS)

    qi = lax.broadcasted_iota(jnp.int32, (blk, blk), 0)
    kj = lax.broadcasted_iota(jnp.int32, (blk, blk), 1)
    own = kj <= qi
    first_valid = (kj - qi) <= jnp.where(first_tile, 0, blk)
    lane = lax.broadcasted_iota(jnp.int32, (blk, LANES), 1)
    low = lane < hd

    def dup(x, kv):
        tile = x[:, (kv // 2) * LANES:(kv // 2 + 1) * LANES]
        rolled = pltpu.roll(tile, hd, 1)
        lo_half, hi_half = (tile, rolled) if kv % 2 == 0 else (rolled, tile)
        return jnp.where(low, lo_half, hi_half).astype(BF16)

    def dup_all(x):
        xf = x.astype(F32)
        return [dup(xf, kv) for kv in kvs]

    zero = jnp.zeros((blk, LANES), BF16)
    prev_k = [kprev_ref[kv] for kv in kvs]
    prev_v = [vprev_ref[kv] for kv in kvs]
    for bi in range(MIXER_TILE_M // blk):
        rows = slice(bi * blk, (bi + 1) * blk)
        cur_k = dup_all(k[rows, :])
        cur_v = dup_all(v[rows, :])
        for kv in kvs:
            lhs = []
            for pair in (2 * kv, 2 * kv + 1):
                x = q[rows, pair * LANES:(pair + 1) * LANES]
                lhs += [jnp.where(low, x, zero), jnp.where(low, zero, x)]
            keys = jnp.concatenate([prev_k[kv], cur_k[kv]], axis=0)
            s_all = _dot_t(jnp.concatenate(lhs, axis=0), keys)
            probs = []
            for gi in range(group):
                head = kv * group + gi
                sc = s_all[gi * blk:(gi + 1) * blk]
                s = jnp.where(own, sc[:, blk:], sc[:, :blk]) * (hd ** -0.5 * LOG2E) - bias_ref[head]
                if bi == 0:
                    s = jnp.where(first_valid, s, NEG_INF)
                sink = sink_ref[head] * LOG2E
                m = jnp.maximum(jnp.max(s, axis=-1, keepdims=True), sink)
                p = jnp.exp2(s - m)
                denom = jnp.sum(p, axis=-1, keepdims=True) + jnp.exp2(sink - m)
                pn = p * (1.0 / denom)
                probs.append(jnp.concatenate([jnp.where(own, 0.0, pn), jnp.where(own, pn, 0.0)],
                                             axis=1).astype(BF16))
            vals = jnp.concatenate([prev_v[kv], cur_v[kv]], axis=0)
            r = _dot(jnp.concatenate(probs, axis=0), vals)
            for pi, pair in enumerate((2 * kv, 2 * kv + 1)):
                r0 = r[(2 * pi) * blk:(2 * pi + 1) * blk]
                r1 = r[(2 * pi + 1) * blk:(2 * pi + 2) * blk]
                o_ref[rows, pair * LANES:(pair + 1) * LANES] = jnp.where(low, r0, r1).astype(BF16)
            yield
        prev_k, prev_v = cur_k, cur_v
    for kv in kvs:
        kprev_ref[kv] = prev_k[kv]
        vprev_ref[kv] = prev_v[kv]


def _row_views(w_ref, offsets, widths):
    return [w_ref.at[lo:lo + width, :] for lo, width in zip(offsets, widths)]


def _interleave(first, second, ratio):
    live_first, live_second = True, True
    while live_first or live_second:
        for _ in range(ratio):
            if live_first:
                live_first = next(first, "done") != "done"
        if live_second:
            live_second = next(second, "done") != "done"


def _mixer_kernel(slope_ref, sink_ref, x_ref, ada_ref,
                  w_in_t_ref,
                  wup_ref, bup_ref, seg_ref, mask_ref, tri_ref, gn_ref,
                  wa_ref, wb_ref, wo_ref, lng_ref, lnb_ref,
                  next_wg_ref, next_wu_ref, next_wd_ref,
                  out_ref, next_wg_out, next_wu_out, next_wd_out,
                  q_s, k_s, v_s, la_s, go_s, b_s, o_s, a_s, state_ref, state_new, kprev_s, vprev_s,
                  bias_ref):
    first_tile = pl.program_id(0) == 0
    for src, dst in ((next_wg_ref, next_wg_out), (next_wu_ref, next_wu_out), (next_wd_ref, next_wd_out)):
        dst[...] = src[...].astype(BF16)
    offsets = [sum(MIX_SIZES[:i]) for i in range(len(MIX_SIZES))]
    widths = list(MIX_SIZES)
    widths[3] = LR_PAD
    (wq_ref, wk_ref, wv_ref, wlr_ref, wgo_ref, wsq_ref, wsk_ref, wsv_ref, wga_ref,
     wgb_ref) = _row_views(w_in_t_ref, offsets, widths)

    def project(w_ref, lo=None):
        return _dot_t(h, w_ref[...] if lo is None else w_ref[lo:lo + step, :])

    @pl.when(first_tile)
    def _():
        state_ref[...] = jnp.zeros_like(state_ref)
        kprev_s[...] = jnp.zeros_like(kprev_s)
        vprev_s[...] = jnp.zeros_like(vprev_s)
        _swa_bias_table(slope_ref, bias_ref)

    x = x_ref[...]
    h = (x * (1.0 + _ada_slice(ada_ref, 4)) + _ada_slice(ada_ref, 3)).astype(BF16)
    step = MIXER_DOT_COLS
    col_blocks = range(0, D_MODEL, step)

    guard = {}
    lr = project(wlr_ref).astype(BF16)
    z = _dot(lr, wup_ref[...]) + bup_ref[...]
    la_s[...] = _log_sigmoid(z) * (1.0 / GLA_GATE_NORMALIZER)
    for _ in _gla_cumsum(la_s, tri_ref, b_s, guard):
        pass
    sq = _tie(project(wsq_ref).astype(BF16), b_s[...])
    sk = project(wsk_ref).astype(BF16)
    sv = project(wsv_ref).astype(BF16)

    gates = {}

    def gate(name, w_ref, lo):
        gates[name, lo] = _sigmoid(project(w_ref, lo))

    def projections_1():
        for lo in range(0, GLA_HEADS * GLA_DK, step):
            q_s[:, lo:lo + step] = project(wq_ref, lo) * (GLA_DK ** -0.5)
            yield
            k_s[:, lo:lo + step] = project(wk_ref, lo)
            yield
        for lo in col_blocks:
            v_s[:, lo:lo + step] = project(wv_ref, lo).astype(BF16)
            yield
        for lo in col_blocks:
            go = project(wgo_ref, lo)
            go_s[:, lo:lo + step] = go * _sigmoid(go)
            yield
        for lo in col_blocks:
            gate("a", wga_ref, lo)
            yield

    _interleave(_swa_tile(sink_ref, sq, sk, sv, kprev_s, vprev_s, bias_ref, first_tile, a_s),
                projections_1(), ratio=2)

    def projections_2():
        for lo in col_blocks:
            gate("b", wgb_ref, lo)
            yield
        for lo in col_blocks:
            gates["yb", lo] = _dot(a_s[...], wb_ref[:, lo:lo + step])
            yield

    _interleave(_gla_tile_fast(q_s, k_s, v_s, b_s, go_s, gn_ref, o_s, state_ref, state_new),
                projections_2(), ratio=2)
    safe = guard["safe"]

    @pl.when(jnp.logical_not(safe))
    def _():
        state_new[...] = state_ref[...]
        _gla_tile_any_decay(q_s, k_s, v_s, la_s, go_s, seg_ref, mask_ref, gn_ref, o_s, state_new)

    state_ref[...] = state_new[...]

    gate_a = jnp.concatenate([gates["a", lo] for lo in col_blocks], axis=1)
    gate_b_yb = jnp.concatenate([gates["b", lo] * gates["yb", lo] for lo in col_blocks], axis=1)
    rows_half = MIXER_TILE_M // 2
    for lo in (0, rows_half):
        rows = slice(lo, lo + rows_half)
        ya = _dot(o_s[rows, :], wa_ref[...])
        merged = (gate_a[rows, :] * ya + gate_b_yb[rows, :]).astype(BF16)
        y = _dot(merged, wo_ref[...])
        r = DEEPNORM_ALPHA * x[rows, :] + _ada_slice(ada_ref, 5) * y
        out_ref[rows, :] = _layer_norm(r, lng_ref[...], lnb_ref[...])


def _mixer(x, ada, w_in_t, wup, bup, gn, slopes, sinks, wa, wb, wo, ln_g, ln_b, next_ffn):
    tm = MIXER_TILE_M
    steps = SEQ // tm
    seg, mask, tri = _gla_constants()
    qk = GLA_HEADS * GLA_DK
    row = pl.BlockSpec((tm, D_MODEL), lambda i: (i, 0))
    smem = pl.BlockSpec(memory_space=pltpu.SMEM)
    up_rows = pl.BlockSpec((D_MODEL // steps, D_FF), lambda i: (i, 0))
    down_rows = pl.BlockSpec((2 * D_FF // steps, D_MODEL), lambda i: (i // 2, 0))

    def w(n):
        return _resident((D_MODEL, n))

    vec = _resident((1, D_MODEL))
    return pl.pallas_call(
        _mixer_kernel,
        grid=(steps,),
        in_specs=[smem, smem, row, _resident((1, N_ADA * D_MODEL)),
                  _resident((sum(MIX_SIZES), D_MODEL)),
                  _resident((LR_PAD, qk)), _resident((1, qk)),
                  _resident(seg.shape), _resident(mask.shape), _resident(tri.shape),
                  _resident((1, GLA_DV)),
                  w(D_MODEL), w(D_MODEL), w(D_MODEL), vec, vec,
                  up_rows, up_rows, down_rows],
        out_specs=[row, up_rows, up_rows, down_rows],
        out_shape=[jax.ShapeDtypeStruct((SEQ, D_MODEL), F32),
                   jax.ShapeDtypeStruct((D_MODEL, D_FF), BF16),
                   jax.ShapeDtypeStruct((D_MODEL, D_FF), BF16),
                   jax.ShapeDtypeStruct((D_FF, D_MODEL), BF16)],
        scratch_shapes=[pltpu.VMEM((tm, qk), F32), pltpu.VMEM((tm, qk), F32),
                        pltpu.VMEM((tm, D_MODEL), BF16), pltpu.VMEM((tm, qk), F32),
                        pltpu.VMEM((tm, D_MODEL), F32), pltpu.VMEM((tm, qk), F32),
                        pltpu.VMEM((tm, D_MODEL), BF16), pltpu.VMEM((tm, D_MODEL), BF16),
                        pltpu.VMEM((GLA_HEADS, GLA_DK, GLA_DV), F32),
                        pltpu.VMEM((GLA_HEADS, GLA_DK, GLA_DV), F32),
                        pltpu.VMEM((SWA_KV_HEADS, SWA_BLOCK, LANES), BF16),
                        pltpu.VMEM((SWA_KV_HEADS, SWA_BLOCK, LANES), BF16),
                        pltpu.VMEM((SWA_HEADS, SWA_BLOCK, SWA_BLOCK), F32)],
        compiler_params=_cparams(),
        name="mixer",
    )(slopes, sinks, x, ada, w_in_t, wup, bup,
      jnp.asarray(seg, BF16), jnp.asarray(mask, F32), jnp.asarray(tri, BF16), gn,
      wa, wb, wo, ln_g, ln_b, *next_ffn)


def _alibi_slopes(n):
    return 2.0 ** (-8.0 * jnp.arange(1, n + 1, dtype=jnp.float32) / n)


def kernel(x, c, w_ada, b_ada, ffn1_w_gate, ffn1_w_up, ffn1_w_down, ln1_g, ln1_b, w_in,
           w_gla_gate_up, b_gla_gate, gla_norm_g, w_branch_gla, swa_sinks, w_branch_swa, w_out,
           ln2_g, ln2_b, ffn2_w_gate, ffn2_w_up, ffn2_w_down, ln3_g, ln3_b):
    assert x.shape == (1, SEQ, D_MODEL) and w_ada.shape[0] == 1
    x2d = x.reshape(SEQ, D_MODEL)
    vec = lambda p: p.reshape(1, -1)

    ada = _ada(c, w_ada[0], b_ada[0])
    x1, w_in_t, wa, wb, wo = _ffn(
        x2d, ada, ffn1_w_gate[0], ffn1_w_up[0], ffn1_w_down[0], vec(ln1_g), vec(ln1_b), 0,
        mixer_weights=(jnp.swapaxes(w_in[0], 0, 1), w_branch_gla[0], w_branch_swa[0], w_out[0]))

    wup = jnp.pad(w_gla_gate_up[0], ((0, LR_PAD - GLA_GATE_RANK), (0, 0))).astype(BF16)
    x2, wg2, wu2, wd2 = _mixer(
        x1, ada, w_in_t, wup, vec(b_gla_gate), vec(gla_norm_g),
        _alibi_slopes(SWA_HEADS), swa_sinks[0], wa, wb, wo,
        vec(ln2_g), vec(ln2_b), (ffn2_w_gate[0], ffn2_w_up[0], ffn2_w_down[0]))

    out = _ffn(x2, ada, wg2, wu2, wd2, vec(ln3_g), vec(ln3_b), 6)
    return out.reshape(1, SEQ, D_MODEL)
```

```python
import functools

import numpy as np
import jax
import jax.numpy as jnp
from jax import lax
from jax.experimental import pallas as pl
from jax.experimental.pallas import tpu as pltpu

D_MODEL = 1024
SEQ = 16384
D_FF = 2816
N_ADA = 9
LN_EPS = 1e-5
RMS_EPS = 1e-6
NEG_INF = -1e30
DEEPNORM_ALPHA = 2.0 ** 0.25
MACARON_WEIGHT = 0.5

GLA_HEADS = 4
GLA_DK = 128
GLA_DV = 256
GLA_GATE_RANK = 16
GLA_GATE_NORMALIZER = 16.0
GLA_CHUNK = 64
GLA_LEVELS = (32, 16, 8, 4, 2, 1)

SWA_HEADS = 16
SWA_KV_HEADS = 4
SWA_HEAD_DIM = 64
SWA_BLOCK = 128

MIX_SIZES = (512, 512, 1024, 16, 1024, 1024, 256, 256, 1024, 1024)

LANES = 128
BF16_ROW_TILE = 16
V7X_VMEM_LIMIT_BYTES = 60 * 1024 * 1024

BF16 = jnp.bfloat16
F32 = jnp.float32


def _cparams():
    return pltpu.CompilerParams(dimension_semantics=("arbitrary",),
                                vmem_limit_bytes=V7X_VMEM_LIMIT_BYTES)


def _resident(shape):
    return pl.BlockSpec(shape, lambda i: (0,) * len(shape), pipeline_mode=pl.Buffered(1))


def _round_up(n, multiple):
    return -(-n // multiple) * multiple


def _dot(a, b):
    return jnp.dot(a, b, preferred_element_type=F32)


def _dot_t(a, b):
    return lax.dot_general(a, b, (((1,), (1,)), ((), ())), preferred_element_type=F32)


def _sigmoid(x):
    return 1.0 / (1.0 + jnp.exp(-x))


def _layer_norm(r, g, b):
    mu = jnp.mean(r, axis=-1, keepdims=True)
    d = r - mu
    var = jnp.mean(d * d, axis=-1, keepdims=True)
    return d * lax.rsqrt(var + LN_EPS) * g + b


def _tie(value, other):
    bits = pltpu.bitcast(other, jnp.uint32)
    words = [bits[r:r + 8, c:c + LANES]
             for r in range(0, other.shape[0], 8) for c in range(0, other.shape[1], LANES)]
    while len(words) > 1:
        words = [a | b for a, b in zip(words[0::2], words[1::2])] + words[len(words) & ~1:]
    zero = lax.shift_right_logical(lax.shift_right_logical(words[0], jnp.uint32(16)), jnp.uint32(16))
    zero = jnp.concatenate([zero, zero], axis=0).astype(jnp.int32).astype(F32).astype(value.dtype)
    tile = value[0:BF16_ROW_TILE, 0:LANES] + zero
    top = jnp.concatenate([tile, value[0:BF16_ROW_TILE, LANES:]], axis=1)
    return jnp.concatenate([top, value[BF16_ROW_TILE:, :]], axis=0)


def _ada_slice(ada_ref, idx):
    return ada_ref[:, idx * D_MODEL:(idx + 1) * D_MODEL]


ADA_TILE_N = 1536


def _ada_kernel(c_ref, w_ref, b_ref, o_ref):
    c = c_ref[...]
    s = (c * _sigmoid(c)).astype(BF16)
    s8 = jnp.broadcast_to(s, (8, D_MODEL))
    y = _dot(s8, w_ref[...].astype(BF16))
    o_ref[...] = y[0:1, :] + b_ref[...]


def _ada(c, w_ada, b_ada):
    n = N_ADA * D_MODEL
    return pl.pallas_call(
        _ada_kernel,
        grid=(n // ADA_TILE_N,),
        in_specs=[pl.BlockSpec((1, D_MODEL), lambda i: (0, 0)),
                  pl.BlockSpec((D_MODEL, ADA_TILE_N), lambda i: (0, i)),
                  pl.BlockSpec((1, ADA_TILE_N), lambda i: (0, i))],
        out_specs=pl.BlockSpec((1, ADA_TILE_N), lambda i: (0, i)),
        out_shape=jax.ShapeDtypeStruct((1, n), F32),
        compiler_params=_cparams(),
        name="ada",
    )(c, w_ada, b_ada.reshape(1, n))


FFN_TILE_M = 512
FFN_SPLIT = 1536
FFN_UP_STAGE_ROWS = 128
FFN_DOWN_STAGE_ROWS = 352
FFN_STAGE_SLOTS = 4


def _load_cast_rows(src_hbm, dst_ref, stage_ref, sems, rows):
    n = src_hbm.shape[0] // rows
    slots = stage_ref.shape[0]

    def copy(c):
        return pltpu.make_async_copy(src_hbm.at[pl.ds(c * rows, rows), :], stage_ref.at[c % slots],
                                     sems.at[c % slots])

    for c in range(min(slots - 1, n)):
        copy(c).start()
    for c in range(n):
        if c + slots - 1 < n:
            copy(c + slots - 1).start()
        copy(c).wait()
        dst_ref[c * rows:(c + 1) * rows, :] = stage_ref[c % slots].astype(BF16)


def _ffn_kernel(x_ref, ada_ref, wg_ref, wu_ref, wd_ref, lng_ref, lnb_ref, *rest, ada_base,
                f32_weights):
    if f32_weights:
        *rest, wg_s, wu_s, wd_s, stage_up, stage_down, sem_up, sem_down = rest

        @pl.when(pl.program_id(0) == 0)
        def _():
            _load_cast_rows(wg_ref, wg_s, stage_up, sem_up, FFN_UP_STAGE_ROWS)
            _load_cast_rows(wu_ref, wu_s, stage_up, sem_up, FFN_UP_STAGE_ROWS)
            _load_cast_rows(wd_ref, wd_s, stage_down, sem_down, FFN_DOWN_STAGE_ROWS)

        wg_ref, wu_ref, wd_ref = wg_s, wu_s, wd_s
    n_cast = len(rest) // 2
    o_ref = rest[n_cast]
    for src, dst in zip(rest[:n_cast], rest[n_cast + 1:]):
        dst[...] = src[...].astype(BF16)
    x = x_ref[...]
    sh = _ada_slice(ada_ref, ada_base)
    sc = _ada_slice(ada_ref, ada_base + 1)
    gt = _ada_slice(ada_ref, ada_base + 2)
    h = (x * (1.0 + sc) + sh).astype(BF16)
    y = None
    for lo, hi in ((0, FFN_SPLIT), (FFN_SPLIT, D_FF)):
        g = _dot(h, wg_ref[:, lo:hi])
        u = _dot(h, wu_ref[:, lo:hi])
        a = (g * _sigmoid(g) * u).astype(BF16)
        part = _dot(a, wd_ref[lo:hi, :])
        y = part if y is None else y + part
    r = DEEPNORM_ALPHA * x + (MACARON_WEIGHT * gt) * y
    o_ref[...] = _layer_norm(r, lng_ref[...], lnb_ref[...])


def _ffn(x, ada, wg, wu, wd, ln_g, ln_b, ada_base, mixer_weights=None):
    tm = FFN_TILE_M
    steps = SEQ // tm
    row = pl.BlockSpec((tm, D_MODEL), lambda i: (i, 0))
    f32_weights = wg.dtype == F32
    scratch = []
    if f32_weights:
        weight_specs = [pl.BlockSpec(memory_space=pl.ANY)] * 3
        scratch = [pltpu.VMEM((D_MODEL, D_FF), BF16), pltpu.VMEM((D_MODEL, D_FF), BF16),
                   pltpu.VMEM((D_FF, D_MODEL), BF16),
                   pltpu.VMEM((FFN_STAGE_SLOTS, FFN_UP_STAGE_ROWS, D_FF), F32),
                   pltpu.VMEM((FFN_STAGE_SLOTS, FFN_DOWN_STAGE_ROWS, D_MODEL), F32),
                   pltpu.SemaphoreType.DMA((FFN_STAGE_SLOTS,)),
                   pltpu.SemaphoreType.DMA((FFN_STAGE_SLOTS,))]
    else:
        weight_specs = [_resident((D_MODEL, D_FF)), _resident((D_MODEL, D_FF)),
                        _resident((D_FF, D_MODEL))]
    in_specs = [row, _resident((1, N_ADA * D_MODEL)), *weight_specs,
                _resident((1, D_MODEL)), _resident((1, D_MODEL))]
    out_specs = [row]
    out_shape = [jax.ShapeDtypeStruct((SEQ, D_MODEL), F32)]
    operands = [x, ada, wg, wu, wd, ln_g, ln_b]
    for w in mixer_weights or ():
        rows = _round_up(pl.cdiv(w.shape[0], steps), BF16_ROW_TILE)
        last = pl.cdiv(w.shape[0], rows) - 1
        spec = pl.BlockSpec((rows, D_MODEL), lambda i, last=last: (jnp.minimum(i, last), 0))
        in_specs.append(spec)
        operands.append(w)
        out_specs.append(spec)
        out_shape.append(jax.ShapeDtypeStruct(w.shape, BF16))
    outs = pl.pallas_call(
        functools.partial(_ffn_kernel, ada_base=ada_base, f32_weights=f32_weights),
        grid=(steps,),
        in_specs=in_specs,
        out_specs=out_specs,
        out_shape=out_shape,
        scratch_shapes=scratch,
        compiler_params=_cparams(),
        name="ffn",
    )(*operands)
    return outs[0] if mixer_weights is None else outs


MIXER_TILE_M = 512
MIXER_DOT_COLS = 512
LR_PAD = LANES
GLA_NSEG = 1 + len(GLA_LEVELS)
GLA_SPLIT = 2
GLA_FAST_CHUNK = 128
GLA_SAFE_LOG_DECAY = -60.0


def _log_sigmoid(z):
    return jnp.minimum(z, 0.0) - jnp.log(1.0 + jnp.exp(-jnp.abs(z)))


def _gla_constants():
    c = GLA_CHUNK
    t = np.arange(c)[:, None]
    u = np.arange(c)[None, :]
    seg = [(u <= t)]
    mask = [(u == t)]
    for m in GLA_LEVELS:
        blk_t, blk_u = t // m, u // m
        odd = (blk_t % 2) == 1
        p_odd = blk_t * m
        p_even = (blk_t + 1) * m
        seg.append(np.where(odd, (u > p_odd) & (u <= t), (u > t) & (u <= p_even)))
        mask.append(odd & (blk_u == blk_t - 1))
    seg = np.concatenate(seg, axis=0).astype(np.float32)
    seg = np.concatenate([seg] * GLA_SPLIT, axis=1)
    mask = np.stack(mask).astype(np.float32)
    fc = GLA_FAST_CHUNK
    tri = np.tril(np.ones((fc, fc), np.float32))
    tri = np.concatenate([tri] * GLA_SPLIT, axis=1)
    return seg, mask, tri


def _split_terms(g):
    parts = []
    rem = g
    for _ in range(GLA_SPLIT):
        p = rem.astype(BF16)
        parts.append(p)
        rem = rem - p.astype(F32)
    return jnp.concatenate(parts, axis=0)


def _decay_column(bl):
    return jnp.exp(jnp.broadcast_to(bl, (8, GLA_DK))).T[:, 0:1]


def _gla_finish(o, gn_ref, go, o_ref, rows, vs):
    ms = jnp.mean(o * o, axis=-1, keepdims=True)
    on = o * lax.rsqrt(ms + RMS_EPS) * gn_ref[...]
    o_ref[rows, vs] = (on * go).astype(BF16)


def _gla_cumsum(la_ref, tri_ref, b_ref, result):
    c = GLA_FAST_CHUNK
    lowest = None
    for ci in range(MIXER_TILE_M // c):
        rows = slice(ci * c, (ci + 1) * c)
        b = _dot(tri_ref[...], _split_terms(la_ref[rows, :]))
        b_ref[rows, :] = b
        total = jnp.min(b[c - 1:c])
        lowest = total if lowest is None else jnp.minimum(lowest, total)
        result["safe"] = lowest >= GLA_SAFE_LOG_DECAY
        yield


def _gla_tile_fast(q_ref, k_ref, v_ref, b_ref, go_ref, gn_ref, o_ref, state_in_ref, state_out_ref):
    c = GLA_FAST_CHUNK
    ti = lax.broadcasted_iota(jnp.int32, (c, c), 0)
    si = lax.broadcasted_iota(jnp.int32, (c, c), 1)
    causal = si <= ti
    chunks = range(MIXER_TILE_M // c)
    heads = range(GLA_HEADS)
    rows = [slice(ci * c, (ci + 1) * c) for ci in chunks]
    kcols = [slice(hd * GLA_DK, (hd + 1) * GLA_DK) for hd in heads]
    vcols = [slice(hd * GLA_DV, (hd + 1) * GLA_DV) for hd in heads]

    q_in, a, upd, d_col = {}, {}, {}, {}
    for ci in chunks:
        b = b_ref[rows[ci], :]
        b_last = b[c - 1:c]
        e_pos = jnp.exp(b)
        e_neg = jnp.exp(-b)
        e_last = jnp.exp(b_last)
        for hd in heads:
            ks = kcols[hd]
            q_in[ci, hd] = (q_ref[rows[ci], ks] * e_pos[:, ks]).astype(BF16)
            k_out = k_ref[rows[ci], ks] * e_neg[:, ks]
            a[ci, hd] = jnp.where(causal, _dot_t(q_in[ci, hd], k_out.astype(BF16)), 0.0).astype(BF16)
            k_dec_t = (k_out * e_last[:, ks]).T.astype(BF16)
            upd[ci, hd] = _dot(k_dec_t, v_ref[rows[ci], vcols[hd]])
            d_col[ci, hd] = _decay_column(b_last[:, ks])
        yield

    for hd in heads:
        state = state_in_ref[hd]
        for ci in chunks:
            v = v_ref[rows[ci], vcols[hd]]
            o = _dot(jnp.concatenate([q_in[ci, hd], a[ci, hd]], axis=1),
                     jnp.concatenate([state.astype(BF16), v], axis=0))
            state = d_col[ci, hd] * state + upd[ci, hd]
            _gla_finish(o, gn_ref, go_ref[rows[ci], vcols[hd]], o_ref, rows[ci], vcols[hd])
        state_out_ref[hd] = state
        yield


def _gla_tile_any_decay(q_ref, k_ref, v_ref, la_ref, go_ref, seg_ref, mask_ref, gn_ref, o_ref,
                        state_ref):
    c = GLA_CHUNK

    def chunk(ci, carry):
        rows = pl.ds(pl.multiple_of(ci * c, c), c)
        e_all = _dot(seg_ref[...], _split_terms(la_ref[rows, :]))
        b = e_all[0:c]
        b_last = b[c - 1:c]
        for hd in range(GLA_HEADS):
            ks = slice(hd * GLA_DK, (hd + 1) * GLA_DK)
            vs = slice(hd * GLA_DV, (hd + 1) * GLA_DV)
            q = q_ref[rows, ks]
            k = k_ref[rows, ks]
            v = v_ref[rows, vs]
            bh = b[:, ks]
            bl = b_last[:, ks]
            state = state_ref[hd]
            o = _dot((q * jnp.exp(bh)).astype(BF16), state.astype(BF16))
            a = _dot_t(q.astype(BF16), k.astype(BF16)) * mask_ref[0]
            for li in range(len(GLA_LEVELS)):
                e = jnp.exp(e_all[(li + 1) * c:(li + 2) * c, ks])
                a = a + _dot_t((q * e).astype(BF16), (k * e).astype(BF16)) * mask_ref[li + 1]
            o = o + _dot(a.astype(BF16), v)
            k_dec = k * jnp.exp(bl - bh)
            state_ref[hd] = _decay_column(bl) * state + _dot(k_dec.T.astype(BF16), v)
            _gla_finish(o, gn_ref, go_ref[rows, vs], o_ref, rows, vs)
        return carry

    lax.fori_loop(0, MIXER_TILE_M // c, chunk, 0)


LOG2E = 1.4426950408889634


def _swa_bias_table(slope_ref, bias_ref):
    blk = SWA_BLOCK
    qi = lax.broadcasted_iota(jnp.int32, (blk, blk), 0)
    kj = lax.broadcasted_iota(jnp.int32, (blk, blk), 1)
    dist = jnp.where(kj <= qi, qi - kj, qi + blk - kj).astype(F32)
    for head in range(SWA_HEADS):
        bias_ref[head] = (slope_ref[head] * LOG2E) * dist


def _swa_tile(sink_ref, q, k, v, kprev_ref, vprev_ref, bias_ref, first_tile, o_ref):
    blk = SWA_BLOCK
    hd = SWA_HEAD_DIM
    group = SWA_HEADS // SWA_KV_HEADS
    kvs = range(SWA_KV_HEADS)

    qi = lax.broadcasted_iota(jnp.int32, (blk, blk), 0)
    kj = lax.broadcasted_iota(jnp.int32, (blk, blk), 1)
    own = kj <= qi
    first_valid = (kj - qi) <= jnp.where(first_tile, 0, blk)
    lane = lax.broadcasted_iota(jnp.int32, (blk, LANES), 1)
    low = lane < hd

    def dup(x, kv):
        tile = x[:, (kv // 2) * LANES:(kv // 2 + 1) * LANES]
        rolled = pltpu.roll(tile, hd, 1)
        lo_half, hi_half = (tile, rolled) if kv % 2 == 0 else (rolled, tile)
        return jnp.where(low, lo_half, hi_half).astype(BF16)

    def dup_all(x):
        xf = x.astype(F32)
        return [dup(xf, kv) for kv in kvs]

    zero = jnp.zeros((blk, LANES), BF16)
    prev_k = [kprev_ref[kv] for kv in kvs]
    prev_v = [vprev_ref[kv] for kv in kvs]
    for bi in range(MIXER_TILE_M // blk):
        rows = slice(bi * blk, (bi + 1) * blk)
        cur_k = dup_all(k[rows, :])
        cur_v = dup_all(v[rows, :])
        for kv in kvs:
            lhs = []
            for pair in (2 * kv, 2 * kv + 1):
                x = q[rows, pair * LANES:(pair + 1) * LANES]
                lhs += [jnp.where(low, x, zero), jnp.where(low, zero, x)]
            keys = jnp.concatenate([prev_k[kv], cur_k[kv]], axis=0)
            s_all = _dot_t(jnp.concatenate(lhs, axis=0), keys)
            probs = []
            for gi in range(group):
                head = kv * group + gi
                sc = s_all[gi * blk:(gi + 1) * blk]
                s = jnp.where(own, sc[:, blk:], sc[:, :blk]) * (hd ** -0.5 * LOG2E) - bias_ref[head]
                if bi == 0:
                    s = jnp.where(first_valid, s, NEG_INF)
                sink = sink_ref[head] * LOG2E
                m = jnp.maximum(jnp.max(s, axis=-1, keepdims=True), sink)
                p = jnp.exp2(s - m)
                denom = jnp.sum(p, axis=-1, keepdims=True) + jnp.exp2(sink - m)
                pn = p * (1.0 / denom)
                probs.append(jnp.concatenate([jnp.where(own, 0.0, pn), jnp.where(own, pn, 0.0)],
                                             axis=1).astype(BF16))
            vals = jnp.concatenate([prev_v[kv], cur_v[kv]], axis=0)
            r = _dot(jnp.concatenate(probs, axis=0), vals)
            for pi, pair in enumerate((2 * kv, 2 * kv + 1)):
                r0 = r[(2 * pi) * blk:(2 * pi + 1) * blk]
                r1 = r[(2 * pi + 1) * blk:(2 * pi + 2) * blk]
                o_ref[rows, pair * LANES:(pair + 1) * LANES] = jnp.where(low, r0, r1).astype(BF16)
            yield
        prev_k, prev_v = cur_k, cur_v
    for kv in kvs:
        kprev_ref[kv] = prev_k[kv]
        vprev_ref[kv] = prev_v[kv]


def _row_views(w_ref, offsets, widths):
    return [w_ref.at[lo:lo + width, :] for lo, width in zip(offsets, widths)]


def _interleave(first, second, ratio):
    live_first, live_second = True, True
    while live_first or live_second:
        for _ in range(ratio):
            if live_first:
                live_first = next(first, "done") != "done"
        if live_second:
            live_second = next(second, "done") != "done"


def _mixer_kernel(slope_ref, sink_ref, x_ref, ada_ref,
                  w_in_t_ref,
                  wup_ref, bup_ref, seg_ref, mask_ref, tri_ref, gn_ref,
                  wa_ref, wb_ref, wo_ref, lng_ref, lnb_ref,
                  next_wg_ref, next_wu_ref, next_wd_ref,
                  out_ref, next_wg_out, next_wu_out, next_wd_out,
                  q_s, k_s, v_s, la_s, go_s, b_s, o_s, a_s, state_ref, state_new, kprev_s, vprev_s,
                  bias_ref):
    first_tile = pl.program_id(0) == 0
    for src, dst in ((next_wg_ref, next_wg_out), (next_wu_ref, next_wu_out), (next_wd_ref, next_wd_out)):
        dst[...] = src[...].astype(BF16)
    offsets = [sum(MIX_SIZES[:i]) for i in range(len(MIX_SIZES))]
    widths = list(MIX_SIZES)
    widths[3] = LR_PAD
    (wq_ref, wk_ref, wv_ref, wlr_ref, wgo_ref, wsq_ref, wsk_ref, wsv_ref, wga_ref,
     wgb_ref) = _row_views(w_in_t_ref, offsets, widths)

    def project(w_ref, lo=None):
        return _dot_t(h, w_ref[...] if lo is None else w_ref[lo:lo + step, :])

    @pl.when(first_tile)
    def _():
        state_ref[...] = jnp.zeros_like(state_ref)
        kprev_s[...] = jnp.zeros_like(kprev_s)
        vprev_s[...] = jnp.zeros_like(vprev_s)
        _swa_bias_table(slope_ref, bias_ref)

    x = x_ref[...]
    h = (x * (1.0 + _ada_slice(ada_ref, 4)) + _ada_slice(ada_ref, 3)).astype(BF16)
    step = MIXER_DOT_COLS
    col_blocks = range(0, D_MODEL, step)

    guard = {}
    lr = project(wlr_ref).astype(BF16)
    z = _dot(lr, wup_ref[...]) + bup_ref[...]
    la_s[...] = _log_sigmoid(z) * (1.0 / GLA_GATE_NORMALIZER)
    for _ in _gla_cumsum(la_s, tri_ref, b_s, guard):
        pass
    sq = _tie(project(wsq_ref).astype(BF16), b_s[...])
    sk = project(wsk_ref).astype(BF16)
    sv = project(wsv_ref).astype(BF16)

    gates = {}

    def gate(name, w_ref, lo):
        gates[name, lo] = _sigmoid(project(w_ref, lo))

    def projections_1():
        for lo in range(0, GLA_HEADS * GLA_DK, step):
            q_s[:, lo:lo + step] = project(wq_ref, lo) * (GLA_DK ** -0.5)
            yield
            k_s[:, lo:lo + step] = project(wk_ref, lo)
            yield
        for lo in col_blocks:
            v_s[:, lo:lo + step] = project(wv_ref, lo).astype(BF16)
            yield
        for lo in col_blocks:
            go = project(wgo_ref, lo)
            go_s[:, lo:lo + step] = go * _sigmoid(go)
            yield
        for lo in col_blocks:
            gate("a", wga_ref, lo)
            yield

    _interleave(_swa_tile(sink_ref, sq, sk, sv, kprev_s, vprev_s, bias_ref, first_tile, a_s),
                projections_1(), ratio=2)

    def projections_2():
        for lo in col_blocks:
            gate("b", wgb_ref, lo)
            yield
        for lo in col_blocks:
            gates["yb", lo] = _dot(a_s[...], wb_ref[:, lo:lo + step])
            yield

    _interleave(_gla_tile_fast(q_s, k_s, v_s, b_s, go_s, gn_ref, o_s, state_ref, state_new),
                projections_2(), ratio=2)
    safe = guard["safe"]

    @pl.when(jnp.logical_not(safe))
    def _():
        state_new[...] = state_ref[...]
        _gla_tile_any_decay(q_s, k_s, v_s, la_s, go_s, seg_ref, mask_ref, gn_ref, o_s, state_new)

    state_ref[...] = state_new[...]

    gate_a = jnp.concatenate([gates["a", lo] for lo in col_blocks], axis=1)
    gate_b_yb = jnp.concatenate([gates["b", lo] * gates["yb", lo] for lo in col_blocks], axis=1)
    rows_half = MIXER_TILE_M // 2
    for lo in (0, rows_half):
        rows = slice(lo, lo + rows_half)
        ya = _dot(o_s[rows, :], wa_ref[...])
        merged = (gate_a[rows, :] * ya + gate_b_yb[rows, :]).astype(BF16)
        y = _dot(merged, wo_ref[...])
        r = DEEPNORM_ALPHA * x[rows, :] + _ada_slice(ada_ref, 5) * y
        out_ref[rows, :] = _layer_norm(r, lng_ref[...], lnb_ref[...])


def _mixer(x, ada, w_in_t, wup, bup, gn, slopes, sinks, wa, wb, wo, ln_g, ln_b, next_ffn):
    tm = MIXER_TILE_M
    steps = SEQ // tm
    seg, mask, tri = _gla_constants()
    qk = GLA_HEADS * GLA_DK
    row = pl.BlockSpec((tm, D_MODEL), lambda i: (i, 0))
    smem = pl.BlockSpec(memory_space=pltpu.SMEM)
    up_rows = pl.BlockSpec((D_MODEL // steps, D_FF), lambda i: (i, 0))
    down_rows = pl.BlockSpec((2 * D_FF // steps, D_MODEL), lambda i: (i // 2, 0))

    def w(n):
        return _resident((D_MODEL, n))

    vec = _resident((1, D_MODEL))
    return pl.pallas_call(
        _mixer_kernel,
        grid=(steps,),
        in_specs=[smem, smem, row, _resident((1, N_ADA * D_MODEL)),
                  _resident((sum(MIX_SIZES), D_MODEL)),
                  _resident((LR_PAD, qk)), _resident((1, qk)),
                  _resident(seg.shape), _resident(mask.shape), _resident(tri.shape),
                  _resident((1, GLA_DV)),
                  w(D_MODEL), w(D_MODEL), w(D_MODEL), vec, vec,
                  up_rows, up_rows, down_rows],
        out_specs=[row, up_rows, up_rows, down_rows],
        out_shape=[jax.ShapeDtypeStruct((SEQ, D_MODEL), F32),
                   jax.ShapeDtypeStruct((D_MODEL, D_FF), BF16),
                   jax.ShapeDtypeStruct((D_MODEL, D_FF), BF16),
                   jax.ShapeDtypeStruct((D_FF, D_MODEL), BF16)],
        scratch_shapes=[pltpu.VMEM((tm, qk), F32), pltpu.VMEM((tm, qk), F32),
                        pltpu.VMEM((tm, D_MODEL), BF16), pltpu.VMEM((tm, qk), F32),
                        pltpu.VMEM((tm, D_MODEL), F32), pltpu.VMEM((tm, qk), F32),
                        pltpu.VMEM((tm, D_MODEL), BF16), pltpu.VMEM((tm, D_MODEL), BF16),
                        pltpu.VMEM((GLA_HEADS, GLA_DK, GLA_DV), F32),
                        pltpu.VMEM((GLA_HEADS, GLA_DK, GLA_DV), F32),
                        pltpu.VMEM((SWA_KV_HEADS, SWA_BLOCK, LANES), BF16),
                        pltpu.VMEM((SWA_KV_HEADS, SWA_BLOCK, LANES), BF16),
                        pltpu.VMEM((SWA_HEADS, SWA_BLOCK, SWA_BLOCK), F32)],
        compiler_params=_cparams(),
        name="mixer",
    )(slopes, sinks, x, ada, w_in_t, wup, bup,
      jnp.asarray(seg, BF16), jnp.asarray(mask, F32), jnp.asarray(tri, BF16), gn,
      wa, wb, wo, ln_g, ln_b, *next_ffn)


def _alibi_slopes(n):
    return 2.0 ** (-8.0 * jnp.arange(1, n + 1, dtype=jnp.float32) / n)


def kernel(x, c, w_ada, b_ada, ffn1_w_gate, ffn1_w_up, ffn1_w_down, ln1_g, ln1_b, w_in,
           w_gla_gate_up, b_gla_gate, gla_norm_g, w_branch_gla, swa_sinks, w_branch_swa, w_out,
           ln2_g, ln2_b, ffn2_w_gate, ffn2_w_up, ffn2_w_down, ln3_g, ln3_b):
    assert x.shape == (1, SEQ, D_MODEL) and w_ada.shape[0] == 1
    x2d = x.reshape(SEQ, D_MODEL)
    vec = lambda p: p.reshape(1, -1)

    ada = _ada(c, w_ada[0], b_ada[0])
    x1, w_in_t, wa, wb, wo = _ffn(
        x2d, ada, ffn1_w_gate[0], ffn1_w_up[0], ffn1_w_down[0], vec(ln1_g), vec(ln1_b), 0,
        mixer_weights=(jnp.swapaxes(w_in[0], 0, 1), w_branch_gla[0], w_branch_swa[0], w_out[0]))

    wup = jnp.pad(w_gla_gate_up[0], ((0, LR_PAD - GLA_GATE_RANK), (0, 0))).astype(BF16)
    x2, wg2, wu2, wd2 = _mixer(
        x1, ada, w_in_t, wup, vec(b_gla_gate), vec(gla_norm_g),
        _alibi_slopes(SWA_HEADS), swa_sinks[0], wa, wb, wo,
        vec(ln2_g), vec(ln2_b), (ffn2_w_gate[0], ffn2_w_up[0], ffn2_w_down[0]))

    out = _ffn(x2, ada, wg2, wu2, wd2, vec(ln3_g), vec(ln3_b), 6)
    return out.reshape(1, SEQ, D_MODEL)
```

```python
import functools

import numpy as np
import jax
import jax.numpy as jnp
from jax import lax
from jax.experimental import pallas as pl
from jax.experimental.pallas import tpu as pltpu

D_MODEL = 1024
SEQ = 16384
D_FF = 2816
N_ADA = 9
LN_EPS = 1e-5
RMS_EPS = 1e-6
NEG_INF = -1e30
DEEPNORM_ALPHA = 2.0 ** 0.25
MACARON_WEIGHT = 0.5

GLA_HEADS = 4
GLA_DK = 128
GLA_DV = 256
GLA_GATE_RANK = 16
GLA_GATE_NORMALIZER = 16.0
GLA_CHUNK = 64
GLA_LEVELS = (32, 16, 8, 4, 2, 1)

SWA_HEADS = 16
SWA_KV_HEADS = 4
SWA_HEAD_DIM = 64
SWA_BLOCK = 128

MIX_SIZES = (512, 512, 1024, 16, 1024, 1024, 256, 256, 1024, 1024)

LANES = 128
BF16_ROW_TILE = 16
V7X_VMEM_LIMIT_BYTES = 60 * 1024 * 1024

BF16 = jnp.bfloat16
F32 = jnp.float32


def _cparams():
    return pltpu.CompilerParams(dimension_semantics=("arbitrary",),
                                vmem_limit_bytes=V7X_VMEM_LIMIT_BYTES)


def _resident(shape):
    return pl.BlockSpec(shape, lambda i: (0,) * len(shape), pipeline_mode=pl.Buffered(1))


def _round_up(n, multiple):
    return -(-n // multiple) * multiple


def _dot(a, b):
    return jnp.dot(a, b, preferred_element_type=F32)


def _dot_t(a, b):
    return lax.dot_general(a, b, (((1,), (1,)), ((), ())), preferred_element_type=F32)


def _sigmoid(x):
    return 1.0 / (1.0 + jnp.exp(-x))


def _layer_norm(r, g, b):
    mu = jnp.mean(r, axis=-1, keepdims=True)
    d = r - mu
    var = jnp.mean(d * d, axis=-1, keepdims=True)
    return d * lax.rsqrt(var + LN_EPS) * g + b


def _tie(value, other):
    bits = pltpu.bitcast(other, jnp.uint32)
    words = [bits[r:r + 8, c:c + LANES]
             for r in range(0, other.shape[0], 8) for c in range(0, other.shape[1], LANES)]
    while len(words) > 1:
        words = [a | b for a, b in zip(words[0::2], words[1::2])] + words[len(words) & ~1:]
    zero = lax.shift_right_logical(lax.shift_right_logical(words[0], jnp.uint32(16)), jnp.uint32(16))
    zero = jnp.concatenate([zero, zero], axis=0).astype(jnp.int32).astype(F32).astype(value.dtype)
    tile = value[0:BF16_ROW_TILE, 0:LANES] + zero
    top = jnp.concatenate([tile, value[0:BF16_ROW_TILE, LANES:]], axis=1)
    return jnp.concatenate([top, value[BF16_ROW_TILE:, :]], axis=0)


def _ada_slice(ada_ref, idx):
    return ada_ref[:, idx * D_MODEL:(idx + 1) * D_MODEL]


ADA_TILE_N = 1536


def _ada_kernel(c_ref, w_ref, b_ref, o_ref):
    c = c_ref[...]
    s = (c * _sigmoid(c)).astype(BF16)
    s8 = jnp.broadcast_to(s, (8, D_MODEL))
    y = _dot(s8, w_ref[...].astype(BF16))
    o_ref[...] = y[0:1, :] + b_ref[...]


def _ada(c, w_ada, b_ada):
    n = N_ADA * D_MODEL
    return pl.pallas_call(
        _ada_kernel,
        grid=(n // ADA_TILE_N,),
        in_specs=[pl.BlockSpec((1, D_MODEL), lambda i: (0, 0)),
                  pl.BlockSpec((D_MODEL, ADA_TILE_N), lambda i: (0, i)),
                  pl.BlockSpec((1, ADA_TILE_N), lambda i: (0, i))],
        out_specs=pl.BlockSpec((1, ADA_TILE_N), lambda i: (0, i)),
        out_shape=jax.ShapeDtypeStruct((1, n), F32),
        compiler_params=_cparams(),
        name="ada",
    )(c, w_ada, b_ada.reshape(1, n))


FFN_TILE_M = 512
FFN_SPLIT = 1536
FFN_UP_STAGE_ROWS = 128
FFN_DOWN_STAGE_ROWS = 352
FFN_STAGE_SLOTS = 4


def _load_cast_rows(src_hbm, dst_ref, stage_ref, sems, rows):
    n = src_hbm.shape[0] // rows
    slots = stage_ref.shape[0]

    def copy(c):
        return pltpu.make_async_copy(src_hbm.at[pl.ds(c * rows, rows), :], stage_ref.at[c % slots],
                                     sems.at[c % slots])

    for c in range(min(slots - 1, n)):
        copy(c).start()
    for c in range(n):
        if c + slots - 1 < n:
            copy(c + slots - 1).start()
        copy(c).wait()
        dst_ref[c * rows:(c + 1) * rows, :] = stage_ref[c % slots].astype(BF16)


def _ffn_kernel(x_ref, ada_ref, wg_ref, wu_ref, wd_ref, lng_ref, lnb_ref, *rest, ada_base,
                f32_weights):
    if f32_weights:
        *rest, wg_s, wu_s, wd_s, stage_up, stage_down, sem_up, sem_down = rest

        @pl.when(pl.program_id(0) == 0)
        def _():
            _load_cast_rows(wg_ref, wg_s, stage_up, sem_up, FFN_UP_STAGE_ROWS)
            _load_cast_rows(wu_ref, wu_s, stage_up, sem_up, FFN_UP_STAGE_ROWS)
            _load_cast_rows(wd_ref, wd_s, stage_down, sem_down, FFN_DOWN_STAGE_ROWS)

        wg_ref, wu_ref, wd_ref = wg_s, wu_s, wd_s
    n_cast = len(rest) // 2
    o_ref = rest[n_cast]
    for src, dst in zip(rest[:n_cast], rest[n_cast + 1:]):
        dst[...] = src[...].astype(BF16)
    x = x_ref[...]
    sh = _ada_slice(ada_ref, ada_base)
    sc = _ada_slice(ada_ref, ada_base + 1)
    gt = _ada_slice(ada_ref, ada_base + 2)
    h = (x * (1.0 + sc) + sh).astype(BF16)
    y = None
    for lo, hi in ((0, FFN_SPLIT), (FFN_SPLIT, D_FF)):
        g = _dot(h, wg_ref[:, lo:hi])
        u = _dot(h, wu_ref[:, lo:hi])
        a = (g * _sigmoid(g) * u).astype(BF16)
        part = _dot(a, wd_ref[lo:hi, :])
        y = part if y is None else y + part
    r = DEEPNORM_ALPHA * x + (MACARON_WEIGHT * gt) * y
    o_ref[...] = _layer_norm(r, lng_ref[...], lnb_ref[...])


def _ffn(x, ada, wg, wu, wd, ln_g, ln_b, ada_base, mixer_weights=None):
    tm = FFN_TILE_M
    steps = SEQ // tm
    row = pl.BlockSpec((tm, D_MODEL), lambda i: (i, 0))
    f32_weights = wg.dtype == F32
    scratch = []
    if f32_weights:
        weight_specs = [pl.BlockSpec(memory_space=pl.ANY)] * 3
        scratch = [pltpu.VMEM((D_MODEL, D_FF), BF16), pltpu.VMEM((D_MODEL, D_FF), BF16),
                   pltpu.VMEM((D_FF, D_MODEL), BF16),
                   pltpu.VMEM((FFN_STAGE_SLOTS, FFN_UP_STAGE_ROWS, D_FF), F32),
                   pltpu.VMEM((FFN_STAGE_SLOTS, FFN_DOWN_STAGE_ROWS, D_MODEL), F32),
                   pltpu.SemaphoreType.DMA((FFN_STAGE_SLOTS,)),
                   pltpu.SemaphoreType.DMA((FFN_STAGE_SLOTS,))]
    else:
        weight_specs = [_resident((D_MODEL, D_FF)), _resident((D_MODEL, D_FF)),
                        _resident((D_FF, D_MODEL))]
    in_specs = [row, _resident((1, N_ADA * D_MODEL)), *weight_specs,
                _resident((1, D_MODEL)), _resident((1, D_MODEL))]
    out_specs = [row]
    out_shape = [jax.ShapeDtypeStruct((SEQ, D_MODEL), F32)]
    operands = [x, ada, wg, wu, wd, ln_g, ln_b]
    for w in mixer_weights or ():
        rows = _round_up(pl.cdiv(w.shape[0], steps), BF16_ROW_TILE)
        last = pl.cdiv(w.shape[0], rows) - 1
        spec = pl.BlockSpec((rows, D_MODEL), lambda i, last=last: (jnp.minimum(i, last), 0))
        in_specs.append(spec)
        operands.append(w)
        out_specs.append(spec)
        out_shape.append(jax.ShapeDtypeStruct(w.shape, BF16))
    outs = pl.pallas_call(
        functools.partial(_ffn_kernel, ada_base=ada_base, f32_weights=f32_weights),
        grid=(steps,),
        in_specs=in_specs,
        out_specs=out_specs,
        out_shape=out_shape,
        scratch_shapes=scratch,
        compiler_params=_cparams(),
        name="ffn",
    )(*operands)
    return outs[0] if mixer_weights is None else outs


MIXER_TILE_M = 512
MIXER_DOT_COLS = 512
LR_PAD = LANES
GLA_NSEG = 1 + len(GLA_LEVELS)
GLA_SPLIT = 2
GLA_FAST_CHUNK = 128
GLA_SAFE_LOG_DECAY = -60.0


def _log_sigmoid(z):
    return jnp.minimum(z, 0.0) - jnp.log(1.0 + jnp.exp(-jnp.abs(z)))


def _gla_constants():
    c = GLA_CHUNK
    t = np.arange(c)[:, None]
    u = np.arange(c)[None, :]
    seg = [(u <= t)]
    mask = [(u == t)]
    for m in GLA_LEVELS:
        blk_t, blk_u = t // m, u // m
        odd = (blk_t % 2) == 1
        p_odd = blk_t * m
        p_even = (blk_t + 1) * m
        seg.append(np.where(odd, (u > p_odd) & (u <= t), (u > t) & (u <= p_even)))
        mask.append(odd & (blk_u == blk_t - 1))
    seg = np.concatenate(seg, axis=0).astype(np.float32)
    seg = np.concatenate([seg] * GLA_SPLIT, axis=1)
    mask = np.stack(mask).astype(np.float32)
    fc = GLA_FAST_CHUNK
    tri = np.tril(np.ones((fc, fc), np.float32))
    tri = np.concatenate([tri] * GLA_SPLIT, axis=1)
    return seg, mask, tri


def _split_terms(g):
    parts = []
    rem = g
    for _ in range(GLA_SPLIT):
        p = rem.astype(BF16)
        parts.append(p)
        rem = rem - p.astype(F32)
    return jnp.concatenate(parts, axis=0)


def _decay_column(bl):
    return jnp.exp(jnp.broadcast_to(bl, (8, GLA_DK))).T[:, 0:1]


def _gla_finish(o, gn_ref, go, o_ref, rows, vs):
    ms = jnp.mean(o * o, axis=-1, keepdims=True)
    on = o * lax.rsqrt(ms + RMS_EPS) * gn_ref[...]
    o_ref[rows, vs] = (on * go).astype(BF16)


def _gla_cumsum(la_ref, tri_ref, b_ref, result):
    c = GLA_FAST_CHUNK
    lowest = None
    for ci in range(MIXER_TILE_M // c):
        rows = slice(ci * c, (ci + 1) * c)
        b = _dot(tri_ref[...], _split_terms(la_ref[rows, :]))
        b_ref[rows, :] = b
        total = jnp.min(b[c - 1:c])
        lowest = total if lowest is None else jnp.minimum(lowest, total)
        result["safe"] = lowest >= GLA_SAFE_LOG_DECAY
        yield


def _gla_tile_fast(q_ref, k_ref, v_ref, b_ref, go_ref, gn_ref, o_ref, state_in_ref, state_out_ref):
    c = GLA_FAST_CHUNK
    ti = lax.broadcasted_iota(jnp.int32, (c, c), 0)
    si = lax.broadcasted_iota(jnp.int32, (c, c), 1)
    causal = si <= ti
    chunks = range(MIXER_TILE_M // c)
    heads = range(GLA_HEADS)
    rows = [slice(ci * c, (ci + 1) * c) for ci in chunks]
    kcols = [slice(hd * GLA_DK, (hd + 1) * GLA_DK) for hd in heads]
    vcols = [slice(hd * GLA_DV, (hd + 1) * GLA_DV) for hd in heads]

    q_in, a, upd, d_col = {}, {}, {}, {}
    for ci in chunks:
        b = b_ref[rows[ci], :]
        b_last = b[c - 1:c]
        e_pos = jnp.exp(b)
        e_neg = jnp.exp(-b)
        e_last = jnp.exp(b_last)
        for hd in heads:
            ks = kcols[hd]
            q_in[ci, hd] = (q_ref[rows[ci], ks] * e_pos[:, ks]).astype(BF16)
            k_out = k_ref[rows[ci], ks] * e_neg[:, ks]
            a[ci, hd] = jnp.where(causal, _dot_t(q_in[ci, hd], k_out.astype(BF16)), 0.0).astype(BF16)
            k_dec_t = (k_out * e_last[:, ks]).T.astype(BF16)
            upd[ci, hd] = _dot(k_dec_t, v_ref[rows[ci], vcols[hd]])
            d_col[ci, hd] = _decay_column(b_last[:, ks])
        yield

    for hd in heads:
        state = state_in_ref[hd]
        for ci in chunks:
            v = v_ref[rows[ci], vcols[hd]]
            o = _dot(jnp.concatenate([q_in[ci, hd], a[ci, hd]], axis=1),
                     jnp.concatenate([state.astype(BF16), v], axis=0))
            state = d_col[ci, hd] * state + upd[ci, hd]
            _gla_finish(o, gn_ref, go_ref[rows[ci], vcols[hd]], o_ref, rows[ci], vcols[hd])
        state_out_ref[hd] = state
        yield


def _gla_tile_any_decay(q_ref, k_ref, v_ref, la_ref, go_ref, seg_ref, mask_ref, gn_ref, o_ref,
                        state_ref):
    c = GLA_CHUNK

    def chunk(ci, carry):
        rows = pl.ds(pl.multiple_of(ci * c, c), c)
        e_all = _dot(seg_ref[...], _split_terms(la_ref[rows, :]))
        b = e_all[0:c]
        b_last = b[c - 1:c]
        for hd in range(GLA_HEADS):
            ks = slice(hd * GLA_DK, (hd + 1) * GLA_DK)
            vs = slice(hd * GLA_DV, (hd + 1) * GLA_DV)
            q = q_ref[rows, ks]
            k = k_ref[rows, ks]
            v = v_ref[rows, vs]
            bh = b[:, ks]
            bl = b_last[:, ks]
            state = state_ref[hd]
            o = _dot((q * jnp.exp(bh)).astype(BF16), state.astype(BF16))
            a = _dot_t(q.astype(BF16), k.astype(BF16)) * mask_ref[0]
            for li in range(len(GLA_LEVELS)):
                e = jnp.exp(e_all[(li + 1) * c:(li + 2) * c, ks])
                a = a + _dot_t((q * e).astype(BF16), (k * e).astype(BF16)) * mask_ref[li + 1]
            o = o + _dot(a.astype(BF16), v)
            k_dec = k * jnp.exp(bl - bh)
            state_ref[hd] = _decay_column(bl) * state + _dot(k_dec.T.astype(BF16), v)
            _gla_finish(o, gn_ref, go_ref[rows, vs], o_ref, rows, vs)
        return carry

    lax.fori_loop(0, MIXER_TILE_M // c, chunk, 0)


LOG2E = 1.4426950408889634


def _swa_bias_table(slope_ref, bias_ref):
    blk = SWA_BLOCK
    qi = lax.broadcasted_iota(jnp.int32, (blk, blk), 0)
    kj = lax.broadcasted_iota(jnp.int32, (blk, blk), 1)
    dist = jnp.where(kj <= qi, qi - kj, qi + blk - kj).astype(F32)
    for head in range(SWA_HEADS):
        bias_ref[head] = (slope_ref[head] * LOG2E) * dist


def _swa_tile(sink_ref, q, k, v, kprev_ref, vprev_ref, bias_ref, first_tile, o_ref):
    blk = SWA_BLOCK
    hd = SWA_HEAD_DIM
    group = SWA_HEADS // SWA_KV_HEADS
    kvs = range(SWA_KV_HEADS)

    qi = lax.broadcasted_iota(jnp.int32, (blk, blk), 0)
    kj = lax.broadcasted_iota(jnp.int32, (blk, blk), 1)
    own = kj <= qi
    first_valid = (kj - qi) <= jnp.where(first_tile, 0, blk)
    lane = lax.broadcasted_iota(jnp.int32, (blk, LANES), 1)
    low = lane < hd

    def dup(x, kv):
        tile = x[:, (kv // 2) * LANES:(kv // 2 + 1) * LANES]
        rolled = pltpu.roll(tile, hd, 1)
        lo_half, hi_half = (tile, rolled) if kv % 2 == 0 else (rolled, tile)
        return jnp.where(low, lo_half, hi_half).astype(BF16)

    def dup_all(x):
        xf = x.astype(F32)
        return [dup(xf, kv) for kv in kvs]

    zero = jnp.zeros((blk, LANES), BF16)
    prev_k = [kprev_ref[kv] for kv in kvs]
    prev_v = [vprev_ref[kv] for kv in kvs]
    for bi in range(MIXER_TILE_M // blk):
        rows = slice(bi * blk, (bi + 1) * blk)
        cur_k = dup_all(k[rows, :])
        cur_v = dup_all(v[rows, :])
        for kv in kvs:
            lhs = []
            for pair in (2 * kv, 2 * kv + 1):
                x = q[rows, pair * LANES:(pair + 1) * LANES]
                lhs += [jnp.where(low, x, zero), jnp.where(low, zero, x)]
            keys = jnp.concatenate([prev_k[kv], cur_k[kv]], axis=0)
            s_all = _dot_t(jnp.concatenate(lhs, axis=0), keys)
            probs = []
            for gi in range(group):
                head = kv * group + gi
                sc = s_all[gi * blk:(gi + 1) * blk]
                s = jnp.where(own, sc[:, blk:], sc[:, :blk]) * (hd ** -0.5 * LOG2E) - bias_ref[head]
                if bi == 0:
                    s = jnp.where(first_valid, s, NEG_INF)
                sink = sink_ref[head] * LOG2E
                m = jnp.maximum(jnp.max(s, axis=-1, keepdims=True), sink)
                p = jnp.exp2(s - m)
                denom = jnp.sum(p, axis=-1, keepdims=True) + jnp.exp2(sink - m)
                pn = p * (1.0 / denom)
                probs.append(jnp.concatenate([jnp.where(own, 0.0, pn), jnp.where(own, pn, 0.0)],
                                             axis=1).astype(BF16))
            vals = jnp.concatenate([prev_v[kv], cur_v[kv]], axis=0)
            r = _dot(jnp.concatenate(probs, axis=0), vals)
            for pi, pair in enumerate((2 * kv, 2 * kv + 1)):
                r0 = r[(2 * pi) * blk:(2 * pi + 1) * blk]
                r1 = r[(2 * pi + 1) * blk:(2 * pi + 2) * blk]
                o_ref[rows, pair * LANES:(pair + 1) * LANES] = jnp.where(low, r0, r1).astype(BF16)
            yield
        prev_k, prev_v = cur_k, cur_v
    for kv in kvs:
        kprev_ref[kv] = prev_k[kv]
        vprev_ref[kv] = prev_v[kv]


def _row_views(w_ref, offsets, widths):
    return [w_ref.at[lo:lo + width, :] for lo, width in zip(offsets, widths)]


def _interleave(first, second, ratio):
    live_first, live_second = True, True
    while live_first or live_second:
        for _ in range(ratio):
            if live_first:
                live_first = next(first, "done") != "done"
        if live_second:
            live_second = next(second, "done") != "done"


def _mixer_kernel(slope_ref, sink_ref, x_ref, ada_ref,
                  w_in_t_ref,
                  wup_ref, bup_ref, seg_ref, mask_ref, tri_ref, gn_ref,
                  wa_ref, wb_ref, wo_ref, lng_ref, lnb_ref,
                  next_wg_ref, next_wu_ref, next_wd_ref,
                  out_ref, next_wg_out, next_wu_out, next_wd_out,
                  q_s, k_s, v_s, la_s, go_s, b_s, o_s, a_s, state_ref, state_new, kprev_s, vprev_s,
                  bias_ref):
    first_tile = pl.program_id(0) == 0
    for src, dst in ((next_wg_ref, next_wg_out), (next_wu_ref, next_wu_out), (next_wd_ref, next_wd_out)):
        dst[...] = src[...].astype(BF16)
    offsets = [sum(MIX_SIZES[:i]) for i in range(len(MIX_SIZES))]
    widths = list(MIX_SIZES)
    widths[3] = LR_PAD
    (wq_ref, wk_ref, wv_ref, wlr_ref, wgo_ref, wsq_ref, wsk_ref, wsv_ref, wga_ref,
     wgb_ref) = _row_views(w_in_t_ref, offsets, widths)

    def project(w_ref, lo=None):
        return _dot_t(h, w_ref[...] if lo is None else w_ref[lo:lo + step, :])

    @pl.when(first_tile)
    def _():
        state_ref[...] = jnp.zeros_like(state_ref)
        kprev_s[...] = jnp.zeros_like(kprev_s)
        vprev_s[...] = jnp.zeros_like(vprev_s)
        _swa_bias_table(slope_ref, bias_ref)

    x = x_ref[...]
    h = (x * (1.0 + _ada_slice(ada_ref, 4)) + _ada_slice(ada_ref, 3)).astype(BF16)
    step = MIXER_DOT_COLS
    col_blocks = range(0, D_MODEL, step)

    guard = {}
    lr = project(wlr_ref).astype(BF16)
    z = _dot(lr, wup_ref[...]) + bup_ref[...]
    la_s[...] = _log_sigmoid(z) * (1.0 / GLA_GATE_NORMALIZER)
    for _ in _gla_cumsum(la_s, tri_ref, b_s, guard):
        pass
    sq = _tie(project(wsq_ref).astype(BF16), b_s[...])
    sk = project(wsk_ref).astype(BF16)
    sv = project(wsv_ref).astype(BF16)

    gates = {}

    def gate(name, w_ref, lo):
        gates[name, lo] = _sigmoid(project(w_ref, lo))

    def projections_1():
        for lo in range(0, GLA_HEADS * GLA_DK, step):
            q_s[:, lo:lo + step] = project(wq_ref, lo) * (GLA_DK ** -0.5)
            yield
            k_s[:, lo:lo + step] = project(wk_ref, lo)
            yield
        for lo in col_blocks:
            v_s[:, lo:lo + step] = project(wv_ref, lo).astype(BF16)
            yield
        for lo in col_blocks:
            go = project(wgo_ref, lo)
            go_s[:, lo:lo + step] = go * _sigmoid(go)
            yield
        for lo in col_blocks:
            gate("a", wga_ref, lo)
            yield

    dense_1 = projections_1()
    next(dense_1)
    _interleave(_swa_tile(sink_ref, sq, sk, sv, kprev_s, vprev_s, bias_ref, first_tile, a_s),
                dense_1, ratio=2)

    def projections_2():
        for lo in col_blocks:
            gate("b", wgb_ref, lo)
            yield
        for lo in col_blocks:
            gates["yb", lo] = _dot(a_s[...], wb_ref[:, lo:lo + step])
            yield

    dense_2 = projections_2()
    next(dense_2)
    _interleave(_gla_tile_fast(q_s, k_s, v_s, b_s, go_s, gn_ref, o_s, state_ref, state_new),
                dense_2, ratio=2)
    safe = guard["safe"]

    @pl.when(jnp.logical_not(safe))
    def _():
        state_new[...] = state_ref[...]
        _gla_tile_any_decay(q_s, k_s, v_s, la_s, go_s, seg_ref, mask_ref, gn_ref, o_s, state_new)

    state_ref[...] = state_new[...]

    gate_a = jnp.concatenate([gates["a", lo] for lo in col_blocks], axis=1)
    gate_b_yb = jnp.concatenate([gates["b", lo] * gates["yb", lo] for lo in col_blocks], axis=1)
    rows_half = MIXER_TILE_M // 2
    for lo in (0, rows_half):
        rows = slice(lo, lo + rows_half)
        ya = _dot(o_s[rows, :], wa_ref[...])
        merged = (gate_a[rows, :] * ya + gate_b_yb[rows, :]).astype(BF16)
        y = _dot(merged, wo_ref[...])
        r = DEEPNORM_ALPHA * x[rows, :] + _ada_slice(ada_ref, 5) * y
        out_ref[rows, :] = _layer_norm(r, lng_ref[...], lnb_ref[...])


def _mixer(x, ada, w_in_t, wup, bup, gn, slopes, sinks, wa, wb, wo, ln_g, ln_b, next_ffn):
    tm = MIXER_TILE_M
    steps = SEQ // tm
    seg, mask, tri = _gla_constants()
    qk = GLA_HEADS * GLA_DK
    row = pl.BlockSpec((tm, D_MODEL), lambda i: (i, 0))
    smem = pl.BlockSpec(memory_space=pltpu.SMEM)
    up_rows = pl.BlockSpec((D_MODEL // steps, D_FF), lambda i: (i, 0))
    down_rows = pl.BlockSpec((2 * D_FF // steps, D_MODEL), lambda i: (i // 2, 0))

    def w(n):
        return _resident((D_MODEL, n))

    vec = _resident((1, D_MODEL))
    return pl.pallas_call(
        _mixer_kernel,
        grid=(steps,),
        in_specs=[smem, smem, row, _resident((1, N_ADA * D_MODEL)),
                  _resident((sum(MIX_SIZES), D_MODEL)),
                  _resident((LR_PAD, qk)), _resident((1, qk)),
                  _resident(seg.shape), _resident(mask.shape), _resident(tri.shape),
                  _resident((1, GLA_DV)),
                  w(D_MODEL), w(D_MODEL), w(D_MODEL), vec, vec,
                  up_rows, up_rows, down_rows],
        out_specs=[row, up_rows, up_rows, down_rows],
        out_shape=[jax.ShapeDtypeStruct((SEQ, D_MODEL), F32),
                   jax.ShapeDtypeStruct((D_MODEL, D_FF), BF16),
                   jax.ShapeDtypeStruct((D_MODEL, D_FF), BF16),
                   jax.ShapeDtypeStruct((D_FF, D_MODEL), BF16)],
        scratch_shapes=[pltpu.VMEM((tm, qk), F32), pltpu.VMEM((tm, qk), F32),
                        pltpu.VMEM((tm, D_MODEL), BF16), pltpu.VMEM((tm, qk), F32),
                        pltpu.VMEM((tm, D_MODEL), F32), pltpu.VMEM((tm, qk), F32),
                        pltpu.VMEM((tm, D_MODEL), BF16), pltpu.VMEM((tm, D_MODEL), BF16),
                        pltpu.VMEM((GLA_HEADS, GLA_DK, GLA_DV), F32),
                        pltpu.VMEM((GLA_HEADS, GLA_DK, GLA_DV), F32),
                        pltpu.VMEM((SWA_KV_HEADS, SWA_BLOCK, LANES), BF16),
                        pltpu.VMEM((SWA_KV_HEADS, SWA_BLOCK, LANES), BF16),
                        pltpu.VMEM((SWA_HEADS, SWA_BLOCK, SWA_BLOCK), F32)],
        compiler_params=_cparams(),
        name="mixer",
    )(slopes, sinks, x, ada, w_in_t, wup, bup,
      jnp.asarray(seg, BF16), jnp.asarray(mask, F32), jnp.asarray(tri, BF16), gn,
      wa, wb, wo, ln_g, ln_b, *next_ffn)


def _alibi_slopes(n):
    return 2.0 ** (-8.0 * jnp.arange(1, n + 1, dtype=jnp.float32) / n)


def kernel(x, c, w_ada, b_ada, ffn1_w_gate, ffn1_w_up, ffn1_w_down, ln1_g, ln1_b, w_in,
           w_gla_gate_up, b_gla_gate, gla_norm_g, w_branch_gla, swa_sinks, w_branch_swa, w_out,
           ln2_g, ln2_b, ffn2_w_gate, ffn2_w_up, ffn2_w_down, ln3_g, ln3_b):
    assert x.shape == (1, SEQ, D_MODEL) and w_ada.shape[0] == 1
    x2d = x.reshape(SEQ, D_MODEL)
    vec = lambda p: p.reshape(1, -1)

    ada = _ada(c, w_ada[0], b_ada[0])
    x1, w_in_t, wa, wb, wo = _ffn(
        x2d, ada, ffn1_w_gate[0], ffn1_w_up[0], ffn1_w_down[0], vec(ln1_g), vec(ln1_b), 0,
        mixer_weights=(jnp.swapaxes(w_in[0], 0, 1), w_branch_gla[0], w_branch_swa[0], w_out[0]))

    wup = jnp.pad(w_gla_gate_up[0], ((0, LR_PAD - GLA_GATE_RANK), (0, 0))).astype(BF16)
    x2, wg2, wu2, wd2 = _mixer(
        x1, ada, w_in_t, wup, vec(b_gla_gate), vec(gla_norm_g),
        _alibi_slopes(SWA_HEADS), swa_sinks[0], wa, wb, wo,
        vec(ln2_g), vec(ln2_b), (ffn2_w_gate[0], ffn2_w_up[0], ffn2_w_down[0]))

    out = _ffn(x2, ada, wg2, wu2, wd2, vec(ln3_g), vec(ln3_b), 6)
    return out.reshape(1, SEQ, D_MODEL)
```

```python
import functools

import numpy as np
import jax
import jax.numpy as jnp
from jax import lax
from jax.experimental import pallas as pl
from jax.experimental.pallas import tpu as pltpu

D_MODEL = 1024
SEQ = 16384
D_FF = 2816
N_ADA = 9
LN_EPS = 1e-5
RMS_EPS = 1e-6
NEG_INF = -1e30
DEEPNORM_ALPHA = 2.0 ** 0.25
MACARON_WEIGHT = 0.5

GLA_HEADS = 4
GLA_DK = 128
GLA_DV = 256
GLA_GATE_RANK = 16
GLA_GATE_NORMALIZER = 16.0
GLA_CHUNK = 64
GLA_LEVELS = (32, 16, 8, 4, 2, 1)

SWA_HEADS = 16
SWA_KV_HEADS = 4
SWA_HEAD_DIM = 64
SWA_BLOCK = 128

MIX_SIZES = (512, 512, 1024, 16, 1024, 1024, 256, 256, 1024, 1024)

LANES = 128
BF16_ROW_TILE = 16
V7X_VMEM_LIMIT_BYTES = 60 * 1024 * 1024

BF16 = jnp.bfloat16
F32 = jnp.float32


def _cparams():
    return pltpu.CompilerParams(dimension_semantics=("arbitrary",),
                                vmem_limit_bytes=V7X_VMEM_LIMIT_BYTES)


def _resident(shape):
    return pl.BlockSpec(shape, lambda i: (0,) * len(shape), pipeline_mode=pl.Buffered(1))


def _round_up(n, multiple):
    return -(-n // multiple) * multiple


def _dot(a, b):
    return jnp.dot(a, b, preferred_element_type=F32)


def _dot_t(a, b):
    return lax.dot_general(a, b, (((1,), (1,)), ((), ())), preferred_element_type=F32)


def _sigmoid(x):
    return 1.0 / (1.0 + jnp.exp(-x))


def _layer_norm(r, g, b):
    mu = jnp.mean(r, axis=-1, keepdims=True)
    d = r - mu
    var = jnp.mean(d * d, axis=-1, keepdims=True)
    return d * lax.rsqrt(var + LN_EPS) * g + b


def _tie(value, other):
    bits = pltpu.bitcast(other, jnp.uint32)
    words = [bits[r:r + 8, c:c + LANES]
             for r in range(0, other.shape[0], 8) for c in range(0, other.shape[1], LANES)]
    while len(words) > 1:
        words = [a | b for a, b in zip(words[0::2], words[1::2])] + words[len(words) & ~1:]
    zero = lax.shift_right_logical(lax.shift_right_logical(words[0], jnp.uint32(16)), jnp.uint32(16))
    zero = jnp.concatenate([zero, zero], axis=0).astype(jnp.int32).astype(F32).astype(value.dtype)
    tile = value[0:BF16_ROW_TILE, 0:LANES] + zero
    top = jnp.concatenate([tile, value[0:BF16_ROW_TILE, LANES:]], axis=1)
    return jnp.concatenate([top, value[BF16_ROW_TILE:, :]], axis=0)


def _ada_slice(ada_ref, idx):
    return ada_ref[:, idx * D_MODEL:(idx + 1) * D_MODEL]


ADA_TILE_N = 1536


def _ada_kernel(c_ref, w_ref, b_ref, o_ref):
    c = c_ref[...]
    s = (c * _sigmoid(c)).astype(BF16)
    s8 = jnp.broadcast_to(s, (8, D_MODEL))
    y = _dot(s8, w_ref[...].astype(BF16))
    o_ref[...] = y[0:1, :] + b_ref[...]


def _ada(c, w_ada, b_ada):
    n = N_ADA * D_MODEL
    return pl.pallas_call(
        _ada_kernel,
        grid=(n // ADA_TILE_N,),
        in_specs=[pl.BlockSpec((1, D_MODEL), lambda i: (0, 0)),
                  pl.BlockSpec((D_MODEL, ADA_TILE_N), lambda i: (0, i)),
                  pl.BlockSpec((1, ADA_TILE_N), lambda i: (0, i))],
        out_specs=pl.BlockSpec((1, ADA_TILE_N), lambda i: (0, i)),
        out_shape=jax.ShapeDtypeStruct((1, n), F32),
        compiler_params=_cparams(),
        name="ada",
    )(c, w_ada, b_ada.reshape(1, n))


FFN_TILE_M = 512
FFN_SPLIT = 1536
FFN_UP_STAGE_ROWS = 128
FFN_DOWN_STAGE_ROWS = 352
FFN_STAGE_SLOTS = 4


def _load_cast_rows(src_hbm, dst_ref, stage_ref, sems, rows):
    n = src_hbm.shape[0] // rows
    slots = stage_ref.shape[0]

    def copy(c):
        return pltpu.make_async_copy(src_hbm.at[pl.ds(c * rows, rows), :], stage_ref.at[c % slots],
                                     sems.at[c % slots])

    for c in range(min(slots - 1, n)):
        copy(c).start()
    for c in range(n):
        if c + slots - 1 < n:
            copy(c + slots - 1).start()
        copy(c).wait()
        dst_ref[c * rows:(c + 1) * rows, :] = stage_ref[c % slots].astype(BF16)


def _ffn_kernel(x_ref, ada_ref, wg_ref, wu_ref, wd_ref, lng_ref, lnb_ref, *rest, ada_base,
                f32_weights):
    if f32_weights:
        *rest, wg_s, wu_s, wd_s, stage_up, stage_down, sem_up, sem_down = rest

        @pl.when(pl.program_id(0) == 0)
        def _():
            _load_cast_rows(wg_ref, wg_s, stage_up, sem_up, FFN_UP_STAGE_ROWS)
            _load_cast_rows(wu_ref, wu_s, stage_up, sem_up, FFN_UP_STAGE_ROWS)
            _load_cast_rows(wd_ref, wd_s, stage_down, sem_down, FFN_DOWN_STAGE_ROWS)

        wg_ref, wu_ref, wd_ref = wg_s, wu_s, wd_s
    n_cast = len(rest) // 2
    o_ref = rest[n_cast]
    for src, dst in zip(rest[:n_cast], rest[n_cast + 1:]):
        dst[...] = src[...].astype(BF16)
    x = x_ref[...]
    sh = _ada_slice(ada_ref, ada_base)
    sc = _ada_slice(ada_ref, ada_base + 1)
    gt = _ada_slice(ada_ref, ada_base + 2)
    h = (x * (1.0 + sc) + sh).astype(BF16)
    y = None
    for lo, hi in ((0, FFN_SPLIT), (FFN_SPLIT, D_FF)):
        g = _dot(h, wg_ref[:, lo:hi])
        u = _dot(h, wu_ref[:, lo:hi])
        a = (g * _sigmoid(g) * u).astype(BF16)
        part = _dot(a, wd_ref[lo:hi, :])
        y = part if y is None else y + part
    r = DEEPNORM_ALPHA * x + (MACARON_WEIGHT * gt) * y
    o_ref[...] = _layer_norm(r, lng_ref[...], lnb_ref[...])


def _ffn(x, ada, wg, wu, wd, ln_g, ln_b, ada_base, mixer_weights=None):
    tm = FFN_TILE_M
    steps = SEQ // tm
    row = pl.BlockSpec((tm, D_MODEL), lambda i: (i, 0))
    f32_weights = wg.dtype == F32
    scratch = []
    if f32_weights:
        weight_specs = [pl.BlockSpec(memory_space=pl.ANY)] * 3
        scratch = [pltpu.VMEM((D_MODEL, D_FF), BF16), pltpu.VMEM((D_MODEL, D_FF), BF16),
                   pltpu.VMEM((D_FF, D_MODEL), BF16),
                   pltpu.VMEM((FFN_STAGE_SLOTS, FFN_UP_STAGE_ROWS, D_FF), F32),
                   pltpu.VMEM((FFN_STAGE_SLOTS, FFN_DOWN_STAGE_ROWS, D_MODEL), F32),
                   pltpu.SemaphoreType.DMA((FFN_STAGE_SLOTS,)),
                   pltpu.SemaphoreType.DMA((FFN_STAGE_SLOTS,))]
    else:
        weight_specs = [_resident((D_MODEL, D_FF)), _resident((D_MODEL, D_FF)),
                        _resident((D_FF, D_MODEL))]
    in_specs = [row, _resident((1, N_ADA * D_MODEL)), *weight_specs,
                _resident((1, D_MODEL)), _resident((1, D_MODEL))]
    out_specs = [row]
    out_shape = [jax.ShapeDtypeStruct((SEQ, D_MODEL), F32)]
    operands = [x, ada, wg, wu, wd, ln_g, ln_b]
    for w in mixer_weights or ():
        rows = _round_up(pl.cdiv(w.shape[0], steps), BF16_ROW_TILE)
        last = pl.cdiv(w.shape[0], rows) - 1
        spec = pl.BlockSpec((rows, D_MODEL), lambda i, last=last: (jnp.minimum(i, last), 0))
        in_specs.append(spec)
        operands.append(w)
        out_specs.append(spec)
        out_shape.append(jax.ShapeDtypeStruct(w.shape, BF16))
    outs = pl.pallas_call(
        functools.partial(_ffn_kernel, ada_base=ada_base, f32_weights=f32_weights),
        grid=(steps,),
        in_specs=in_specs,
        out_specs=out_specs,
        out_shape=out_shape,
        scratch_shapes=scratch,
        compiler_params=_cparams(),
        name="ffn",
    )(*operands)
    return outs[0] if mixer_weights is None else outs


MIXER_TILE_M = 512
MIXER_DOT_COLS = 512
LR_PAD = LANES
GLA_NSEG = 1 + len(GLA_LEVELS)
GLA_SPLIT = 2
GLA_FAST_CHUNK = 128
GLA_SAFE_LOG_DECAY = -60.0


def _log_sigmoid(z):
    return jnp.minimum(z, 0.0) - jnp.log(1.0 + jnp.exp(-jnp.abs(z)))


def _gla_constants():
    c = GLA_CHUNK
    t = np.arange(c)[:, None]
    u = np.arange(c)[None, :]
    seg = [(u <= t)]
    mask = [(u == t)]
    for m in GLA_LEVELS:
        blk_t, blk_u = t // m, u // m
        odd = (blk_t % 2) == 1
        p_odd = blk_t * m
        p_even = (blk_t + 1) * m
        seg.append(np.where(odd, (u > p_odd) & (u <= t), (u > t) & (u <= p_even)))
        mask.append(odd & (blk_u == blk_t - 1))
    seg = np.concatenate(seg, axis=0).astype(np.float32)
    seg = np.concatenate([seg] * GLA_SPLIT, axis=1)
    mask = np.stack(mask).astype(np.float32)
    assert mask.shape[0] == GLA_NSEG and seg.shape == (GLA_NSEG * c, GLA_SPLIT * c)
    fc = GLA_FAST_CHUNK
    tri = np.tril(np.ones((fc, fc), np.float32))
    tri = np.concatenate([tri] * GLA_SPLIT, axis=1)
    return seg, mask, tri


def _split_terms(g):
    parts = []
    rem = g
    for _ in range(GLA_SPLIT):
        p = rem.astype(BF16)
        parts.append(p)
        rem = rem - p.astype(F32)
    return jnp.concatenate(parts, axis=0)


def _decay_column(bl):
    return jnp.exp(jnp.broadcast_to(bl, (8, GLA_DK))).T[:, 0:1]


def _gla_finish(o, gn_ref, go, o_ref, rows, vs):
    ms = jnp.mean(o * o, axis=-1, keepdims=True)
    on = o * lax.rsqrt(ms + RMS_EPS) * gn_ref[...]
    o_ref[rows, vs] = (on * go).astype(BF16)


def _gla_cumsum(la_ref, tri_ref, b_ref, result):
    c = GLA_FAST_CHUNK
    lowest = None
    for ci in range(MIXER_TILE_M // c):
        rows = slice(ci * c, (ci + 1) * c)
        b = _dot(tri_ref[...], _split_terms(la_ref[rows, :]))
        b_ref[rows, :] = b
        total = jnp.min(b[c - 1:c])
        lowest = total if lowest is None else jnp.minimum(lowest, total)
        result["safe"] = lowest >= GLA_SAFE_LOG_DECAY
        yield


def _gla_tile_fast(q_ref, k_ref, v_ref, b_ref, go_ref, gn_ref, o_ref, state_in_ref, state_out_ref):
    c = GLA_FAST_CHUNK
    ti = lax.broadcasted_iota(jnp.int32, (c, c), 0)
    si = lax.broadcasted_iota(jnp.int32, (c, c), 1)
    causal = si <= ti
    chunks = range(MIXER_TILE_M // c)
    heads = range(GLA_HEADS)
    rows = [slice(ci * c, (ci + 1) * c) for ci in chunks]
    kcols = [slice(hd * GLA_DK, (hd + 1) * GLA_DK) for hd in heads]
    vcols = [slice(hd * GLA_DV, (hd + 1) * GLA_DV) for hd in heads]

    q_in, a, upd, d_col = {}, {}, {}, {}
    for ci in chunks:
        b = b_ref[rows[ci], :]
        b_last = b[c - 1:c]
        e_pos = jnp.exp(b)
        e_neg = jnp.exp(-b)
        e_last = jnp.exp(b_last)
        for hd in heads:
            ks = kcols[hd]
            q_in[ci, hd] = (q_ref[rows[ci], ks] * e_pos[:, ks]).astype(BF16)
            k_out = k_ref[rows[ci], ks] * e_neg[:, ks]
            a[ci, hd] = jnp.where(causal, _dot_t(q_in[ci, hd], k_out.astype(BF16)), 0.0).astype(BF16)
            k_dec_t = (k_out * e_last[:, ks]).T.astype(BF16)
            upd[ci, hd] = _dot(k_dec_t, v_ref[rows[ci], vcols[hd]])
            d_col[ci, hd] = _decay_column(b_last[:, ks])
        yield

    for hd in heads:
        state = state_in_ref[hd]
        for ci in chunks:
            v = v_ref[rows[ci], vcols[hd]]
            o = _dot(jnp.concatenate([q_in[ci, hd], a[ci, hd]], axis=1),
                     jnp.concatenate([state.astype(BF16), v], axis=0))
            state = d_col[ci, hd] * state + upd[ci, hd]
            _gla_finish(o, gn_ref, go_ref[rows[ci], vcols[hd]], o_ref, rows[ci], vcols[hd])
        state_out_ref[hd] = state
        yield


def _gla_tile_any_decay(q_ref, k_ref, v_ref, la_ref, go_ref, seg_ref, mask_ref, gn_ref, o_ref,
                        state_ref):
    c = GLA_CHUNK

    def chunk(ci, carry):
        rows = pl.ds(pl.multiple_of(ci * c, c), c)
        e_all = _dot(seg_ref[...], _split_terms(la_ref[rows, :]))
        b = e_all[0:c]
        b_last = b[c - 1:c]
        for hd in range(GLA_HEADS):
            ks = slice(hd * GLA_DK, (hd + 1) * GLA_DK)
            vs = slice(hd * GLA_DV, (hd + 1) * GLA_DV)
            q = q_ref[rows, ks]
            k = k_ref[rows, ks]
            v = v_ref[rows, vs]
            bh = b[:, ks]
            bl = b_last[:, ks]
            state = state_ref[hd]
            o = _dot((q * jnp.exp(bh)).astype(BF16), state.astype(BF16))
            a = _dot_t(q.astype(BF16), k.astype(BF16)) * mask_ref[0]
            for li in range(len(GLA_LEVELS)):
                e = jnp.exp(e_all[(li + 1) * c:(li + 2) * c, ks])
                a = a + _dot_t((q * e).astype(BF16), (k * e).astype(BF16)) * mask_ref[li + 1]
            o = o + _dot(a.astype(BF16), v)
            k_dec = k * jnp.exp(bl - bh)
            state_ref[hd] = _decay_column(bl) * state + _dot(k_dec.T.astype(BF16), v)
            _gla_finish(o, gn_ref, go_ref[rows, vs], o_ref, rows, vs)
        return carry

    lax.fori_loop(0, MIXER_TILE_M // c, chunk, 0)


LOG2E = 1.4426950408889634


def _swa_bias_table(slope_ref, bias_ref):
    blk = SWA_BLOCK
    qi = lax.broadcasted_iota(jnp.int32, (blk, blk), 0)
    kj = lax.broadcasted_iota(jnp.int32, (blk, blk), 1)
    dist = jnp.where(kj <= qi, qi - kj, qi + blk - kj).astype(F32)
    for head in range(SWA_HEADS):
        bias_ref[head] = (slope_ref[head] * LOG2E) * dist


def _swa_tile(sink_ref, q, k, v, kprev_ref, vprev_ref, bias_ref, first_tile, o_ref):
    blk = SWA_BLOCK
    hd = SWA_HEAD_DIM
    group = SWA_HEADS // SWA_KV_HEADS
    kvs = range(SWA_KV_HEADS)

    qi = lax.broadcasted_iota(jnp.int32, (blk, blk), 0)
    kj = lax.broadcasted_iota(jnp.int32, (blk, blk), 1)
    own = kj <= qi
    first_valid = (kj - qi) <= jnp.where(first_tile, 0, blk)
    lane = lax.broadcasted_iota(jnp.int32, (blk, LANES), 1)
    low = lane < hd

    def dup(x, kv):
        tile = x[:, (kv // 2) * LANES:(kv // 2 + 1) * LANES]
        rolled = pltpu.roll(tile, hd, 1)
        lo_half, hi_half = (tile, rolled) if kv % 2 == 0 else (rolled, tile)
        return jnp.where(low, lo_half, hi_half).astype(BF16)

    def dup_all(x):
        xf = x.astype(F32)
        return [dup(xf, kv) for kv in kvs]

    zero = jnp.zeros((blk, LANES), BF16)
    prev_k = [kprev_ref[kv] for kv in kvs]
    prev_v = [vprev_ref[kv] for kv in kvs]
    for bi in range(MIXER_TILE_M // blk):
        rows = slice(bi * blk, (bi + 1) * blk)
        cur_k = dup_all(k[rows, :])
        cur_v = dup_all(v[rows, :])
        for kv in kvs:
            lhs = []
            for pair in (2 * kv, 2 * kv + 1):
                x = q[rows, pair * LANES:(pair + 1) * LANES]
                lhs += [jnp.where(low, x, zero), jnp.where(low, zero, x)]
            keys = jnp.concatenate([prev_k[kv], cur_k[kv]], axis=0)
            s_all = _dot_t(jnp.concatenate(lhs, axis=0), keys)
            probs = []
            for gi in range(group):
                head = kv * group + gi
                sc = s_all[gi * blk:(gi + 1) * blk]
                s = jnp.where(own, sc[:, blk:], sc[:, :blk]) * (hd ** -0.5 * LOG2E) - bias_ref[head]
                if bi == 0:
                    s = jnp.where(first_valid, s, NEG_INF)
                sink = sink_ref[head] * LOG2E
                m = jnp.maximum(jnp.max(s, axis=-1, keepdims=True), sink)
                p = jnp.exp2(s - m)
                denom = jnp.sum(p, axis=-1, keepdims=True) + jnp.exp2(sink - m)
                pn = p * (1.0 / denom)
                probs.append(jnp.concatenate([jnp.where(own, 0.0, pn), jnp.where(own, pn, 0.0)],
                                             axis=1).astype(BF16))
            vals = jnp.concatenate([prev_v[kv], cur_v[kv]], axis=0)
            r = _dot(jnp.concatenate(probs, axis=0), vals)
            for pi, pair in enumerate((2 * kv, 2 * kv + 1)):
                r0 = r[(2 * pi) * blk:(2 * pi + 1) * blk]
                r1 = r[(2 * pi + 1) * blk:(2 * pi + 2) * blk]
                o_ref[rows, pair * LANES:(pair + 1) * LANES] = jnp.where(low, r0, r1).astype(BF16)
            yield
        prev_k, prev_v = cur_k, cur_v
    for kv in kvs:
        kprev_ref[kv] = prev_k[kv]
        vprev_ref[kv] = prev_v[kv]


def _row_views(w_ref, offsets, widths):
    return [w_ref.at[lo:lo + width, :] for lo, width in zip(offsets, widths)]


def _interleave(first, second, ratio):
    live_first, live_second = True, True
    while live_first or live_second:
        for _ in range(ratio):
            if live_first:
                live_first = next(first, "done") != "done"
        if live_second:
            live_second = next(second, "done") != "done"


def _mixer_kernel(slope_ref, sink_ref, x_ref, ada_ref,
                  w_in_t_ref,
                  wup_ref, bup_ref, seg_ref, mask_ref, tri_ref, gn_ref,
                  wa_ref, wb_ref, wo_ref, lng_ref, lnb_ref,
                  next_wg_ref, next_wu_ref, next_wd_ref,
                  out_ref, next_wg_out, next_wu_out, next_wd_out,
                  q_s, k_s, v_s, la_s, go_s, b_s, o_s, a_s, state_ref, state_new, kprev_s, vprev_s,
                  bias_ref):
    first_tile = pl.program_id(0) == 0
    for src, dst in ((next_wg_ref, next_wg_out), (next_wu_ref, next_wu_out), (next_wd_ref, next_wd_out)):
        dst[...] = src[...].astype(BF16)
    offsets = [sum(MIX_SIZES[:i]) for i in range(len(MIX_SIZES))]
    widths = list(MIX_SIZES)
    widths[3] = LR_PAD
    (wq_ref, wk_ref, wv_ref, wlr_ref, wgo_ref, wsq_ref, wsk_ref, wsv_ref, wga_ref,
     wgb_ref) = _row_views(w_in_t_ref, offsets, widths)

    def project(w_ref, lo=None):
        return _dot_t(h, w_ref[...] if lo is None else w_ref[lo:lo + step, :])

    @pl.when(first_tile)
    def _():
        state_ref[...] = jnp.zeros_like(state_ref)
        kprev_s[...] = jnp.zeros_like(kprev_s)
        vprev_s[...] = jnp.zeros_like(vprev_s)
        _swa_bias_table(slope_ref, bias_ref)

    x = x_ref[...]
    h = (x * (1.0 + _ada_slice(ada_ref, 4)) + _ada_slice(ada_ref, 3)).astype(BF16)
    step = MIXER_DOT_COLS
    col_blocks = range(0, D_MODEL, step)

    guard = {}
    lr = project(wlr_ref).astype(BF16)
    z = _dot(lr, wup_ref[...]) + bup_ref[...]
    la_s[...] = _log_sigmoid(z) * (1.0 / GLA_GATE_NORMALIZER)
    for _ in _gla_cumsum(la_s, tri_ref, b_s, guard):
        pass
    sq = _tie(project(wsq_ref).astype(BF16), b_s[...])
    sk = project(wsk_ref).astype(BF16)
    sv = project(wsv_ref).astype(BF16)

    gates = {}

    def gate(name, w_ref, lo):
        gates[name, lo] = _sigmoid(project(w_ref, lo))

    def projections_1():
        for lo in range(0, GLA_HEADS * GLA_DK, step):
            q_s[:, lo:lo + step] = project(wq_ref, lo) * (GLA_DK ** -0.5)
            yield
            k_s[:, lo:lo + step] = project(wk_ref, lo)
            yield
        for lo in col_blocks:
            v_s[:, lo:lo + step] = project(wv_ref, lo).astype(BF16)
            yield
        for lo in col_blocks:
            go = project(wgo_ref, lo)
            go_s[:, lo:lo + step] = go * _sigmoid(go)
            yield
        for lo in col_blocks:
            gate("a", wga_ref, lo)
            yield

    dense_1 = projections_1()
    next(dense_1)
    _interleave(_swa_tile(sink_ref, sq, sk, sv, kprev_s, vprev_s, bias_ref, first_tile, a_s),
                dense_1, ratio=2)

    def projections_2():
        for lo in col_blocks:
            gate("b", wgb_ref, lo)
            yield
        for lo in col_blocks:
            gates["yb", lo] = _dot(a_s[...], wb_ref[:, lo:lo + step])
            yield

    dense_2 = projections_2()
    next(dense_2)
    _interleave(_gla_tile_fast(q_s, k_s, v_s, b_s, go_s, gn_ref, o_s, state_ref, state_new),
                dense_2, ratio=2)
    safe = guard["safe"]

    @pl.when(jnp.logical_not(safe))
    def _():
        state_new[...] = state_ref[...]
        _gla_tile_any_decay(q_s, k_s, v_s, la_s, go_s, seg_ref, mask_ref, gn_ref, o_s, state_new)

    state_ref[...] = state_new[...]

    gate_a = jnp.concatenate([gates["a", lo] for lo in col_blocks], axis=1)
    gate_b_yb = jnp.concatenate([gates["b", lo] * gates["yb", lo] for lo in col_blocks], axis=1)
    rows_half = MIXER_TILE_M // 2
    for lo in (0, rows_half):
        rows = slice(lo, lo + rows_half)
        ya = _dot(o_s[rows, :], wa_ref[...])
        merged = (gate_a[rows, :] * ya + gate_b_yb[rows, :]).astype(BF16)
        y = _dot(merged, wo_ref[...])
        r = DEEPNORM_ALPHA * x[rows, :] + _ada_slice(ada_ref, 5) * y
        out_ref[rows, :] = _layer_norm(r, lng_ref[...], lnb_ref[...])


def _mixer(x, ada, w_in_t, wup, bup, gn, slopes, sinks, wa, wb, wo, ln_g, ln_b, next_ffn):
    tm = MIXER_TILE_M
    steps = SEQ // tm
    seg, mask, tri = _gla_constants()
    qk = GLA_HEADS * GLA_DK
    row = pl.BlockSpec((tm, D_MODEL), lambda i: (i, 0))
    smem = pl.BlockSpec(memory_space=pltpu.SMEM)
    up_rows = pl.BlockSpec((D_MODEL // steps, D_FF), lambda i: (i, 0))
    down_rows = pl.BlockSpec((2 * D_FF // steps, D_MODEL), lambda i: (i // 2, 0))

    def w(n):
        return _resident((D_MODEL, n))

    vec = _resident((1, D_MODEL))
    return pl.pallas_call(
        _mixer_kernel,
        grid=(steps,),
        in_specs=[smem, smem, row, _resident((1, N_ADA * D_MODEL)),
                  _resident((sum(MIX_SIZES), D_MODEL)),
                  _resident((LR_PAD, qk)), _resident((1, qk)),
                  _resident(seg.shape), _resident(mask.shape), _resident(tri.shape),
                  _resident((1, GLA_DV)),
                  w(D_MODEL), w(D_MODEL), w(D_MODEL), vec, vec,
                  up_rows, up_rows, down_rows],
        out_specs=[row, up_rows, up_rows, down_rows],
        out_shape=[jax.ShapeDtypeStruct((SEQ, D_MODEL), F32),
                   jax.ShapeDtypeStruct((D_MODEL, D_FF), BF16),
                   jax.ShapeDtypeStruct((D_MODEL, D_FF), BF16),
                   jax.ShapeDtypeStruct((D_FF, D_MODEL), BF16)],
        scratch_shapes=[pltpu.VMEM((tm, qk), F32), pltpu.VMEM((tm, qk), F32),
                        pltpu.VMEM((tm, D_MODEL), BF16), pltpu.VMEM((tm, qk), F32),
                        pltpu.VMEM((tm, D_MODEL), F32), pltpu.VMEM((tm, qk), F32),
                        pltpu.VMEM((tm, D_MODEL), BF16), pltpu.VMEM((tm, D_MODEL), BF16),
                        pltpu.VMEM((GLA_HEADS, GLA_DK, GLA_DV), F32),
                        pltpu.VMEM((GLA_HEADS, GLA_DK, GLA_DV), F32),
                        pltpu.VMEM((SWA_KV_HEADS, SWA_BLOCK, LANES), BF16),
                        pltpu.VMEM((SWA_KV_HEADS, SWA_BLOCK, LANES), BF16),
                        pltpu.VMEM((SWA_HEADS, SWA_BLOCK, SWA_BLOCK), F32)],
        compiler_params=_cparams(),
        name="mixer",
    )(slopes, sinks, x, ada, w_in_t, wup, bup,
      jnp.asarray(seg, BF16), jnp.asarray(mask, F32), jnp.asarray(tri, BF16), gn,
      wa, wb, wo, ln_g, ln_b, *next_ffn)


def _alibi_slopes(n):
    return 2.0 ** (-8.0 * jnp.arange(1, n + 1, dtype=jnp.float32) / n)


def kernel(x, c, w_ada, b_ada, ffn1_w_gate, ffn1_w_up, ffn1_w_down, ln1_g, ln1_b, w_in,
           w_gla_gate_up, b_gla_gate, gla_norm_g, w_branch_gla, swa_sinks, w_branch_swa, w_out,
           ln2_g, ln2_b, ffn2_w_gate, ffn2_w_up, ffn2_w_down, ln3_g, ln3_b):
    assert x.shape == (1, SEQ, D_MODEL) and w_ada.shape[0] == 1
    x2d = x.reshape(SEQ, D_MODEL)
    vec = lambda p: p.reshape(1, -1)

    ada = _ada(c, w_ada[0], b_ada[0])
    x1, w_in_t, wa, wb, wo = _ffn(
        x2d, ada, ffn1_w_gate[0], ffn1_w_up[0], ffn1_w_down[0], vec(ln1_g), vec(ln1_b), 0,
        mixer_weights=(jnp.swapaxes(w_in[0], 0, 1), w_branch_gla[0], w_branch_swa[0], w_out[0]))

    wup = jnp.pad(w_gla_gate_up[0], ((0, LR_PAD - GLA_GATE_RANK), (0, 0))).astype(BF16)
    x2, wg2, wu2, wd2 = _mixer(
        x1, ada, w_in_t, wup, vec(b_gla_gate), vec(gla_norm_g),
        _alibi_slopes(SWA_HEADS), swa_sinks[0], wa, wb, wo,
        vec(ln2_g), vec(ln2_b), (ffn2_w_gate[0], ffn2_w_up[0], ffn2_w_down[0]))

    out = _ffn(x2, ada, wg2, wu2, wd2, vec(ln3_g), vec(ln3_b), 6)
    return out.reshape(1, SEQ, D_MODEL)
```

```python
import functools

import numpy as np
import jax
import jax.numpy as jnp
from jax import lax
from jax.experimental import pallas as pl
from jax.experimental.pallas import tpu as pltpu

D_MODEL = 1024
SEQ = 16384
D_FF = 2816
N_ADA = 9
LN_EPS = 1e-5
RMS_EPS = 1e-6
NEG_INF = -1e30
DEEPNORM_ALPHA = 2.0 ** 0.25
MACARON_WEIGHT = 0.5

GLA_HEADS = 4
GLA_DK = 128
GLA_DV = 256
GLA_GATE_RANK = 16
GLA_GATE_NORMALIZER = 16.0
GLA_CHUNK = 64
GLA_LEVELS = (32, 16, 8, 4, 2, 1)

SWA_HEADS = 16
SWA_KV_HEADS = 4
SWA_HEAD_DIM = 64
SWA_BLOCK = 128

MIX_SIZES = (512, 512, 1024, 16, 1024, 1024, 256, 256, 1024, 1024)

LANES = 128
BF16_ROW_TILE = 16
V7X_VMEM_LIMIT_BYTES = 60 * 1024 * 1024

BF16 = jnp.bfloat16
F32 = jnp.float32


def _cparams():
    return pltpu.CompilerParams(dimension_semantics=("arbitrary",),
                                vmem_limit_bytes=V7X_VMEM_LIMIT_BYTES)


def _resident(shape):
    return pl.BlockSpec(shape, lambda i: (0,) * len(shape), pipeline_mode=pl.Buffered(1))


def _round_up(n, multiple):
    return -(-n // multiple) * multiple


def _dot(a, b):
    return jnp.dot(a, b, preferred_element_type=F32)


def _dot_t(a, b):
    return lax.dot_general(a, b, (((1,), (1,)), ((), ())), preferred_element_type=F32)


def _sigmoid(x):
    return 1.0 / (1.0 + jnp.exp(-x))


def _layer_norm(r, g, b):
    mu = jnp.mean(r, axis=-1, keepdims=True)
    d = r - mu
    var = jnp.mean(d * d, axis=-1, keepdims=True)
    return d * lax.rsqrt(var + LN_EPS) * g + b


def _tie(value, other):
    bits = pltpu.bitcast(other, jnp.uint32)
    words = [bits[r:r + 8, c:c + LANES]
             for r in range(0, other.shape[0], 8) for c in range(0, other.shape[1], LANES)]
    while len(words) > 1:
        words = [a | b for a, b in zip(words[0::2], words[1::2])] + words[len(words) & ~1:]
    zero = lax.shift_right_logical(lax.shift_right_logical(words[0], jnp.uint32(16)), jnp.uint32(16))
    zero = jnp.concatenate([zero, zero], axis=0).astype(jnp.int32).astype(F32).astype(value.dtype)
    tile = value[0:BF16_ROW_TILE, 0:LANES] + zero
    top = jnp.concatenate([tile, value[0:BF16_ROW_TILE, LANES:]], axis=1)
    return jnp.concatenate([top, value[BF16_ROW_TILE:, :]], axis=0)


def _ada_slice(ada_ref, idx):
    return ada_ref[:, idx * D_MODEL:(idx + 1) * D_MODEL]


ADA_TILE_N = 1536


def _ada_kernel(c_ref, w_ref, b_ref, o_ref):
    c = c_ref[...]
    s = (c * _sigmoid(c)).astype(BF16)
    s8 = jnp.broadcast_to(s, (8, D_MODEL))
    y = _dot(s8, w_ref[...].astype(BF16))
    o_ref[...] = y[0:1, :] + b_ref[...]


def _ada(c, w_ada, b_ada):
    n = N_ADA * D_MODEL
    return pl.pallas_call(
        _ada_kernel,
        grid=(n // ADA_TILE_N,),
        in_specs=[pl.BlockSpec((1, D_MODEL), lambda i: (0, 0)),
                  pl.BlockSpec((D_MODEL, ADA_TILE_N), lambda i: (0, i)),
                  pl.BlockSpec((1, ADA_TILE_N), lambda i: (0, i))],
        out_specs=pl.BlockSpec((1, ADA_TILE_N), lambda i: (0, i)),
        out_shape=jax.ShapeDtypeStruct((1, n), F32),
        compiler_params=_cparams(),
        name="ada",
    )(c, w_ada, b_ada.reshape(1, n))


FFN_TILE_M = 512
FFN_SPLITS = (0, 1536, D_FF)
FFN_BIG_TILE_M = 1024
FFN_BIG_SPLITS = (0, 1024, 2048, D_FF)
FFN_UP_STAGE_ROWS = 128
FFN_DOWN_STAGE_ROWS = 352
FFN_STAGE_SLOTS = 4


def _load_cast_rows(src_hbm, dst_ref, stage_ref, sems, rows):
    n = src_hbm.shape[0] // rows
    slots = stage_ref.shape[0]

    def copy(c):
        return pltpu.make_async_copy(src_hbm.at[pl.ds(c * rows, rows), :], stage_ref.at[c % slots],
                                     sems.at[c % slots])

    for c in range(min(slots - 1, n)):
        copy(c).start()
    for c in range(n):
        if c + slots - 1 < n:
            copy(c + slots - 1).start()
        copy(c).wait()
        dst_ref[c * rows:(c + 1) * rows, :] = stage_ref[c % slots].astype(BF16)


def _ffn_kernel(x_ref, ada_ref, wg_ref, wu_ref, wd_ref, lng_ref, lnb_ref, *rest, ada_base,
                f32_weights, splits):
    if f32_weights:
        *rest, wg_s, wu_s, wd_s, stage_up, stage_down, sem_up, sem_down = rest

        @pl.when(pl.program_id(0) == 0)
        def _():
            _load_cast_rows(wg_ref, wg_s, stage_up, sem_up, FFN_UP_STAGE_ROWS)
            _load_cast_rows(wu_ref, wu_s, stage_up, sem_up, FFN_UP_STAGE_ROWS)
            _load_cast_rows(wd_ref, wd_s, stage_down, sem_down, FFN_DOWN_STAGE_ROWS)

        wg_ref, wu_ref, wd_ref = wg_s, wu_s, wd_s
    n_cast = len(rest) // 2
    o_ref = rest[n_cast]
    for src, dst in zip(rest[:n_cast], rest[n_cast + 1:]):
        dst[...] = src[...].astype(BF16)
    x = x_ref[...]
    sh = _ada_slice(ada_ref, ada_base)
    sc = _ada_slice(ada_ref, ada_base + 1)
    gt = _ada_slice(ada_ref, ada_base + 2)
    h = (x * (1.0 + sc) + sh).astype(BF16)
    y = None
    for lo, hi in zip(splits[:-1], splits[1:]):
        g = _dot(h, wg_ref[:, lo:hi])
        u = _dot(h, wu_ref[:, lo:hi])
        a = (g * _sigmoid(g) * u).astype(BF16)
        part = _dot(a, wd_ref[lo:hi, :])
        y = part if y is None else y + part
    r = DEEPNORM_ALPHA * x + (MACARON_WEIGHT * gt) * y
    o_ref[...] = _layer_norm(r, lng_ref[...], lnb_ref[...])


def _ffn(x, ada, wg, wu, wd, ln_g, ln_b, ada_base, mixer_weights=None):
    f32_weights = wg.dtype == F32
    tm, splits = (FFN_TILE_M, FFN_SPLITS) if f32_weights else (FFN_BIG_TILE_M, FFN_BIG_SPLITS)
    steps = SEQ // tm
    row = pl.BlockSpec((tm, D_MODEL), lambda i: (i, 0))
    scratch = []
    if f32_weights:
        weight_specs = [pl.BlockSpec(memory_space=pl.ANY)] * 3
        scratch = [pltpu.VMEM((D_MODEL, D_FF), BF16), pltpu.VMEM((D_MODEL, D_FF), BF16),
                   pltpu.VMEM((D_FF, D_MODEL), BF16),
                   pltpu.VMEM((FFN_STAGE_SLOTS, FFN_UP_STAGE_ROWS, D_FF), F32),
                   pltpu.VMEM((FFN_STAGE_SLOTS, FFN_DOWN_STAGE_ROWS, D_MODEL), F32),
                   pltpu.SemaphoreType.DMA((FFN_STAGE_SLOTS,)),
                   pltpu.SemaphoreType.DMA((FFN_STAGE_SLOTS,))]
    else:
        weight_specs = [_resident((D_MODEL, D_FF)), _resident((D_MODEL, D_FF)),
                        _resident((D_FF, D_MODEL))]
    in_specs = [row, _resident((1, N_ADA * D_MODEL)), *weight_specs,
                _resident((1, D_MODEL)), _resident((1, D_MODEL))]
    out_specs = [row]
    out_shape = [jax.ShapeDtypeStruct((SEQ, D_MODEL), F32)]
    operands = [x, ada, wg, wu, wd, ln_g, ln_b]
    for w in mixer_weights or ():
        rows = _round_up(pl.cdiv(w.shape[0], steps), BF16_ROW_TILE)
        last = pl.cdiv(w.shape[0], rows) - 1
        spec = pl.BlockSpec((rows, D_MODEL), lambda i, last=last: (jnp.minimum(i, last), 0))
        in_specs.append(spec)
        operands.append(w)
        out_specs.append(spec)
        out_shape.append(jax.ShapeDtypeStruct(w.shape, BF16))
    outs = pl.pallas_call(
        functools.partial(_ffn_kernel, ada_base=ada_base, f32_weights=f32_weights, splits=splits),
        grid=(steps,),
        in_specs=in_specs,
        out_specs=out_specs,
        out_shape=out_shape,
        scratch_shapes=scratch,
        compiler_params=_cparams(),
        name="ffn",
    )(*operands)
    return outs[0] if mixer_weights is None else outs


MIXER_TILE_M = 512
MIXER_DOT_COLS = 512
LR_PAD = LANES
GLA_NSEG = 1 + len(GLA_LEVELS)
GLA_SPLIT = 2
GLA_FAST_CHUNK = 128
GLA_SAFE_LOG_DECAY = -60.0


def _log_sigmoid(z):
    return jnp.minimum(z, 0.0) - jnp.log(1.0 + jnp.exp(-jnp.abs(z)))


def _gla_constants():
    c = GLA_CHUNK
    t = np.arange(c)[:, None]
    u = np.arange(c)[None, :]
    seg = [(u <= t)]
    mask = [(u == t)]
    for m in GLA_LEVELS:
        blk_t, blk_u = t // m, u // m
        odd = (blk_t % 2) == 1
        p_odd = blk_t * m
        p_even = (blk_t + 1) * m
        seg.append(np.where(odd, (u > p_odd) & (u <= t), (u > t) & (u <= p_even)))
        mask.append(odd & (blk_u == blk_t - 1))
    seg = np.concatenate(seg, axis=0).astype(np.float32)
    seg = np.concatenate([seg] * GLA_SPLIT, axis=1)
    mask = np.stack(mask).astype(np.float32)
    assert mask.shape[0] == GLA_NSEG and seg.shape == (GLA_NSEG * c, GLA_SPLIT * c)
    fc = GLA_FAST_CHUNK
    tri = np.tril(np.ones((fc, fc), np.float32))
    tri = np.concatenate([tri] * GLA_SPLIT, axis=1)
    return seg, mask, tri


def _split_terms(g):
    parts = []
    rem = g
    for _ in range(GLA_SPLIT):
        p = rem.astype(BF16)
        parts.append(p)
        rem = rem - p.astype(F32)
    return jnp.concatenate(parts, axis=0)


def _decay_column(bl):
    return jnp.exp(jnp.broadcast_to(bl, (8, GLA_DK))).T[:, 0:1]


def _gla_finish(o, gn_ref, go, o_ref, rows, vs):
    ms = jnp.mean(o * o, axis=-1, keepdims=True)
    on = o * lax.rsqrt(ms + RMS_EPS) * gn_ref[...]
    o_ref[rows, vs] = (on * go).astype(BF16)


def _gla_cumsum(la_ref, tri_ref, b_ref, result):
    c = GLA_FAST_CHUNK
    lowest = None
    for ci in range(MIXER_TILE_M // c):
        rows = slice(ci * c, (ci + 1) * c)
        b = _dot(tri_ref[...], _split_terms(la_ref[rows, :]))
        b_ref[rows, :] = b
        total = jnp.min(b[c - 1:c])
        lowest = total if lowest is None else jnp.minimum(lowest, total)
        result["safe"] = lowest >= GLA_SAFE_LOG_DECAY
        yield


def _gla_tile_fast(q_ref, k_ref, v_ref, b_ref, go_ref, gn_ref, o_ref, state_in_ref, state_out_ref):
    c = GLA_FAST_CHUNK
    ti = lax.broadcasted_iota(jnp.int32, (c, c), 0)
    si = lax.broadcasted_iota(jnp.int32, (c, c), 1)
    causal = si <= ti
    chunks = range(MIXER_TILE_M // c)
    heads = range(GLA_HEADS)
    rows = [slice(ci * c, (ci + 1) * c) for ci in chunks]
    kcols = [slice(hd * GLA_DK, (hd + 1) * GLA_DK) for hd in heads]
    vcols = [slice(hd * GLA_DV, (hd + 1) * GLA_DV) for hd in heads]

    q_in, a, upd, d_col = {}, {}, {}, {}
    for ci in chunks:
        b = b_ref[rows[ci], :]
        b_last = b[c - 1:c]
        e_pos = jnp.exp(b)
        e_neg = jnp.exp(-b)
        e_last = jnp.exp(b_last)
        for hd in heads:
            ks = kcols[hd]
            q_in[ci, hd] = (q_ref[rows[ci], ks] * e_pos[:, ks]).astype(BF16)
            k_out = k_ref[rows[ci], ks] * e_neg[:, ks]
            a[ci, hd] = jnp.where(causal, _dot_t(q_in[ci, hd], k_out.astype(BF16)), 0.0).astype(BF16)
            k_dec_t = (k_out * e_last[:, ks]).T.astype(BF16)
            upd[ci, hd] = _dot(k_dec_t, v_ref[rows[ci], vcols[hd]])
            d_col[ci, hd] = _decay_column(b_last[:, ks])
        yield

    for hd in heads:
        state = state_in_ref[hd]
        for ci in chunks:
            v = v_ref[rows[ci], vcols[hd]]
            o = _dot(jnp.concatenate([q_in[ci, hd], a[ci, hd]], axis=1),
                     jnp.concatenate([state.astype(BF16), v], axis=0))
            state = d_col[ci, hd] * state + upd[ci, hd]
            _gla_finish(o, gn_ref, go_ref[rows[ci], vcols[hd]], o_ref, rows[ci], vcols[hd])
        state_out_ref[hd] = state
        yield


def _gla_tile_any_decay(q_ref, k_ref, v_ref, la_ref, go_ref, seg_ref, mask_ref, gn_ref, o_ref,
                        state_ref):
    c = GLA_CHUNK

    def chunk(ci, carry):
        rows = pl.ds(pl.multiple_of(ci * c, c), c)
        e_all = _dot(seg_ref[...], _split_terms(la_ref[rows, :]))
        b = e_all[0:c]
        b_last = b[c - 1:c]
        for hd in range(GLA_HEADS):
            ks = slice(hd * GLA_DK, (hd + 1) * GLA_DK)
            vs = slice(hd * GLA_DV, (hd + 1) * GLA_DV)
            q = q_ref[rows, ks]
            k = k_ref[rows, ks]
            v = v_ref[rows, vs]
            bh = b[:, ks]
            bl = b_last[:, ks]
            state = state_ref[hd]
            o = _dot((q * jnp.exp(bh)).astype(BF16), state.astype(BF16))
            a = _dot_t(q.astype(BF16), k.astype(BF16)) * mask_ref[0]
            for li in range(len(GLA_LEVELS)):
                e = jnp.exp(e_all[(li + 1) * c:(li + 2) * c, ks])
                a = a + _dot_t((q * e).astype(BF16), (k * e).astype(BF16)) * mask_ref[li + 1]
            o = o + _dot(a.astype(BF16), v)
            k_dec = k * jnp.exp(bl - bh)
            state_ref[hd] = _decay_column(bl) * state + _dot(k_dec.T.astype(BF16), v)
            _gla_finish(o, gn_ref, go_ref[rows, vs], o_ref, rows, vs)
        return carry

    lax.fori_loop(0, MIXER_TILE_M // c, chunk, 0)


LOG2E = 1.4426950408889634


def _swa_bias_table(slope_ref, bias_ref):
    blk = SWA_BLOCK
    qi = lax.broadcasted_iota(jnp.int32, (blk, blk), 0)
    kj = lax.broadcasted_iota(jnp.int32, (blk, blk), 1)
    dist = jnp.where(kj <= qi, qi - kj, qi + blk - kj).astype(F32)
    for head in range(SWA_HEADS):
        bias_ref[head] = (slope_ref[head] * LOG2E) * dist


def _swa_tile(sink_ref, q, k, v, kprev_ref, vprev_ref, bias_ref, first_tile, o_ref):
    blk = SWA_BLOCK
    hd = SWA_HEAD_DIM
    group = SWA_HEADS // SWA_KV_HEADS
    kvs = range(SWA_KV_HEADS)

    qi = lax.broadcasted_iota(jnp.int32, (blk, blk), 0)
    kj = lax.broadcasted_iota(jnp.int32, (blk, blk), 1)
    own = kj <= qi
    first_valid = (kj - qi) <= jnp.where(first_tile, 0, blk)
    lane = lax.broadcasted_iota(jnp.int32, (blk, LANES), 1)
    low = lane < hd

    def dup(x, kv):
        tile = x[:, (kv // 2) * LANES:(kv // 2 + 1) * LANES]
        rolled = pltpu.roll(tile, hd, 1)
        lo_half, hi_half = (tile, rolled) if kv % 2 == 0 else (rolled, tile)
        return jnp.where(low, lo_half, hi_half).astype(BF16)

    def dup_all(x):
        xf = x.astype(F32)
        return [dup(xf, kv) for kv in kvs]

    zero = jnp.zeros((blk, LANES), BF16)
    prev_k = [kprev_ref[kv] for kv in kvs]
    prev_v = [vprev_ref[kv] for kv in kvs]
    for bi in range(MIXER_TILE_M // blk):
        rows = slice(bi * blk, (bi + 1) * blk)
        cur_k = dup_all(k[rows, :])
        cur_v = dup_all(v[rows, :])
        for kv in kvs:
            lhs = []
            for pair in (2 * kv, 2 * kv + 1):
                x = q[rows, pair * LANES:(pair + 1) * LANES]
                lhs += [jnp.where(low, x, zero), jnp.where(low, zero, x)]
            keys = jnp.concatenate([prev_k[kv], cur_k[kv]], axis=0)
            s_all = _dot_t(jnp.concatenate(lhs, axis=0), keys)
            probs = []
            for gi in range(group):
                head = kv * group + gi
                sc = s_all[gi * blk:(gi + 1) * blk]
                s = jnp.where(own, sc[:, blk:], sc[:, :blk]) * (hd ** -0.5 * LOG2E) - bias_ref[head]
                if bi == 0:
                    s = jnp.where(first_valid, s, NEG_INF)
                sink = sink_ref[head] * LOG2E
                m = jnp.maximum(jnp.max(s, axis=-1, keepdims=True), sink)
                p = jnp.exp2(s - m)
                denom = jnp.sum(p, axis=-1, keepdims=True) + jnp.exp2(sink - m)
                pn = p * (1.0 / denom)
                probs.append(jnp.concatenate([jnp.where(own, 0.0, pn), jnp.where(own, pn, 0.0)],
                                             axis=1).astype(BF16))
            vals = jnp.concatenate([prev_v[kv], cur_v[kv]], axis=0)
            r = _dot(jnp.concatenate(probs, axis=0), vals)
            for pi, pair in enumerate((2 * kv, 2 * kv + 1)):
                r0 = r[(2 * pi) * blk:(2 * pi + 1) * blk]
                r1 = r[(2 * pi + 1) * blk:(2 * pi + 2) * blk]
                o_ref[rows, pair * LANES:(pair + 1) * LANES] = jnp.where(low, r0, r1).astype(BF16)
            yield
        prev_k, prev_v = cur_k, cur_v
    for kv in kvs:
        kprev_ref[kv] = prev_k[kv]
        vprev_ref[kv] = prev_v[kv]


def _row_views(w_ref, offsets, widths):
    return [w_ref.at[lo:lo + width, :] for lo, width in zip(offsets, widths)]


def _interleave(first, second, ratio):
    live_first, live_second = True, True
    while live_first or live_second:
        for _ in range(ratio):
            if live_first:
                live_first = next(first, "done") != "done"
        if live_second:
            live_second = next(second, "done") != "done"


def _mixer_kernel(slope_ref, sink_ref, x_ref, ada_ref,
                  w_in_t_ref,
                  wup_ref, bup_ref, seg_ref, mask_ref, tri_ref, gn_ref,
                  wa_ref, wb_ref, wo_ref, lng_ref, lnb_ref,
                  next_wg_ref, next_wu_ref, next_wd_ref,
                  out_ref, next_wg_out, next_wu_out, next_wd_out,
                  q_s, k_s, v_s, la_s, go_s, b_s, o_s, a_s, state_ref, state_new, kprev_s, vprev_s,
                  bias_ref):
    first_tile = pl.program_id(0) == 0
    for src, dst in ((next_wg_ref, next_wg_out), (next_wu_ref, next_wu_out), (next_wd_ref, next_wd_out)):
        dst[...] = src[...].astype(BF16)
    offsets = [sum(MIX_SIZES[:i]) for i in range(len(MIX_SIZES))]
    widths = list(MIX_SIZES)
    widths[3] = LR_PAD
    (wq_ref, wk_ref, wv_ref, wlr_ref, wgo_ref, wsq_ref, wsk_ref, wsv_ref, wga_ref,
     wgb_ref) = _row_views(w_in_t_ref, offsets, widths)

    def project(w_ref, lo=None):
        return _dot_t(h, w_ref[...] if lo is None else w_ref[lo:lo + step, :])

    @pl.when(first_tile)
    def _():
        state_ref[...] = jnp.zeros_like(state_ref)
        kprev_s[...] = jnp.zeros_like(kprev_s)
        vprev_s[...] = jnp.zeros_like(vprev_s)
        _swa_bias_table(slope_ref, bias_ref)

    x = x_ref[...]
    h = (x * (1.0 + _ada_slice(ada_ref, 4)) + _ada_slice(ada_ref, 3)).astype(BF16)
    step = MIXER_DOT_COLS
    col_blocks = range(0, D_MODEL, step)

    guard = {}
    lr = project(wlr_ref).astype(BF16)
    z = _dot(lr, wup_ref[...]) + bup_ref[...]
    la_s[...] = _log_sigmoid(z) * (1.0 / GLA_GATE_NORMALIZER)
    for _ in _gla_cumsum(la_s, tri_ref, b_s, guard):
        pass
    sq = _tie(project(wsq_ref).astype(BF16), b_s[...])
    sk = project(wsk_ref).astype(BF16)
    sv = project(wsv_ref).astype(BF16)

    gates = {}

    def gate(name, w_ref, lo):
        gates[name, lo] = _sigmoid(project(w_ref, lo))

    def projections_1():
        for lo in range(0, GLA_HEADS * GLA_DK, step):
            q_s[:, lo:lo + step] = project(wq_ref, lo) * (GLA_DK ** -0.5)
            yield
            k_s[:, lo:lo + step] = project(wk_ref, lo)
            yield
        for lo in col_blocks:
            v_s[:, lo:lo + step] = project(wv_ref, lo).astype(BF16)
            yield
        for lo in col_blocks:
            go = project(wgo_ref, lo)
            go_s[:, lo:lo + step] = go * _sigmoid(go)
            yield
        for lo in col_blocks:
            gate("a", wga_ref, lo)
            yield

    dense_1 = projections_1()
    next(dense_1)
    _interleave(_swa_tile(sink_ref, sq, sk, sv, kprev_s, vprev_s, bias_ref, first_tile, a_s),
                dense_1, ratio=2)

    def projections_2():
        for lo in col_blocks:
            gate("b", wgb_ref, lo)
            yield
        for lo in col_blocks:
            gates["yb", lo] = _dot(a_s[...], wb_ref[:, lo:lo + step])
            yield

    dense_2 = projections_2()
    next(dense_2)
    _interleave(_gla_tile_fast(q_s, k_s, v_s, b_s, go_s, gn_ref, o_s, state_ref, state_new),
                dense_2, ratio=2)
    safe = guard["safe"]

    @pl.when(jnp.logical_not(safe))
    def _():
        state_new[...] = state_ref[...]
        _gla_tile_any_decay(q_s, k_s, v_s, la_s, go_s, seg_ref, mask_ref, gn_ref, o_s, state_new)

    state_ref[...] = state_new[...]

    gate_a = jnp.concatenate([gates["a", lo] for lo in col_blocks], axis=1)
    gate_b_yb = jnp.concatenate([gates["b", lo] * gates["yb", lo] for lo in col_blocks], axis=1)
    rows_half = MIXER_TILE_M // 2
    for lo in (0, rows_half):
        rows = slice(lo, lo + rows_half)
        ya = _dot(o_s[rows, :], wa_ref[...])
        merged = (gate_a[rows, :] * ya + gate_b_yb[rows, :]).astype(BF16)
        y = _dot(merged, wo_ref[...])
        r = DEEPNORM_ALPHA * x[rows, :] + _ada_slice(ada_ref, 5) * y
        out_ref[rows, :] = _layer_norm(r, lng_ref[...], lnb_ref[...])


def _mixer(x, ada, w_in_t, wup, bup, gn, slopes, sinks, wa, wb, wo, ln_g, ln_b, next_ffn):
    tm = MIXER_TILE_M
    steps = SEQ // tm
    seg, mask, tri = _gla_constants()
    qk = GLA_HEADS * GLA_DK
    row = pl.BlockSpec((tm, D_MODEL), lambda i: (i, 0))
    smem = pl.BlockSpec(memory_space=pltpu.SMEM)
    up_rows = pl.BlockSpec((D_MODEL // steps, D_FF), lambda i: (i, 0))
    down_rows = pl.BlockSpec((2 * D_FF // steps, D_MODEL), lambda i: (i // 2, 0))

    def w(n):
        return _resident((D_MODEL, n))

    vec = _resident((1, D_MODEL))
    return pl.pallas_call(
        _mixer_kernel,
        grid=(steps,),
        in_specs=[smem, smem, row, _resident((1, N_ADA * D_MODEL)),
                  _resident((sum(MIX_SIZES), D_MODEL)),
                  _resident((LR_PAD, qk)), _resident((1, qk)),
                  _resident(seg.shape), _resident(mask.shape), _resident(tri.shape),
                  _resident((1, GLA_DV)),
                  w(D_MODEL), w(D_MODEL), w(D_MODEL), vec, vec,
                  up_rows, up_rows, down_rows],
        out_specs=[row, up_rows, up_rows, down_rows],
        out_shape=[jax.ShapeDtypeStruct((SEQ, D_MODEL), F32),
                   jax.ShapeDtypeStruct((D_MODEL, D_FF), BF16),
                   jax.ShapeDtypeStruct((D_MODEL, D_FF), BF16),
                   jax.ShapeDtypeStruct((D_FF, D_MODEL), BF16)],
        scratch_shapes=[pltpu.VMEM((tm, qk), F32), pltpu.VMEM((tm, qk), F32),
                        pltpu.VMEM((tm, D_MODEL), BF16), pltpu.VMEM((tm, qk), F32),
                        pltpu.VMEM((tm, D_MODEL), F32), pltpu.VMEM((tm, qk), F32),
                        pltpu.VMEM((tm, D_MODEL), BF16), pltpu.VMEM((tm, D_MODEL), BF16),
                        pltpu.VMEM((GLA_HEADS, GLA_DK, GLA_DV), F32),
                        pltpu.VMEM((GLA_HEADS, GLA_DK, GLA_DV), F32),
                        pltpu.VMEM((SWA_KV_HEADS, SWA_BLOCK, LANES), BF16),
                        pltpu.VMEM((SWA_KV_HEADS, SWA_BLOCK, LANES), BF16),
                        pltpu.VMEM((SWA_HEADS, SWA_BLOCK, SWA_BLOCK), F32)],
        compiler_params=_cparams(),
        name="mixer",
    )(slopes, sinks, x, ada, w_in_t, wup, bup,
      jnp.asarray(seg, BF16), jnp.asarray(mask, F32), jnp.asarray(tri, BF16), gn,
      wa, wb, wo, ln_g, ln_b, *next_ffn)


def _alibi_slopes(n):
    return 2.0 ** (-8.0 * jnp.arange(1, n + 1, dtype=jnp.float32) / n)


def kernel(x, c, w_ada, b_ada, ffn1_w_gate, ffn1_w_up, ffn1_w_down, ln1_g, ln1_b, w_in,
           w_gla_gate_up, b_gla_gate, gla_norm_g, w_branch_gla, swa_sinks, w_branch_swa, w_out,
           ln2_g, ln2_b, ffn2_w_gate, ffn2_w_up, ffn2_w_down, ln3_g, ln3_b):
    assert x.shape == (1, SEQ, D_MODEL) and w_ada.shape[0] == 1
    x2d = x.reshape(SEQ, D_MODEL)
    vec = lambda p: p.reshape(1, -1)

    ada = _ada(c, w_ada[0], b_ada[0])
    x1, w_in_t, wa, wb, wo = _ffn(
        x2d, ada, ffn1_w_gate[0], ffn1_w_up[0], ffn1_w_down[0], vec(ln1_g), vec(ln1_b), 0,
        mixer_weights=(jnp.swapaxes(w_in[0], 0, 1), w_branch_gla[0], w_branch_swa[0], w_out[0]))

    wup = jnp.pad(w_gla_gate_up[0], ((0, LR_PAD - GLA_GATE_RANK), (0, 0))).astype(BF16)
    x2, wg2, wu2, wd2 = _mixer(
        x1, ada, w_in_t, wup, vec(b_gla_gate), vec(gla_norm_g),
        _alibi_slopes(SWA_HEADS), swa_sinks[0], wa, wb, wo,
        vec(ln2_g), vec(ln2_b), (ffn2_w_gate[0], ffn2_w_up[0], ffn2_w_down[0]))

    out = _ffn(x2, ada, wg2, wu2, wd2, vec(ln3_g), vec(ln3_b), 6)
    return out.reshape(1, SEQ, D_MODEL)
```

```python
import functools

import numpy as np
import jax
import jax.numpy as jnp
from jax import lax
from jax.experimental import pallas as pl
from jax.experimental.pallas import tpu as pltpu

D_MODEL = 1024
SEQ = 16384
D_FF = 2816
N_ADA = 9
LN_EPS = 1e-5
RMS_EPS = 1e-6
NEG_INF = -1e30
DEEPNORM_ALPHA = 2.0 ** 0.25
MACARON_WEIGHT = 0.5

GLA_HEADS = 4
GLA_DK = 128
GLA_DV = 256
GLA_GATE_RANK = 16
GLA_GATE_NORMALIZER = 16.0
GLA_CHUNK = 64
GLA_LEVELS = (32, 16, 8, 4, 2, 1)

SWA_HEADS = 16
SWA_KV_HEADS = 4
SWA_HEAD_DIM = 64
SWA_BLOCK = 128

MIX_SIZES = (512, 512, 1024, 16, 1024, 1024, 256, 256, 1024, 1024)

LANES = 128
BF16_ROW_TILE = 16
V7X_VMEM_LIMIT_BYTES = 60 * 1024 * 1024

BF16 = jnp.bfloat16
F32 = jnp.float32


def _cparams():
    return pltpu.CompilerParams(dimension_semantics=("arbitrary",),
                                vmem_limit_bytes=V7X_VMEM_LIMIT_BYTES)


def _resident(shape):
    return pl.BlockSpec(shape, lambda i: (0,) * len(shape), pipeline_mode=pl.Buffered(1))


def _round_up(n, multiple):
    return -(-n // multiple) * multiple


def _dot(a, b):
    return jnp.dot(a, b, preferred_element_type=F32)


def _dot_t(a, b):
    return lax.dot_general(a, b, (((1,), (1,)), ((), ())), preferred_element_type=F32)


def _sigmoid(x):
    return 1.0 / (1.0 + jnp.exp(-x))


def _layer_norm(r, g, b):
    mu = jnp.mean(r, axis=-1, keepdims=True)
    d = r - mu
    var = jnp.mean(d * d, axis=-1, keepdims=True)
    return d * lax.rsqrt(var + LN_EPS) * g + b


def _tie(value, other):
    bits = pltpu.bitcast(other, jnp.uint32)
    words = [bits[r:r + 8, c:c + LANES]
             for r in range(0, other.shape[0], 8) for c in range(0, other.shape[1], LANES)]
    while len(words) > 1:
        words = [a | b for a, b in zip(words[0::2], words[1::2])] + words[len(words) & ~1:]
    zero = lax.shift_right_logical(lax.shift_right_logical(words[0], jnp.uint32(16)), jnp.uint32(16))
    zero = jnp.concatenate([zero, zero], axis=0).astype(jnp.int32).astype(F32).astype(value.dtype)
    tile = value[0:BF16_ROW_TILE, 0:LANES] + zero
    top = jnp.concatenate([tile, value[0:BF16_ROW_TILE, LANES:]], axis=1)
    return jnp.concatenate([top, value[BF16_ROW_TILE:, :]], axis=0)


def _ada_slice(ada_ref, idx):
    return ada_ref[:, idx * D_MODEL:(idx + 1) * D_MODEL]


ADA_CHUNK_N = 768
ADA_SLOTS = 4


def _ada_kernel(c_ref, w_hbm, b_ref, o_ref, stage_ref, sems):
    c = c_ref[...]
    s = (c * _sigmoid(c)).astype(BF16)
    s8 = jnp.broadcast_to(s, (8, D_MODEL))
    n = (N_ADA * D_MODEL) // ADA_CHUNK_N

    def copy(k):
        return pltpu.make_async_copy(w_hbm.at[:, pl.ds(k * ADA_CHUNK_N, ADA_CHUNK_N)],
                                     stage_ref.at[k % ADA_SLOTS], sems.at[k % ADA_SLOTS])

    for k in range(ADA_SLOTS - 1):
        copy(k).start()
    for k in range(n):
        if k + ADA_SLOTS - 1 < n:
            copy(k + ADA_SLOTS - 1).start()
        copy(k).wait()
        cols = slice(k * ADA_CHUNK_N, (k + 1) * ADA_CHUNK_N)
        y = _dot(s8, stage_ref[k % ADA_SLOTS].astype(BF16))
        o_ref[:, cols] = y[0:1, :] + b_ref[:, cols]


def _ada(c, w_ada, b_ada):
    n = N_ADA * D_MODEL
    return pl.pallas_call(
        _ada_kernel,
        grid=(1,),
        in_specs=[pl.BlockSpec((1, D_MODEL), lambda i: (0, 0)),
                  pl.BlockSpec(memory_space=pl.ANY),
                  pl.BlockSpec((1, n), lambda i: (0, 0))],
        out_specs=pl.BlockSpec((1, n), lambda i: (0, 0)),
        out_shape=jax.ShapeDtypeStruct((1, n), F32),
        scratch_shapes=[pltpu.VMEM((ADA_SLOTS, D_MODEL, ADA_CHUNK_N), F32),
                        pltpu.SemaphoreType.DMA((ADA_SLOTS,))],
        compiler_params=_cparams(),
        name="ada",
    )(c, w_ada, b_ada.reshape(1, n))


FFN_TILE_M = 512
FFN_SPLIT = 1536
FFN_UP_STAGE_ROWS = 128
FFN_DOWN_STAGE_ROWS = 352
FFN_STAGE_SLOTS = 4


def _load_cast_rows(src_hbm, dst_ref, stage_ref, sems, rows):
    n = src_hbm.shape[0] // rows
    slots = stage_ref.shape[0]

    def copy(c):
        return pltpu.make_async_copy(src_hbm.at[pl.ds(c * rows, rows), :], stage_ref.at[c % slots],
                                     sems.at[c % slots])

    for c in range(min(slots - 1, n)):
        copy(c).start()
    for c in range(n):
        if c + slots - 1 < n:
            copy(c + slots - 1).start()
        copy(c).wait()
        dst_ref[c * rows:(c + 1) * rows, :] = stage_ref[c % slots].astype(BF16)


def _ffn_kernel(x_ref, ada_ref, wg_ref, wu_ref, wd_ref, lng_ref, lnb_ref, *rest, ada_base,
                f32_weights):
    if f32_weights:
        *rest, wg_s, wu_s, wd_s, stage_up, stage_down, sem_up, sem_down = rest

        @pl.when(pl.program_id(0) == 0)
        def _():
            _load_cast_rows(wg_ref, wg_s, stage_up, sem_up, FFN_UP_STAGE_ROWS)
            _load_cast_rows(wu_ref, wu_s, stage_up, sem_up, FFN_UP_STAGE_ROWS)
            _load_cast_rows(wd_ref, wd_s, stage_down, sem_down, FFN_DOWN_STAGE_ROWS)

        wg_ref, wu_ref, wd_ref = wg_s, wu_s, wd_s
    n_cast = len(rest) // 2
    o_ref = rest[n_cast]
    for src, dst in zip(rest[:n_cast], rest[n_cast + 1:]):
        dst[...] = src[...].astype(BF16)
    x = x_ref[...]
    sh = _ada_slice(ada_ref, ada_base)
    sc = _ada_slice(ada_ref, ada_base + 1)
    gt = _ada_slice(ada_ref, ada_base + 2)
    h = (x * (1.0 + sc) + sh).astype(BF16)
    y = None
    for lo, hi in ((0, FFN_SPLIT), (FFN_SPLIT, D_FF)):
        g = _dot(h, wg_ref[:, lo:hi])
        u = _dot(h, wu_ref[:, lo:hi])
        a = (g * _sigmoid(g) * u).astype(BF16)
        part = _dot(a, wd_ref[lo:hi, :])
        y = part if y is None else y + part
    r = DEEPNORM_ALPHA * x + (MACARON_WEIGHT * gt) * y
    o_ref[...] = _layer_norm(r, lng_ref[...], lnb_ref[...])


def _ffn(x, ada, wg, wu, wd, ln_g, ln_b, ada_base, mixer_weights=None):
    tm = FFN_TILE_M
    steps = SEQ // tm
    row = pl.BlockSpec((tm, D_MODEL), lambda i: (i, 0))
    f32_weights = wg.dtype == F32
    scratch = []
    if f32_weights:
        weight_specs = [pl.BlockSpec(memory_space=pl.ANY)] * 3
        scratch = [pltpu.VMEM((D_MODEL, D_FF), BF16), pltpu.VMEM((D_MODEL, D_FF), BF16),
                   pltpu.VMEM((D_FF, D_MODEL), BF16),
                   pltpu.VMEM((FFN_STAGE_SLOTS, FFN_UP_STAGE_ROWS, D_FF), F32),
                   pltpu.VMEM((FFN_STAGE_SLOTS, FFN_DOWN_STAGE_ROWS, D_MODEL), F32),
                   pltpu.SemaphoreType.DMA((FFN_STAGE_SLOTS,)),
                   pltpu.SemaphoreType.DMA((FFN_STAGE_SLOTS,))]
    else:
        weight_specs = [_resident((D_MODEL, D_FF)), _resident((D_MODEL, D_FF)),
                        _resident((D_FF, D_MODEL))]
    in_specs = [row, _resident((1, N_ADA * D_MODEL)), *weight_specs,
                _resident((1, D_MODEL)), _resident((1, D_MODEL))]
    out_specs = [row]
    out_shape = [jax.ShapeDtypeStruct((SEQ, D_MODEL), F32)]
    operands = [x, ada, wg, wu, wd, ln_g, ln_b]
    for w in mixer_weights or ():
        rows = _round_up(pl.cdiv(w.shape[0], steps), BF16_ROW_TILE)
        last = pl.cdiv(w.shape[0], rows) - 1
        spec = pl.BlockSpec((rows, D_MODEL), lambda i, last=last: (jnp.minimum(i, last), 0))
        in_specs.append(spec)
        operands.append(w)
        out_specs.append(spec)
        out_shape.append(jax.ShapeDtypeStruct(w.shape, BF16))
    outs = pl.pallas_call(
        functools.partial(_ffn_kernel, ada_base=ada_base, f32_weights=f32_weights),
        grid=(steps,),
        in_specs=in_specs,
        out_specs=out_specs,
        out_shape=out_shape,
        scratch_shapes=scratch,
        compiler_params=_cparams(),
        name="ffn",
    )(*operands)
    return outs[0] if mixer_weights is None else outs


MIXER_TILE_M = 512
MIXER_DOT_COLS = 512
LR_PAD = LANES
GLA_NSEG = 1 + len(GLA_LEVELS)
GLA_SPLIT = 2
GLA_FAST_CHUNK = 128
GLA_SAFE_LOG_DECAY = -60.0


def _log_sigmoid(z):
    return jnp.minimum(z, 0.0) - jnp.log(1.0 + jnp.exp(-jnp.abs(z)))


def _gla_constants():
    c = GLA_CHUNK
    t = np.arange(c)[:, None]
    u = np.arange(c)[None, :]
    seg = [(u <= t)]
    mask = [(u == t)]
    for m in GLA_LEVELS:
        blk_t, blk_u = t // m, u // m
        odd = (blk_t % 2) == 1
        p_odd = blk_t * m
        p_even = (blk_t + 1) * m
        seg.append(np.where(odd, (u > p_odd) & (u <= t), (u > t) & (u <= p_even)))
        mask.append(odd & (blk_u == blk_t - 1))
    seg = np.concatenate(seg, axis=0).astype(np.float32)
    seg = np.concatenate([seg] * GLA_SPLIT, axis=1)
    mask = np.stack(mask).astype(np.float32)
    assert mask.shape[0] == GLA_NSEG and seg.shape == (GLA_NSEG * c, GLA_SPLIT * c)
    fc = GLA_FAST_CHUNK
    tri = np.tril(np.ones((fc, fc), np.float32))
    tri = np.concatenate([tri] * GLA_SPLIT, axis=1)
    return seg, mask, tri


def _split_terms(g):
    parts = []
    rem = g
    for _ in range(GLA_SPLIT):
        p = rem.astype(BF16)
        parts.append(p)
        rem = rem - p.astype(F32)
    return jnp.concatenate(parts, axis=0)


def _decay_column(bl):
    return jnp.exp(jnp.broadcast_to(bl, (8, GLA_DK))).T[:, 0:1]


def _gla_finish(o, gn_ref, go, o_ref, rows, vs):
    ms = jnp.mean(o * o, axis=-1, keepdims=True)
    on = o * lax.rsqrt(ms + RMS_EPS) * gn_ref[...]
    o_ref[rows, vs] = (on * go).astype(BF16)


def _gla_cumsum(la_ref, tri_ref, b_ref, result):
    c = GLA_FAST_CHUNK
    lowest = None
    for ci in range(MIXER_TILE_M // c):
        rows = slice(ci * c, (ci + 1) * c)
        b = _dot(tri_ref[...], _split_terms(la_ref[rows, :]))
        b_ref[rows, :] = b
        total = jnp.min(b[c - 1:c])
        lowest = total if lowest is None else jnp.minimum(lowest, total)
        result["safe"] = lowest >= GLA_SAFE_LOG_DECAY
        yield


def _gla_tile_fast(q_ref, k_ref, v_ref, b_ref, go_ref, gn_ref, o_ref, state_in_ref, state_out_ref):
    c = GLA_FAST_CHUNK
    ti = lax.broadcasted_iota(jnp.int32, (c, c), 0)
    si = lax.broadcasted_iota(jnp.int32, (c, c), 1)
    causal = si <= ti
    chunks = range(MIXER_TILE_M // c)
    heads = range(GLA_HEADS)
    rows = [slice(ci * c, (ci + 1) * c) for ci in chunks]
    kcols = [slice(hd * GLA_DK, (hd + 1) * GLA_DK) for hd in heads]
    vcols = [slice(hd * GLA_DV, (hd + 1) * GLA_DV) for hd in heads]

    q_in, a, upd, d_col = {}, {}, {}, {}
    for ci in chunks:
        b = b_ref[rows[ci], :]
        b_last = b[c - 1:c]
        e_pos = jnp.exp(b)
        e_neg = jnp.exp(-b)
        e_last = jnp.exp(b_last)
        for hd in heads:
            ks = kcols[hd]
            q_in[ci, hd] = (q_ref[rows[ci], ks] * e_pos[:, ks]).astype(BF16)
            k_out = k_ref[rows[ci], ks] * e_neg[:, ks]
            a[ci, hd] = jnp.where(causal, _dot_t(q_in[ci, hd], k_out.astype(BF16)), 0.0).astype(BF16)
            k_dec_t = (k_out * e_last[:, ks]).T.astype(BF16)
            upd[ci, hd] = _dot(k_dec_t, v_ref[rows[ci], vcols[hd]])
            d_col[ci, hd] = _decay_column(b_last[:, ks])
        yield

    for hd in heads:
        state = state_in_ref[hd]
        for ci in chunks:
            v = v_ref[rows[ci], vcols[hd]]
            o = _dot(jnp.concatenate([q_in[ci, hd], a[ci, hd]], axis=1),
                     jnp.concatenate([state.astype(BF16), v], axis=0))
            state = d_col[ci, hd] * state + upd[ci, hd]
            _gla_finish(o, gn_ref, go_ref[rows[ci], vcols[hd]], o_ref, rows[ci], vcols[hd])
        state_out_ref[hd] = state
        yield


def _gla_tile_any_decay(q_ref, k_ref, v_ref, la_ref, go_ref, seg_ref, mask_ref, gn_ref, o_ref,
                        state_ref):
    c = GLA_CHUNK

    def chunk(ci, carry):
        rows = pl.ds(pl.multiple_of(ci * c, c), c)
        e_all = _dot(seg_ref[...], _split_terms(la_ref[rows, :]))
        b = e_all[0:c]
        b_last = b[c - 1:c]
        for hd in range(GLA_HEADS):
            ks = slice(hd * GLA_DK, (hd + 1) * GLA_DK)
            vs = slice(hd * GLA_DV, (hd + 1) * GLA_DV)
            q = q_ref[rows, ks]
            k = k_ref[rows, ks]
            v = v_ref[rows, vs]
            bh = b[:, ks]
            bl = b_last[:, ks]
            state = state_ref[hd]
            o = _dot((q * jnp.exp(bh)).astype(BF16), state.astype(BF16))
            a = _dot_t(q.astype(BF16), k.astype(BF16)) * mask_ref[0]
            for li in range(len(GLA_LEVELS)):
                e = jnp.exp(e_all[(li + 1) * c:(li + 2) * c, ks])
                a = a + _dot_t((q * e).astype(BF16), (k * e).astype(BF16)) * mask_ref[li + 1]
            o = o + _dot(a.astype(BF16), v)
            k_dec = k * jnp.exp(bl - bh)
            state_ref[hd] = _decay_column(bl) * state + _dot(k_dec.T.astype(BF16), v)
            _gla_finish(o, gn_ref, go_ref[rows, vs], o_ref, rows, vs)
        return carry

    lax.fori_loop(0, MIXER_TILE_M // c, chunk, 0)


LOG2E = 1.4426950408889634


def _swa_bias_table(slope_ref, bias_ref):
    blk = SWA_BLOCK
    qi = lax.broadcasted_iota(jnp.int32, (blk, blk), 0)
    kj = lax.broadcasted_iota(jnp.int32, (blk, blk), 1)
    dist = jnp.where(kj <= qi, qi - kj, qi + blk - kj).astype(F32)
    for head in range(SWA_HEADS):
        bias_ref[head] = (slope_ref[head] * LOG2E) * dist


def _swa_tile(sink_ref, q, k, v, kprev_ref, vprev_ref, bias_ref, first_tile, o_ref):
    blk = SWA_BLOCK
    hd = SWA_HEAD_DIM
    group = SWA_HEADS // SWA_KV_HEADS
    kvs = range(SWA_KV_HEADS)

    qi = lax.broadcasted_iota(jnp.int32, (blk, blk), 0)
    kj = lax.broadcasted_iota(jnp.int32, (blk, blk), 1)
    own = kj <= qi
    first_valid = (kj - qi) <= jnp.where(first_tile, 0, blk)
    lane = lax.broadcasted_iota(jnp.int32, (blk, LANES), 1)
    low = lane < hd

    def dup(x, kv):
        tile = x[:, (kv // 2) * LANES:(kv // 2 + 1) * LANES]
        rolled = pltpu.roll(tile, hd, 1)
        lo_half, hi_half = (tile, rolled) if kv % 2 == 0 else (rolled, tile)
        return jnp.where(low, lo_half, hi_half).astype(BF16)

    def dup_all(x):
        xf = x.astype(F32)
        return [dup(xf, kv) for kv in kvs]

    zero = jnp.zeros((blk, LANES), BF16)
    prev_k = [kprev_ref[kv] for kv in kvs]
    prev_v = [vprev_ref[kv] for kv in kvs]
    for bi in range(MIXER_TILE_M // blk):
        rows = slice(bi * blk, (bi + 1) * blk)
        cur_k = dup_all(k[rows, :])
        cur_v = dup_all(v[rows, :])
        for kv in kvs:
            lhs = []
            for pair in (2 * kv, 2 * kv + 1):
                x = q[rows, pair * LANES:(pair + 1) * LANES]
                lhs += [jnp.where(low, x, zero), jnp.where(low, zero, x)]
            keys = jnp.concatenate([prev_k[kv], cur_k[kv]], axis=0)
            s_all = _dot_t(jnp.concatenate(lhs, axis=0), keys)
            probs = []
            for gi in range(group):
                head = kv * group + gi
                sc = s_all[gi * blk:(gi + 1) * blk]
                s = jnp.where(own, sc[:, blk:], sc[:, :blk]) * (hd ** -0.5 * LOG2E) - bias_ref[head]
                if bi == 0:
                    s = jnp.where(first_valid, s, NEG_INF)
                sink = sink_ref[head] * LOG2E
                m = jnp.maximum(jnp.max(s, axis=-1, keepdims=True), sink)
                p = jnp.exp2(s - m)
                denom = jnp.sum(p, axis=-1, keepdims=True) + jnp.exp2(sink - m)
                pn = p * (1.0 / denom)
                probs.append(jnp.concatenate([jnp.where(own, 0.0, pn), jnp.where(own, pn, 0.0)],
                                             axis=1).astype(BF16))
            vals = jnp.concatenate([prev_v[kv], cur_v[kv]], axis=0)
            r = _dot(jnp.concatenate(probs, axis=0), vals)
            for pi, pair in enumerate((2 * kv, 2 * kv + 1)):
                r0 = r[(2 * pi) * blk:(2 * pi + 1) * blk]
                r1 = r[(2 * pi + 1) * blk:(2 * pi + 2) * blk]
                o_ref[rows, pair * LANES:(pair + 1) * LANES] = jnp.where(low, r0, r1).astype(BF16)
            yield
        prev_k, prev_v = cur_k, cur_v
    for kv in kvs:
        kprev_ref[kv] = prev_k[kv]
        vprev_ref[kv] = prev_v[kv]


def _row_views(w_ref, offsets, widths):
    return [w_ref.at[lo:lo + width, :] for lo, width in zip(offsets, widths)]


def _interleave(first, second, ratio):
    live_first, live_second = True, True
    while live_first or live_second:
        for _ in range(ratio):
            if live_first:
                live_first = next(first, "done") != "done"
        if live_second:
            live_second = next(second, "done") != "done"


def _mixer_kernel(slope_ref, sink_ref, x_ref, ada_ref,
                  w_in_t_ref,
                  wup_ref, bup_ref, seg_ref, mask_ref, tri_ref, gn_ref,
                  wa_ref, wb_ref, wo_ref, lng_ref, lnb_ref,
                  next_wg_ref, next_wu_ref, next_wd_ref,
                  out_ref, next_wg_out, next_wu_out, next_wd_out,
                  q_s, k_s, v_s, la_s, go_s, b_s, o_s, a_s, state_ref, state_new, kprev_s, vprev_s,
                  bias_ref):
    first_tile = pl.program_id(0) == 0
    for src, dst in ((next_wg_ref, next_wg_out), (next_wu_ref, next_wu_out), (next_wd_ref, next_wd_out)):
        dst[...] = src[...].astype(BF16)
    offsets = [sum(MIX_SIZES[:i]) for i in range(len(MIX_SIZES))]
    widths = list(MIX_SIZES)
    widths[3] = LR_PAD
    (wq_ref, wk_ref, wv_ref, wlr_ref, wgo_ref, wsq_ref, wsk_ref, wsv_ref, wga_ref,
     wgb_ref) = _row_views(w_in_t_ref, offsets, widths)

    def project(w_ref, lo=None):
        return _dot_t(h, w_ref[...] if lo is None else w_ref[lo:lo + step, :])

    @pl.when(first_tile)
    def _():
        state_ref[...] = jnp.zeros_like(state_ref)
        kprev_s[...] = jnp.zeros_like(kprev_s)
        vprev_s[...] = jnp.zeros_like(vprev_s)
        _swa_bias_table(slope_ref, bias_ref)

    x = x_ref[...]
    h = (x * (1.0 + _ada_slice(ada_ref, 4)) + _ada_slice(ada_ref, 3)).astype(BF16)
    step = MIXER_DOT_COLS
    col_blocks = range(0, D_MODEL, step)

    guard = {}
    lr = project(wlr_ref).astype(BF16)
    z = _dot(lr, wup_ref[...]) + bup_ref[...]
    la_s[...] = _log_sigmoid(z) * (1.0 / GLA_GATE_NORMALIZER)
    for _ in _gla_cumsum(la_s, tri_ref, b_s, guard):
        pass
    sq = _tie(project(wsq_ref).astype(BF16), b_s[...])
    sk = project(wsk_ref).astype(BF16)
    sv = project(wsv_ref).astype(BF16)

    gates = {}

    def gate(name, w_ref, lo):
        gates[name, lo] = _sigmoid(project(w_ref, lo))

    def projections_1():
        for lo in range(0, GLA_HEADS * GLA_DK, step):
            q_s[:, lo:lo + step] = project(wq_ref, lo) * (GLA_DK ** -0.5)
            yield
            k_s[:, lo:lo + step] = project(wk_ref, lo)
            yield
        for lo in col_blocks:
            v_s[:, lo:lo + step] = project(wv_ref, lo).astype(BF16)
            yield
        for lo in col_blocks:
            go = project(wgo_ref, lo)
            go_s[:, lo:lo + step] = go * _sigmoid(go)
            yield
        for lo in col_blocks:
            gate("a", wga_ref, lo)
            yield

    dense_1 = projections_1()
    next(dense_1)
    _interleave(_swa_tile(sink_ref, sq, sk, sv, kprev_s, vprev_s, bias_ref, first_tile, a_s),
                dense_1, ratio=2)

    def projections_2():
        for lo in col_blocks:
            gate("b", wgb_ref, lo)
            yield
        for lo in col_blocks:
            gates["yb", lo] = _dot(a_s[...], wb_ref[:, lo:lo + step])
            yield

    dense_2 = projections_2()
    next(dense_2)
    _interleave(_gla_tile_fast(q_s, k_s, v_s, b_s, go_s, gn_ref, o_s, state_ref, state_new),
                dense_2, ratio=4)
    safe = guard["safe"]

    @pl.when(jnp.logical_not(safe))
    def _():
        state_new[...] = state_ref[...]
        _gla_tile_any_decay(q_s, k_s, v_s, la_s, go_s, seg_ref, mask_ref, gn_ref, o_s, state_new)

    state_ref[...] = state_new[...]

    gate_a = jnp.concatenate([gates["a", lo] for lo in col_blocks], axis=1)
    gate_b_yb = jnp.concatenate([gates["b", lo] * gates["yb", lo] for lo in col_blocks], axis=1)
    rows_half = MIXER_TILE_M // 2
    for lo in (0, rows_half):
        rows = slice(lo, lo + rows_half)
        ya = _dot(o_s[rows, :], wa_ref[...])
        merged = (gate_a[rows, :] * ya + gate_b_yb[rows, :]).astype(BF16)
        y = _dot(merged, wo_ref[...])
        r = DEEPNORM_ALPHA * x[rows, :] + _ada_slice(ada_ref, 5) * y
        out_ref[rows, :] = _layer_norm(r, lng_ref[...], lnb_ref[...])


def _mixer(x, ada, w_in_t, wup, bup, gn, slopes, sinks, wa, wb, wo, ln_g, ln_b, next_ffn):
    tm = MIXER_TILE_M
    steps = SEQ // tm
    seg, mask, tri = _gla_constants()
    qk = GLA_HEADS * GLA_DK
    row = pl.BlockSpec((tm, D_MODEL), lambda i: (i, 0))
    smem = pl.BlockSpec(memory_space=pltpu.SMEM)
    up_rows = pl.BlockSpec((D_MODEL // steps, D_FF), lambda i: (i, 0))
    down_rows = pl.BlockSpec((2 * D_FF // steps, D_MODEL), lambda i: (i // 2, 0))

    def w(n):
        return _resident((D_MODEL, n))

    vec = _resident((1, D_MODEL))
    return pl.pallas_call(
        _mixer_kernel,
        grid=(steps,),
        in_specs=[smem, smem, row, _resident((1, N_ADA * D_MODEL)),
                  _resident((sum(MIX_SIZES), D_MODEL)),
                  _resident((LR_PAD, qk)), _resident((1, qk)),
                  _resident(seg.shape), _resident(mask.shape), _resident(tri.shape),
                  _resident((1, GLA_DV)),
                  w(D_MODEL), w(D_MODEL), w(D_MODEL), vec, vec,
                  up_rows, up_rows, down_rows],
        out_specs=[row, up_rows, up_rows, down_rows],
        out_shape=[jax.ShapeDtypeStruct((SEQ, D_MODEL), F32),
                   jax.ShapeDtypeStruct((D_MODEL, D_FF), BF16),
                   jax.ShapeDtypeStruct((D_MODEL, D_FF), BF16),
                   jax.ShapeDtypeStruct((D_FF, D_MODEL), BF16)],
        scratch_shapes=[pltpu.VMEM((tm, qk), F32), pltpu.VMEM((tm, qk), F32),
                        pltpu.VMEM((tm, D_MODEL), BF16), pltpu.VMEM((tm, qk), F32),
                        pltpu.VMEM((tm, D_MODEL), F32), pltpu.VMEM((tm, qk), F32),
                        pltpu.VMEM((tm, D_MODEL), BF16), pltpu.VMEM((tm, D_MODEL), BF16),
                        pltpu.VMEM((GLA_HEADS, GLA_DK, GLA_DV), F32),
                        pltpu.VMEM((GLA_HEADS, GLA_DK, GLA_DV), F32),
                        pltpu.VMEM((SWA_KV_HEADS, SWA_BLOCK, LANES), BF16),
                        pltpu.VMEM((SWA_KV_HEADS, SWA_BLOCK, LANES), BF16),
                        pltpu.VMEM((SWA_HEADS, SWA_BLOCK, SWA_BLOCK), F32)],
        compiler_params=_cparams(),
        name="mixer",
    )(slopes, sinks, x, ada, w_in_t, wup, bup,
      jnp.asarray(seg, BF16), jnp.asarray(mask, F32), jnp.asarray(tri, BF16), gn,
      wa, wb, wo, ln_g, ln_b, *next_ffn)


def _alibi_slopes(n):
    return 2.0 ** (-8.0 * jnp.arange(1, n + 1, dtype=jnp.float32) / n)


def kernel(x, c, w_ada, b_ada, ffn1_w_gate, ffn1_w_up, ffn1_w_down, ln1_g, ln1_b, w_in,
           w_gla_gate_up, b_gla_gate, gla_norm_g, w_branch_gla, swa_sinks, w_branch_swa, w_out,
           ln2_g, ln2_b, ffn2_w_gate, ffn2_w_up, ffn2_w_down, ln3_g, ln3_b):
    assert x.shape == (1, SEQ, D_MODEL) and w_ada.shape[0] == 1
    x2d = x.reshape(SEQ, D_MODEL)
    vec = lambda p: p.reshape(1, -1)

    ada = _ada(c, w_ada[0], b_ada[0])
    x1, w_in_t, wa, wb, wo = _ffn(
        x2d, ada, ffn1_w_gate[0], ffn1_w_up[0], ffn1_w_down[0], vec(ln1_g), vec(ln1_b), 0,
        mixer_weights=(jnp.swapaxes(w_in[0], 0, 1), w_branch_gla[0], w_branch_swa[0], w_out[0]))

    wup = jnp.pad(w_gla_gate_up[0], ((0, LR_PAD - GLA_GATE_RANK), (0, 0))).astype(BF16)
    x2, wg2, wu2, wd2 = _mixer(
        x1, ada, w_in_t, wup, vec(b_gla_gate), vec(gla_norm_g),
        _alibi_slopes(SWA_HEADS), swa_sinks[0], wa, wb, wo,
        vec(ln2_g), vec(ln2_b), (ffn2_w_gate[0], ffn2_w_up[0], ffn2_w_down[0]))

    out = _ffn(x2, ada, wg2, wu2, wd2, vec(ln3_g), vec(ln3_b), 6)
    return out.reshape(1, SEQ, D_MODEL)
```

```python
import functools

import numpy as np
import jax
import jax.numpy as jnp
from jax import lax
from jax.experimental import pallas as pl
from jax.experimental.pallas import tpu as pltpu

D_MODEL = 1024
SEQ = 16384
D_FF = 2816
N_ADA = 9
LN_EPS = 1e-5
RMS_EPS = 1e-6
NEG_INF = -1e30
DEEPNORM_ALPHA = 2.0 ** 0.25
MACARON_WEIGHT = 0.5

GLA_HEADS = 4
GLA_DK = 128
GLA_DV = 256
GLA_GATE_RANK = 16
GLA_GATE_NORMALIZER = 16.0
GLA_CHUNK = 64
GLA_LEVELS = (32, 16, 8, 4, 2, 1)

SWA_HEADS = 16
SWA_KV_HEADS = 4
SWA_HEAD_DIM = 64
SWA_BLOCK = 128

MIX_SIZES = (512, 512, 1024, 16, 1024, 1024, 256, 256, 1024, 1024)

LANES = 128
BF16_ROW_TILE = 16
V7X_VMEM_LIMIT_BYTES = 60 * 1024 * 1024

BF16 = jnp.bfloat16
F32 = jnp.float32


def _cparams():
    return pltpu.CompilerParams(dimension_semantics=("arbitrary",),
                                vmem_limit_bytes=V7X_VMEM_LIMIT_BYTES)


def _resident(shape):
    return pl.BlockSpec(shape, lambda i: (0,) * len(shape), pipeline_mode=pl.Buffered(1))


def _round_up(n, multiple):
    return -(-n // multiple) * multiple


def _dot(a, b):
    return jnp.dot(a, b, preferred_element_type=F32)


def _dot_t(a, b):
    return lax.dot_general(a, b, (((1,), (1,)), ((), ())), preferred_element_type=F32)


def _sigmoid(x):
    return 1.0 / (1.0 + jnp.exp(-x))


def _layer_norm(r, g, b):
    mu = jnp.mean(r, axis=-1, keepdims=True)
    d = r - mu
    var = jnp.mean(d * d, axis=-1, keepdims=True)
    return d * lax.rsqrt(var + LN_EPS) * g + b


def _tie(value, other):
    bits = pltpu.bitcast(other, jnp.uint32)
    words = [bits[r:r + 8, c:c + LANES]
             for r in range(0, other.shape[0], 8) for c in range(0, other.shape[1], LANES)]
    while len(words) > 1:
        words = [a | b for a, b in zip(words[0::2], words[1::2])] + words[len(words) & ~1:]
    zero = lax.shift_right_logical(lax.shift_right_logical(words[0], jnp.uint32(16)), jnp.uint32(16))
    zero = jnp.concatenate([zero, zero], axis=0).astype(jnp.int32).astype(F32).astype(value.dtype)
    tile = value[0:BF16_ROW_TILE, 0:LANES] + zero
    top = jnp.concatenate([tile, value[0:BF16_ROW_TILE, LANES:]], axis=1)
    return jnp.concatenate([top, value[BF16_ROW_TILE:, :]], axis=0)


def _ada_slice(ada_ref, idx):
    return ada_ref[:, idx * D_MODEL:(idx + 1) * D_MODEL]


ADA_TILE_N = 1536


def _ada_kernel(c_ref, w_ref, b_ref, o_ref):
    c = c_ref[...]
    s = (c * _sigmoid(c)).astype(BF16)
    s8 = jnp.broadcast_to(s, (8, D_MODEL))
    y = _dot(s8, w_ref[...].astype(BF16))
    o_ref[...] = y[0:1, :] + b_ref[...]


def _ada(c, w_ada, b_ada):
    n = N_ADA * D_MODEL
    return pl.pallas_call(
        _ada_kernel,
        grid=(n // ADA_TILE_N,),
        in_specs=[pl.BlockSpec((1, D_MODEL), lambda i: (0, 0)),
                  pl.BlockSpec((D_MODEL, ADA_TILE_N), lambda i: (0, i)),
                  pl.BlockSpec((1, ADA_TILE_N), lambda i: (0, i))],
        out_specs=pl.BlockSpec((1, ADA_TILE_N), lambda i: (0, i)),
        out_shape=jax.ShapeDtypeStruct((1, n), F32),
        compiler_params=_cparams(),
        name="ada",
    )(c, w_ada, b_ada.reshape(1, n))


FFN_TILE_M = 512
FFN_SPLIT = 1536
FFN_UP_STAGE_ROWS = 128
FFN_DOWN_STAGE_ROWS = 352
FFN_STAGE_SLOTS = 4


def _load_cast_rows(src_hbm, dst_ref, stage_ref, sems, rows):
    n = src_hbm.shape[0] // rows
    slots = stage_ref.shape[0]

    def copy(c):
        return pltpu.make_async_copy(src_hbm.at[pl.ds(c * rows, rows), :], stage_ref.at[c % slots],
                                     sems.at[c % slots])

    for c in range(min(slots - 1, n)):
        copy(c).start()
    for c in range(n):
        if c + slots - 1 < n:
            copy(c + slots - 1).start()
        copy(c).wait()
        dst_ref[c * rows:(c + 1) * rows, :] = stage_ref[c % slots].astype(BF16)


def _ffn_kernel(x_ref, ada_ref, wg_ref, wu_ref, wd_ref, lng_ref, lnb_ref, *rest, ada_base,
                f32_weights):
    if f32_weights:
        *rest, wg_s, wu_s, wd_s, stage_up, stage_down, sem_up, sem_down = rest

        @pl.when(pl.program_id(0) == 0)
        def _():
            _load_cast_rows(wg_ref, wg_s, stage_up, sem_up, FFN_UP_STAGE_ROWS)
            _load_cast_rows(wu_ref, wu_s, stage_up, sem_up, FFN_UP_STAGE_ROWS)
            _load_cast_rows(wd_ref, wd_s, stage_down, sem_down, FFN_DOWN_STAGE_ROWS)

        wg_ref, wu_ref, wd_ref = wg_s, wu_s, wd_s
    n_cast = len(rest) // 2
    o_ref = rest[n_cast]
    for src, dst in zip(rest[:n_cast], rest[n_cast + 1:]):
        dst[...] = src[...].astype(BF16)
    x = x_ref[...]
    sh = _ada_slice(ada_ref, ada_base)
    sc = _ada_slice(ada_ref, ada_base + 1)
    gt = _ada_slice(ada_ref, ada_base + 2)
    h = (x * (1.0 + sc) + sh).astype(BF16)
    y = None
    for lo, hi in ((0, FFN_SPLIT), (FFN_SPLIT, D_FF)):
        g = _dot(h, wg_ref[:, lo:hi])
        u = _dot(h, wu_ref[:, lo:hi])
        a = (g * _sigmoid(g) * u).astype(BF16)
        part = _dot(a, wd_ref[lo:hi, :])
        y = part if y is None else y + part
    r = DEEPNORM_ALPHA * x + (MACARON_WEIGHT * gt) * y
    o_ref[...] = _layer_norm(r, lng_ref[...], lnb_ref[...])


def _ffn(x, ada, wg, wu, wd, ln_g, ln_b, ada_base, mixer_weights=None):
    tm = FFN_TILE_M
    steps = SEQ // tm
    row = pl.BlockSpec((tm, D_MODEL), lambda i: (i, 0))
    f32_weights = wg.dtype == F32
    scratch = []
    if f32_weights:
        weight_specs = [pl.BlockSpec(memory_space=pl.ANY)] * 3
        scratch = [pltpu.VMEM((D_MODEL, D_FF), BF16), pltpu.VMEM((D_MODEL, D_FF), BF16),
                   pltpu.VMEM((D_FF, D_MODEL), BF16),
                   pltpu.VMEM((FFN_STAGE_SLOTS, FFN_UP_STAGE_ROWS, D_FF), F32),
                   pltpu.VMEM((FFN_STAGE_SLOTS, FFN_DOWN_STAGE_ROWS, D_MODEL), F32),
                   pltpu.SemaphoreType.DMA((FFN_STAGE_SLOTS,)),
                   pltpu.SemaphoreType.DMA((FFN_STAGE_SLOTS,))]
    else:
        weight_specs = [_resident((D_MODEL, D_FF)), _resident((D_MODEL, D_FF)),
                        _resident((D_FF, D_MODEL))]
    in_specs = [row, _resident((1, N_ADA * D_MODEL)), *weight_specs,
                _resident((1, D_MODEL)), _resident((1, D_MODEL))]
    out_specs = [row]
    out_shape = [jax.ShapeDtypeStruct((SEQ, D_MODEL), F32)]
    operands = [x, ada, wg, wu, wd, ln_g, ln_b]
    for w in mixer_weights or ():
        rows = _round_up(pl.cdiv(w.shape[0], steps), BF16_ROW_TILE)
        last = pl.cdiv(w.shape[0], rows) - 1
        spec = pl.BlockSpec((rows, D_MODEL), lambda i, last=last: (jnp.minimum(i, last), 0))
        in_specs.append(spec)
        operands.append(w)
        out_specs.append(spec)
        out_shape.append(jax.ShapeDtypeStruct(w.shape, BF16))
    outs = pl.pallas_call(
        functools.partial(_ffn_kernel, ada_base=ada_base, f32_weights=f32_weights),
        grid=(steps,),
        in_specs=in_specs,
        out_specs=out_specs,
        out_shape=out_shape,
        scratch_shapes=scratch,
        compiler_params=_cparams(),
        name="ffn",
    )(*operands)
    return outs[0] if mixer_weights is None else outs


MIXER_TILE_M = 512
MIXER_DOT_COLS = 512
LR_PAD = LANES
GLA_NSEG = 1 + len(GLA_LEVELS)
GLA_SPLIT = 2
GLA_FAST_CHUNK = 128
GLA_SAFE_LOG_DECAY = -60.0


def _log_sigmoid(z):
    return jnp.minimum(z, 0.0) - jnp.log(1.0 + jnp.exp(-jnp.abs(z)))


def _gla_constants():
    c = GLA_CHUNK
    t = np.arange(c)[:, None]
    u = np.arange(c)[None, :]
    seg = [(u <= t)]
    mask = [(u == t)]
    for m in GLA_LEVELS:
        blk_t, blk_u = t // m, u // m
        odd = (blk_t % 2) == 1
        p_odd = blk_t * m
        p_even = (blk_t + 1) * m
        seg.append(np.where(odd, (u > p_odd) & (u <= t), (u > t) & (u <= p_even)))
        mask.append(odd & (blk_u == blk_t - 1))
    seg = np.concatenate(seg, axis=0).astype(np.float32)
    seg = np.concatenate([seg] * GLA_SPLIT, axis=1)
    mask = np.stack(mask).astype(np.float32)
    assert mask.shape[0] == GLA_NSEG and seg.shape == (GLA_NSEG * c, GLA_SPLIT * c)
    fc = GLA_FAST_CHUNK
    tri = np.tril(np.ones((fc, fc), np.float32))
    tri = np.concatenate([tri] * GLA_SPLIT, axis=1)
    return seg, mask, tri


def _split_terms(g):
    parts = []
    rem = g
    for _ in range(GLA_SPLIT):
        p = rem.astype(BF16)
        parts.append(p)
        rem = rem - p.astype(F32)
    return jnp.concatenate(parts, axis=0)


def _decay_column(bl):
    return jnp.exp(jnp.broadcast_to(bl, (8, GLA_DK))).T[:, 0:1]


def _gla_finish(o, gn_ref, go, o_ref, rows, vs):
    ms = jnp.mean(o * o, axis=-1, keepdims=True)
    on = o * lax.rsqrt(ms + RMS_EPS) * gn_ref[...]
    o_ref[rows, vs] = (on * go).astype(BF16)


def _gla_cumsum(la_ref, tri_ref, b_ref, result):
    c = GLA_FAST_CHUNK
    lowest = None
    for ci in range(MIXER_TILE_M // c):
        rows = slice(ci * c, (ci + 1) * c)
        b = _dot(tri_ref[...], _split_terms(la_ref[rows, :]))
        b_ref[rows, :] = b
        total = jnp.min(b[c - 1:c])
        lowest = total if lowest is None else jnp.minimum(lowest, total)
        result["safe"] = lowest >= GLA_SAFE_LOG_DECAY
        yield


def _gla_tile_fast(q_ref, k_ref, v_ref, b_ref, go_ref, gn_ref, o_ref, state_in_ref, state_out_ref):
    c = GLA_FAST_CHUNK
    ti = lax.broadcasted_iota(jnp.int32, (c, c), 0)
    si = lax.broadcasted_iota(jnp.int32, (c, c), 1)
    causal = si <= ti
    chunks = range(MIXER_TILE_M // c)
    heads = range(GLA_HEADS)
    rows = [slice(ci * c, (ci + 1) * c) for ci in chunks]
    kcols = [slice(hd * GLA_DK, (hd + 1) * GLA_DK) for hd in heads]
    vcols = [slice(hd * GLA_DV, (hd + 1) * GLA_DV) for hd in heads]

    q_in, a, upd, d_col = {}, {}, {}, {}
    for ci in chunks:
        b = b_ref[rows[ci], :]
        b_last = b[c - 1:c]
        e_pos = jnp.exp(b)
        e_neg = jnp.exp(-b)
        e_last = jnp.exp(b_last)
        for hd in heads:
            ks = kcols[hd]
            q_in[ci, hd] = (q_ref[rows[ci], ks] * e_pos[:, ks]).astype(BF16)
            k_out = k_ref[rows[ci], ks] * e_neg[:, ks]
            a[ci, hd] = jnp.where(causal, _dot_t(q_in[ci, hd], k_out.astype(BF16)), 0.0).astype(BF16)
            k_dec_t = (k_out * e_last[:, ks]).T.astype(BF16)
            upd[ci, hd] = _dot(k_dec_t, v_ref[rows[ci], vcols[hd]])
            d_col[ci, hd] = _decay_column(b_last[:, ks])
        yield

    for hd in heads:
        state = state_in_ref[hd]
        for ci in chunks:
            v = v_ref[rows[ci], vcols[hd]]
            o = _dot(jnp.concatenate([q_in[ci, hd], a[ci, hd]], axis=1),
                     jnp.concatenate([state.astype(BF16), v], axis=0))
            state = d_col[ci, hd] * state + upd[ci, hd]
            _gla_finish(o, gn_ref, go_ref[rows[ci], vcols[hd]], o_ref, rows[ci], vcols[hd])
        state_out_ref[hd] = state
        yield


def _gla_tile_any_decay(q_ref, k_ref, v_ref, la_ref, go_ref, seg_ref, mask_ref, gn_ref, o_ref,
                        state_ref):
    c = GLA_CHUNK

    def chunk(ci, carry):
        rows = pl.ds(pl.multiple_of(ci * c, c), c)
        e_all = _dot(seg_ref[...], _split_terms(la_ref[rows, :]))
        b = e_all[0:c]
        b_last = b[c - 1:c]
        for hd in range(GLA_HEADS):
            ks = slice(hd * GLA_DK, (hd + 1) * GLA_DK)
            vs = slice(hd * GLA_DV, (hd + 1) * GLA_DV)
            q = q_ref[rows, ks]
            k = k_ref[rows, ks]
            v = v_ref[rows, vs]
            bh = b[:, ks]
            bl = b_last[:, ks]
            state = state_ref[hd]
            o = _dot((q * jnp.exp(bh)).astype(BF16), state.astype(BF16))
            a = _dot_t(q.astype(BF16), k.astype(BF16)) * mask_ref[0]
            for li in range(len(GLA_LEVELS)):
                e = jnp.exp(e_all[(li + 1) * c:(li + 2) * c, ks])
                a = a + _dot_t((q * e).astype(BF16), (k * e).astype(BF16)) * mask_ref[li + 1]
            o = o + _dot(a.astype(BF16), v)
            k_dec = k * jnp.exp(bl - bh)
            state_ref[hd] = _decay_column(bl) * state + _dot(k_dec.T.astype(BF16), v)
            _gla_finish(o, gn_ref, go_ref[rows, vs], o_ref, rows, vs)
        return carry

    lax.fori_loop(0, MIXER_TILE_M // c, chunk, 0)


LOG2E = 1.4426950408889634


def _swa_bias_table(slope_ref, bias_ref):
    blk = SWA_BLOCK
    qi = lax.broadcasted_iota(jnp.int32, (blk, blk), 0)
    kj = lax.broadcasted_iota(jnp.int32, (blk, blk), 1)
    dist = jnp.where(kj <= qi, qi - kj, qi + blk - kj).astype(F32)
    for head in range(SWA_HEADS):
        bias_ref[head] = (slope_ref[head] * LOG2E) * dist


def _swa_tile(sink_ref, q, k, v, kprev_ref, vprev_ref, bias_ref, first_tile, o_ref):
    blk = SWA_BLOCK
    hd = SWA_HEAD_DIM
    group = SWA_HEADS // SWA_KV_HEADS
    kvs = range(SWA_KV_HEADS)

    qi = lax.broadcasted_iota(jnp.int32, (blk, blk), 0)
    kj = lax.broadcasted_iota(jnp.int32, (blk, blk), 1)
    own = kj <= qi
    first_valid = (kj - qi) <= jnp.where(first_tile, 0, blk)
    lane = lax.broadcasted_iota(jnp.int32, (blk, LANES), 1)
    low = lane < hd

    def dup(x, kv):
        tile = x[:, (kv // 2) * LANES:(kv // 2 + 1) * LANES]
        rolled = pltpu.roll(tile, hd, 1)
        lo_half, hi_half = (tile, rolled) if kv % 2 == 0 else (rolled, tile)
        return jnp.where(low, lo_half, hi_half).astype(BF16)

    def dup_all(x):
        xf = x.astype(F32)
        return [dup(xf, kv) for kv in kvs]

    zero = jnp.zeros((blk, LANES), BF16)
    prev_k = [kprev_ref[kv] for kv in kvs]
    prev_v = [vprev_ref[kv] for kv in kvs]
    for bi in range(MIXER_TILE_M // blk):
        rows = slice(bi * blk, (bi + 1) * blk)
        cur_k = dup_all(k[rows, :])
        cur_v = dup_all(v[rows, :])
        for kv in kvs:
            lhs = []
            for pair in (2 * kv, 2 * kv + 1):
                x = q[rows, pair * LANES:(pair + 1) * LANES]
                lhs += [jnp.where(low, x, zero), jnp.where(low, zero, x)]
            keys = jnp.concatenate([prev_k[kv], cur_k[kv]], axis=0)
            s_all = _dot_t(jnp.concatenate(lhs, axis=0), keys)
            probs = []
            for gi in range(group):
                head = kv * group + gi
                sc = s_all[gi * blk:(gi + 1) * blk]
                s = jnp.where(own, sc[:, blk:], sc[:, :blk]) * (hd ** -0.5 * LOG2E) - bias_ref[head]
                if bi == 0:
                    s = jnp.where(first_valid, s, NEG_INF)
                sink = sink_ref[head] * LOG2E
                m = jnp.maximum(jnp.max(s, axis=-1, keepdims=True), sink)
                p = jnp.exp2(s - m)
                denom = jnp.sum(p, axis=-1, keepdims=True) + jnp.exp2(sink - m)
                pn = p * (1.0 / denom)
                probs.append(jnp.concatenate([jnp.where(own, 0.0, pn), jnp.where(own, pn, 0.0)],
                                             axis=1).astype(BF16))
            vals = jnp.concatenate([prev_v[kv], cur_v[kv]], axis=0)
            r = _dot(jnp.concatenate(probs, axis=0), vals)
            for pi, pair in enumerate((2 * kv, 2 * kv + 1)):
                r0 = r[(2 * pi) * blk:(2 * pi + 1) * blk]
                r1 = r[(2 * pi + 1) * blk:(2 * pi + 2) * blk]
                o_ref[rows, pair * LANES:(pair + 1) * LANES] = jnp.where(low, r0, r1).astype(BF16)
            yield
        prev_k, prev_v = cur_k, cur_v
    for kv in kvs:
        kprev_ref[kv] = prev_k[kv]
        vprev_ref[kv] = prev_v[kv]


def _row_views(w_ref, offsets, widths):
    return [w_ref.at[lo:lo + width, :] for lo, width in zip(offsets, widths)]


def _interleave(first, second, ratio):
    live_first, live_second = True, True
    while live_first or live_second:
        for _ in range(ratio):
            if live_first:
                live_first = next(first, "done") != "done"
        if live_second:
            live_second = next(second, "done") != "done"


def _mixer_kernel(slope_ref, sink_ref, x_ref, ada_ref,
                  w_in_t_ref,
                  wup_ref, bup_ref, seg_ref, mask_ref, tri_ref, gn_ref,
                  wa_ref, wb_ref, wo_ref, lng_ref, lnb_ref,
                  out_ref,
                  q_s, k_s, v_s, la_s, go_s, b_s, o_s, a_s, state_ref, state_new, kprev_s, vprev_s,
                  bias_ref):
    first_tile = pl.program_id(0) == 0
    offsets = [sum(MIX_SIZES[:i]) for i in range(len(MIX_SIZES))]
    widths = list(MIX_SIZES)
    widths[3] = LR_PAD
    (wq_ref, wk_ref, wv_ref, wlr_ref, wgo_ref, wsq_ref, wsk_ref, wsv_ref, wga_ref,
     wgb_ref) = _row_views(w_in_t_ref, offsets, widths)

    def project(w_ref, lo=None):
        return _dot_t(h, w_ref[...] if lo is None else w_ref[lo:lo + step, :])

    @pl.when(first_tile)
    def _():
        state_ref[...] = jnp.zeros_like(state_ref)
        kprev_s[...] = jnp.zeros_like(kprev_s)
        vprev_s[...] = jnp.zeros_like(vprev_s)
        _swa_bias_table(slope_ref, bias_ref)

    x = x_ref[...]
    h = (x * (1.0 + _ada_slice(ada_ref, 4)) + _ada_slice(ada_ref, 3)).astype(BF16)
    step = MIXER_DOT_COLS
    col_blocks = range(0, D_MODEL, step)

    guard = {}
    lr = project(wlr_ref).astype(BF16)
    z = _dot(lr, wup_ref[...]) + bup_ref[...]
    la_s[...] = _log_sigmoid(z) * (1.0 / GLA_GATE_NORMALIZER)
    for _ in _gla_cumsum(la_s, tri_ref, b_s, guard):
        pass
    sq = _tie(project(wsq_ref).astype(BF16), b_s[...])
    sk = project(wsk_ref).astype(BF16)
    sv = project(wsv_ref).astype(BF16)

    gates = {}

    def gate(name, w_ref, lo):
        gates[name, lo] = _sigmoid(project(w_ref, lo))

    def projections_1():
        for lo in range(0, GLA_HEADS * GLA_DK, step):
            q_s[:, lo:lo + step] = project(wq_ref, lo) * (GLA_DK ** -0.5)
            yield
            k_s[:, lo:lo + step] = project(wk_ref, lo)
            yield
        for lo in col_blocks:
            v_s[:, lo:lo + step] = project(wv_ref, lo).astype(BF16)
            yield
        for lo in col_blocks:
            go = project(wgo_ref, lo)
            go_s[:, lo:lo + step] = go * _sigmoid(go)
            yield
        for lo in col_blocks:
            gate("a", wga_ref, lo)
            yield

    dense_1 = projections_1()
    next(dense_1)
    _interleave(_swa_tile(sink_ref, sq, sk, sv, kprev_s, vprev_s, bias_ref, first_tile, a_s),
                dense_1, ratio=2)

    def projections_2():
        for lo in col_blocks:
            gate("b", wgb_ref, lo)
            yield
        for lo in col_blocks:
            gates["yb", lo] = _dot(a_s[...], wb_ref[:, lo:lo + step])
            yield

    dense_2 = projections_2()
    next(dense_2)
    _interleave(_gla_tile_fast(q_s, k_s, v_s, b_s, go_s, gn_ref, o_s, state_ref, state_new),
                dense_2, ratio=4)
    safe = guard["safe"]

    @pl.when(jnp.logical_not(safe))
    def _():
        state_new[...] = state_ref[...]
        _gla_tile_any_decay(q_s, k_s, v_s, la_s, go_s, seg_ref, mask_ref, gn_ref, o_s, state_new)

    state_ref[...] = state_new[...]

    gate_a = jnp.concatenate([gates["a", lo] for lo in col_blocks], axis=1)
    gate_b_yb = jnp.concatenate([gates["b", lo] * gates["yb", lo] for lo in col_blocks], axis=1)
    rows_half = MIXER_TILE_M // 2
    for lo in (0, rows_half):
        rows = slice(lo, lo + rows_half)
        ya = _dot(o_s[rows, :], wa_ref[...])
        merged = (gate_a[rows, :] * ya + gate_b_yb[rows, :]).astype(BF16)
        y = _dot(merged, wo_ref[...])
        r = DEEPNORM_ALPHA * x[rows, :] + _ada_slice(ada_ref, 5) * y
        out_ref[rows, :] = _layer_norm(r, lng_ref[...], lnb_ref[...])


def _mixer(x, ada, w_in_t, wup, bup, gn, slopes, sinks, wa, wb, wo, ln_g, ln_b):
    tm = MIXER_TILE_M
    steps = SEQ // tm
    seg, mask, tri = _gla_constants()
    qk = GLA_HEADS * GLA_DK
    row = pl.BlockSpec((tm, D_MODEL), lambda i: (i, 0))
    smem = pl.BlockSpec(memory_space=pltpu.SMEM)

    def w(n):
        return _resident((D_MODEL, n))

    vec = _resident((1, D_MODEL))
    return pl.pallas_call(
        _mixer_kernel,
        grid=(steps,),
        in_specs=[smem, smem, row, _resident((1, N_ADA * D_MODEL)),
                  _resident((sum(MIX_SIZES), D_MODEL)),
                  _resident((LR_PAD, qk)), _resident((1, qk)),
                  _resident(seg.shape), _resident(mask.shape), _resident(tri.shape),
                  _resident((1, GLA_DV)),
                  w(D_MODEL), w(D_MODEL), w(D_MODEL), vec, vec],
        out_specs=row,
        out_shape=jax.ShapeDtypeStruct((SEQ, D_MODEL), F32),
        scratch_shapes=[pltpu.VMEM((tm, qk), F32), pltpu.VMEM((tm, qk), F32),
                        pltpu.VMEM((tm, D_MODEL), BF16), pltpu.VMEM((tm, qk), F32),
                        pltpu.VMEM((tm, D_MODEL), F32), pltpu.VMEM((tm, qk), F32),
                        pltpu.VMEM((tm, D_MODEL), BF16), pltpu.VMEM((tm, D_MODEL), BF16),
                        pltpu.VMEM((GLA_HEADS, GLA_DK, GLA_DV), F32),
                        pltpu.VMEM((GLA_HEADS, GLA_DK, GLA_DV), F32),
                        pltpu.VMEM((SWA_KV_HEADS, SWA_BLOCK, LANES), BF16),
                        pltpu.VMEM((SWA_KV_HEADS, SWA_BLOCK, LANES), BF16),
                        pltpu.VMEM((SWA_HEADS, SWA_BLOCK, SWA_BLOCK), F32)],
        compiler_params=_cparams(),
        name="mixer",
    )(slopes, sinks, x, ada, w_in_t, wup, bup,
      jnp.asarray(seg, BF16), jnp.asarray(mask, F32), jnp.asarray(tri, BF16), gn,
      wa, wb, wo, ln_g, ln_b)


def _alibi_slopes(n):
    return 2.0 ** (-8.0 * jnp.arange(1, n + 1, dtype=jnp.float32) / n)


def kernel(x, c, w_ada, b_ada, ffn1_w_gate, ffn1_w_up, ffn1_w_down, ln1_g, ln1_b, w_in,
           w_gla_gate_up, b_gla_gate, gla_norm_g, w_branch_gla, swa_sinks, w_branch_swa, w_out,
           ln2_g, ln2_b, ffn2_w_gate, ffn2_w_up, ffn2_w_down, ln3_g, ln3_b):
    assert x.shape == (1, SEQ, D_MODEL) and w_ada.shape[0] == 1
    x2d = x.reshape(SEQ, D_MODEL)
    vec = lambda p: p.reshape(1, -1)

    ada = _ada(c, w_ada[0], b_ada[0])
    x1, w_in_t, wa, wb, wo = _ffn(
        x2d, ada, ffn1_w_gate[0], ffn1_w_up[0], ffn1_w_down[0], vec(ln1_g), vec(ln1_b), 0,
        mixer_weights=(jnp.swapaxes(w_in[0], 0, 1), w_branch_gla[0], w_branch_swa[0], w_out[0]))

    wup = jnp.pad(w_gla_gate_up[0], ((0, LR_PAD - GLA_GATE_RANK), (0, 0))).astype(BF16)
    x2 = _mixer(x1, ada, w_in_t, wup, vec(b_gla_gate), vec(gla_norm_g),
                _alibi_slopes(SWA_HEADS), swa_sinks[0], wa, wb, wo, vec(ln2_g), vec(ln2_b))

    out = _ffn(x2, ada, ffn2_w_gate[0], ffn2_w_up[0], ffn2_w_down[0], vec(ln3_g), vec(ln3_b), 6)
    return out.reshape(1, SEQ, D_MODEL)
```
